```python
import jax, jax.numpy as jnp
from jax import lax
import numpy as np

D_MODEL = 1024
BATCH = 8
SEQ = 4096
DEPTH = 1

GRID_W = 64
HEAD_DIM = 64
ATTN_HEADS = 8
ATTN_KV_HEADS = 2
ATTN_GROUP = ATTN_HEADS // ATTN_KV_HEADS
ROPE_THETA = 10000.0
Q_BLOCK = 128
HGRN_HEADS = 8
HGRN_KEY_DIM = 64
HGRN_VAL_DIM = 64
HGRN_CHUNK = 32
ATTN_WIDTH = ATTN_HEADS * HEAD_DIM
KV_WIDTH = ATTN_KV_HEADS * HEAD_DIM
HGRN_KW = HGRN_HEADS * HGRN_KEY_DIM
HGRN_VW = HGRN_HEADS * HGRN_VAL_DIM
D_FF = 2816
NORM_EPS = 1e-6
IN_SPLITS = (ATTN_WIDTH, KV_WIDTH, KV_WIDTH, HGRN_KW, HGRN_KW, HGRN_KW, HGRN_VW, HGRN_VW, D_MODEL, D_MODEL)
IN_WIDTH = 3 * KV_WIDTH + ATTN_WIDTH - KV_WIDTH + 3 * HGRN_KW + 2 * HGRN_VW + 2 * D_MODEL

kernel_name = "hybrid_gqa_axial_hgrn2_macaron"


def rmsnorm(x, g):
    x32 = x.astype(jnp.float32)
    y = x32 * lax.rsqrt(jnp.mean(x32 * x32, axis=-1, keepdims=True) + NORM_EPS)
    return (y * g.astype(jnp.float32)).astype(x.dtype)


def swiglu(h, w_gate, w_up, w_down):
    return (jax.nn.silu(h @ w_gate) * (h @ w_up)) @ w_down


def split_columns(p):
    offsets = [int(o) for o in np.cumsum(IN_SPLITS)[:-1]]
    return jnp.split(p, offsets, axis=-1)


def axial_rope_tables(seq_len):
    rows = seq_len // GRID_W
    row = jnp.repeat(jnp.arange(rows, dtype=jnp.float32), GRID_W)
    col = jnp.tile(jnp.arange(GRID_W, dtype=jnp.float32), rows)
    n_freq = HEAD_DIM // 4
    inv = ROPE_THETA ** (-jnp.arange(n_freq, dtype=jnp.float32) / n_freq)
    ang = jnp.concatenate([row[:, None] * inv, col[:, None] * inv], axis=-1)
    return jnp.cos(ang), jnp.sin(ang)


def apply_rope(x, cos, sin):
    x32 = x.astype(jnp.float32)
    x1, x2 = jnp.split(x32, 2, axis=-1)
    c = cos[:, None, :]
    s = sin[:, None, :]
    return jnp.concatenate([x1 * c - x2 * s, x2 * c + x1 * s], axis=-1).astype(x.dtype)


def axial_gqa_attention(q, k, v, q_g, k_g, cos, sin):
    B, S, _ = q.shape
    q = apply_rope(rmsnorm(q.reshape(B, S, ATTN_HEADS, HEAD_DIM), q_g), cos, sin)
    k = apply_rope(rmsnorm(k.reshape(B, S, ATTN_KV_HEADS, HEAD_DIM), k_g), cos, sin)
    v = v.reshape(B, S, ATTN_KV_HEADS, HEAD_DIM)
    q = (q * (HEAD_DIM ** -0.5)).reshape(B, S // Q_BLOCK, Q_BLOCK, ATTN_KV_HEADS, ATTN_GROUP, HEAD_DIM)
    q_blocks = jnp.moveaxis(q, 1, 0)

    def attend(qb):
        s = jnp.einsum('bqgrd,bkgd->bgrqk', qb, k).astype(jnp.float32)
        p = jax.nn.softmax(s, axis=-1).astype(v.dtype)
        return jnp.einsum('bgrqk,bkgd->bqgrd', p, v)

    o = lax.map(attend, q_blocks)
    return jnp.moveaxis(o, 0, 1).reshape(B, S, ATTN_WIDTH)


def gated_scan_chunked(q, k, v, log_f):
    B, H, S, Dk = q.shape
    Dv = v.shape[-1]
    C = HGRN_CHUNK
    N = S // C
    q, k, v, log_f = [t.astype(jnp.float32).reshape(B, H, N, C, t.shape[-1]) for t in (q, k, v, log_f)]
    b = jnp.cumsum(log_f, axis=3)
    b_mid = b[:, :, :, C // 2 - 1:C // 2, :]
    b_last = b[:, :, :, C - 1:, :]
    scores = jnp.einsum('bhntd,bhnsd->bhnts', q * jnp.exp(b - b_mid), k * jnp.exp(b_mid - b))
    lower = jnp.tril(jnp.ones((C, C), dtype=bool))
    scores = jnp.where(lower, scores, 0.0)
    o_intra = jnp.einsum('bhnts,bhnse->bhnte', scores, v)
    contrib = jnp.einsum('bhnsd,bhnse->bhnde', k * jnp.exp(b_last - b), v)
    decay = jnp.exp(b_last[:, :, :, 0, :])

    def step(state, inp):
        dec, add = inp
        return dec[..., None] * state + add, state

    _, states_in = lax.scan(step, jnp.zeros((B, H, Dk, Dv), jnp.float32),
                            (jnp.moveaxis(decay, 2, 0), jnp.moveaxis(contrib, 2, 0)))
    states_in = jnp.moveaxis(states_in, 0, 2)
    o_inter = jnp.einsum('bhntd,bhnde->bhnte', q * jnp.exp(b), states_in)
    return (o_intra + o_inter).reshape(B, H, S, Dv)


def hgrn2_bidirectional(q, f_fwd, f_bwd, i, g, lb, norm_g):
    B, S, _ = q.shape

    def heads(t, d):
        return t.reshape(B, S, HGRN_HEADS, d).transpose(0, 2, 1, 3)

    qh = heads(jax.nn.silu(q), HGRN_KEY_DIM)
    vh = heads(i, HGRN_VAL_DIM)

    def direction(f_logits, lb_dir, reverse):
        f = lb_dir + (1.0 - lb_dir) * jax.nn.sigmoid(f_logits.astype(jnp.float32))
        kh = heads(1.0 - f, HGRN_KEY_DIM)
        lfh = heads(jnp.log(f), HGRN_KEY_DIM)
        ts = (qh, kh, vh, lfh)
        if reverse:
            ts = tuple(jnp.flip(t, axis=2) for t in ts)
        o = gated_scan_chunked(*ts)
        return jnp.flip(o, axis=2) if reverse else o

    o = direction(f_fwd, lb[0], False) + direction(f_bwd, lb[1], True)
    o = rmsnorm(o.transpose(0, 2, 1, 3), norm_g.reshape(HGRN_HEADS, HGRN_VAL_DIM))
    o = o.reshape(B, S, HGRN_VW) * jax.nn.silu(g.astype(jnp.float32))
    return o.astype(g.dtype)


def setup_inputs(seed: int = 0) -> dict:
    key = jax.random.key(seed)
    ks = jax.random.split(key, 20)
    f32 = jnp.float32

    def w(k, shape, fan_in):
        return jax.random.normal(k, shape, f32) * (fan_in ** -0.5)

    def gain(k, shape):
        return 1.0 + 0.02 * jax.random.normal(k, shape, f32)

    return {
        "x": jax.random.normal(ks[0], (BATCH, SEQ, D_MODEL), f32),
        "ffn1_norm_g": gain(ks[1], (DEPTH, D_MODEL)),
        "ffn1_w_gate": w(ks[2], (DEPTH, D_MODEL, D_FF), D_MODEL),
        "ffn1_w_up": w(ks[3], (DEPTH, D_MODEL, D_FF), D_MODEL),
        "ffn1_w_down": w(ks[4], (DEPTH, D_FF, D_MODEL), D_FF),
        "mix_norm_g": gain(ks[5], (DEPTH, D_MODEL)),
        "w_in": w(ks[6], (DEPTH, D_MODEL, IN_WIDTH), D_MODEL),
        "q_norm_g": gain(ks[7], (DEPTH, HEAD_DIM)),
        "k_norm_g": gain(ks[8], (DEPTH, HEAD_DIM)),
        "hgrn_lb_logits": 0.1 * jax.random.normal(ks[9], (2, DEPTH + 1, HGRN_KW), f32),
        "hgrn_out_norm_g": gain(ks[10], (DEPTH, HGRN_VW)),
        "w_branch_attn": w(ks[11], (DEPTH, ATTN_WIDTH, D_MODEL), ATTN_WIDTH),
        "w_branch_hgrn": w(ks[12], (DEPTH, HGRN_VW, D_MODEL), HGRN_VW),
        "w_out": w(ks[13], (DEPTH, D_MODEL, D_MODEL), D_MODEL),
        "ffn2_norm_g": gain(ks[14], (DEPTH, D_MODEL)),
        "ffn2_w_gate": w(ks[15], (DEPTH, D_MODEL, D_FF), D_MODEL),
        "ffn2_w_up": w(ks[16], (DEPTH, D_MODEL, D_FF), D_MODEL),
        "ffn2_w_down": w(ks[17], (DEPTH, D_FF, D_MODEL), D_FF),
        "final_norm_g": gain(ks[18], (D_MODEL,)),
    }


def reference(x, ffn1_norm_g, ffn1_w_gate, ffn1_w_up, ffn1_w_down, mix_norm_g, w_in,
              q_norm_g, k_norm_g, hgrn_lb_logits, hgrn_out_norm_g, w_branch_attn,
              w_branch_hgrn, w_out, ffn2_norm_g, ffn2_w_gate, ffn2_w_up, ffn2_w_down,
              final_norm_g):
    S = x.shape[1]
    cos, sin = axial_rope_tables(S)
    lb_all = jnp.cumsum(jax.nn.softmax(hgrn_lb_logits.astype(jnp.float32), axis=1), axis=1)
    for l in range(DEPTH):
        x = x + 0.5 * swiglu(rmsnorm(x, ffn1_norm_g[l]), ffn1_w_gate[l], ffn1_w_up[l], ffn1_w_down[l])
        h = rmsnorm(x, mix_norm_g[l])
        aq, ak, av, hq, hff, hfb, hi, hg, ga, gb = split_columns(h @ w_in[l])
        y_attn = axial_gqa_attention(aq, ak, av, q_norm_g[l], k_norm_g[l], cos, sin) @ w_branch_attn[l]
        y_hgrn = hgrn2_bidirectional(hq, hff, hfb, hi, hg, lb_all[:, l, :], hgrn_out_norm_g[l]) @ w_branch_hgrn[l]
        merged = jax.nn.sigmoid(ga) * y_attn + jax.nn.sigmoid(gb) * y_hgrn
        x = x + merged @ w_out[l]
        x = x + 0.5 * swiglu(rmsnorm(x, ffn2_norm_g[l]), ffn2_w_gate[l], ffn2_w_up[l], ffn2_w_down[l])
    return rmsnorm(x, final_norm_g)
```

```python
import functools

import jax
import jax.numpy as jnp
import numpy as np
from jax import lax
from jax.experimental import pallas as pl
from jax.experimental.pallas import tpu as pltpu

F32 = jnp.float32
BF16 = jnp.bfloat16

NORM_EPS = 1e-6
ROPE_THETA = 10000.0
GRID_W = 64
HEAD_DIM = 64
ATTN_HEADS = 8
ATTN_KV_HEADS = 2
ATTN_GROUP = ATTN_HEADS // ATTN_KV_HEADS
HGRN_HEADS = 8
HGRN_DIM = 64
HGRN_CHUNK = 32
ATTN_WIDTH = ATTN_HEADS * HEAD_DIM
KV_WIDTH = ATTN_KV_HEADS * HEAD_DIM
HGRN_WIDTH = HGRN_HEADS * HGRN_DIM

V7X_LANES = 128
V7X_MXU_DIM = 256
V7X_VMEM_LIMIT_BYTES = 56 * 1024 * 1024

FFN_ROWS = 512
FFN_COLS = V7X_MXU_DIM
PROJ_ROWS = 256
ATTN_Q_ROWS = 128
HGRN_ROWS = 256
MERGE_ROWS = 256


def _compiler_params(semantics):
    return pltpu.CompilerParams(dimension_semantics=semantics, vmem_limit_bytes=V7X_VMEM_LIMIT_BYTES)


def _resident(shape):
    zeros = (0,) * len(shape)
    return pl.BlockSpec(shape, lambda *_: zeros, pipeline_mode=pl.Buffered(1))


def _rmsnorm(x, g):
    return x * lax.rsqrt(jnp.mean(x * x, axis=-1, keepdims=True) + NORM_EPS) * g


def _dot(a, b):
    return jnp.dot(a, b, preferred_element_type=F32)


def _dot_nt(a, b):
    return lax.dot_general(a, b, (((1,), (1,)), ((), ())), preferred_element_type=F32)


def _dot_tn(a, b):
    return lax.dot_general(a, b, (((0,), (0,)), ((), ())), preferred_element_type=F32)


def _split2(x):
    hi = x.astype(BF16)
    lo = (x - hi.astype(F32)).astype(BF16)
    return hi, lo


def _split3(x):
    hi = x.astype(BF16)
    r = x - hi.astype(F32)
    mid = r.astype(BF16)
    lo = (r - mid.astype(F32)).astype(BF16)
    return hi, mid, lo


def _group_mean(x, gmat):
    hi, lo = _split2(x)
    return _dot(hi, gmat) + _dot(lo, gmat)


def _ffn_kernel(x_ref, g_ref, wg_ref, wu_ref, wd_ref, fg_ref, o_ref, *, final_norm):
    x = x_ref[...]
    xn = _rmsnorm(x, g_ref[...]).astype(BF16)
    d_ff = wg_ref.shape[1]
    acc = None
    for c in range(d_ff // FFN_COLS):
        sl = slice(c * FFN_COLS, (c + 1) * FFN_COLS)
        gate = _dot(xn, wg_ref[:, sl])
        up = _dot(xn, wu_ref[:, sl])
        act = (jax.nn.silu(gate) * up).astype(BF16)
        part = _dot(act, wd_ref[sl, :])
        acc = part if acc is None else acc + part
    y = x + 0.5 * acc
    if final_norm:
        y = _rmsnorm(y, fg_ref[...])
    o_ref[...] = y


def _ffn(x, norm_g, w_gate, w_up, w_down, final_g, final_norm):
    t, d = x.shape
    d_ff = w_gate.shape[1]
    assert t % FFN_ROWS == 0 and d_ff % FFN_COLS == 0
    row = pl.BlockSpec((FFN_ROWS, d), lambda i: (i, 0))
    return pl.pallas_call(
        functools.partial(_ffn_kernel, final_norm=final_norm),
        grid=(t // FFN_ROWS,),
        in_specs=[row, _resident((1, d)), _resident((d, d_ff)), _resident((d, d_ff)),
                  _resident((d_ff, d)), _resident((1, d))],
        out_specs=row,
        out_shape=jax.ShapeDtypeStruct((t, d), F32),
        compiler_params=_compiler_params(("parallel",)),
        name="ffn_final" if final_norm else "ffn",
    )(x, norm_g, w_gate, w_up, w_down, final_g)


def _rope(xn, cos_t, sin_t, width):
    half = HEAD_DIM // 2
    lane = lax.broadcasted_iota(jnp.int32, xn.shape, 1)
    first_half = (lane % HEAD_DIM) < half
    partner = jnp.where(first_half, pltpu.roll(xn, width - half, 1), pltpu.roll(xn, half, 1))
    return xn * cos_t + partner * sin_t


def _proj_kernel(x_ref, g_ref, w_ref, qg_ref, kg_ref, gq_ref, gk_ref, cos_ref, sin_ref,
                 q_ref, k_ref, v_ref, hq_ref, ff_ref, fb_ref, hi_ref, hg_ref, ga_ref, gb_ref):
    d = x_ref.shape[-1]
    h = _rmsnorm(x_ref[0], g_ref[...]).astype(BF16)

    def cols(start, width):
        return _dot(h, w_ref[:, start:start + width])

    cos2 = cos_ref[...]
    sin2 = sin_ref[...]
    off = 0
    q = cols(off, ATTN_WIDTH)
    off += ATTN_WIDTH
    reps = ATTN_WIDTH // V7X_LANES
    qn = q * lax.rsqrt(_group_mean(q * q, gq_ref[...]) + NORM_EPS) * qg_ref[...]
    qr = _rope(qn, jnp.concatenate([cos2] * reps, axis=1), jnp.concatenate([sin2] * reps, axis=1),
               ATTN_WIDTH) * (HEAD_DIM ** -0.5)
    for hd in range(ATTN_HEADS):
        q_ref[0, hd] = qr[:, hd * HEAD_DIM:(hd + 1) * HEAD_DIM].astype(BF16)
    k = cols(off, KV_WIDTH)
    off += KV_WIDTH
    kn = k * lax.rsqrt(_group_mean(k * k, gk_ref[...]) + NORM_EPS) * kg_ref[...]
    kr = _rope(kn, cos2, sin2, KV_WIDTH)
    v = cols(off, KV_WIDTH)
    off += KV_WIDTH
    for hd in range(ATTN_KV_HEADS):
        k_ref[0, hd] = kr[:, hd * HEAD_DIM:(hd + 1) * HEAD_DIM].astype(BF16)
        v_ref[0, hd] = v[:, hd * HEAD_DIM:(hd + 1) * HEAD_DIM].astype(BF16)
    hq_ref[0] = jax.nn.silu(cols(off, HGRN_WIDTH))
    off += HGRN_WIDTH
    ff_ref[0] = cols(off, HGRN_WIDTH)
    off += HGRN_WIDTH
    fb_ref[0] = cols(off, HGRN_WIDTH)
    off += HGRN_WIDTH
    hi_ref[0] = cols(off, HGRN_WIDTH).astype(BF16)
    off += HGRN_WIDTH
    hg_ref[0] = jax.nn.silu(cols(off, HGRN_WIDTH))
    off += HGRN_WIDTH
    ga_ref[0] = jax.nn.sigmoid(cols(off, d))
    off += d
    gb_ref[0] = jax.nn.sigmoid(cols(off, d))


def _group_matrix(width, group):
    idx = np.arange(width) // group
    return jnp.asarray((idx[:, None] == idx[None, :]).astype(np.float32) / group, dtype=BF16)


def _rope_tables(seq_len):
    rows = seq_len // GRID_W
    row = jnp.repeat(jnp.arange(rows, dtype=F32), GRID_W)
    col = jnp.tile(jnp.arange(GRID_W, dtype=F32), rows)
    n_freq = HEAD_DIM // 4
    inv = ROPE_THETA ** (-jnp.arange(n_freq, dtype=F32) / n_freq)
    ang = jnp.concatenate([row[:, None] * inv, col[:, None] * inv], axis=-1)
    cos, sin = jnp.cos(ang), jnp.sin(ang)
    cos2 = jnp.concatenate([cos, cos] * 2, axis=-1)
    sin2 = jnp.concatenate([-sin, sin] * 2, axis=-1)
    return cos2, sin2


def _in_proj(x1, norm_g, w_in, q_g, k_g):
    b, s, d = x1.shape
    n_in = w_in.shape[1]
    assert s % PROJ_ROWS == 0
    cos2, sin2 = _rope_tables(s)
    tm = PROJ_ROWS
    tok = lambda width: pl.BlockSpec((1, tm, width), lambda bi, i: (bi, i, 0))
    heads = lambda n: pl.BlockSpec((1, n, tm, HEAD_DIM), lambda bi, i: (bi, 0, i, 0))
    pos = pl.BlockSpec((tm, V7X_LANES), lambda bi, i: (i, 0))
    f32_tok = lambda width: jax.ShapeDtypeStruct((b, s, width), F32)
    out_shape = (
        jax.ShapeDtypeStruct((b, ATTN_HEADS, s, HEAD_DIM), BF16),
        jax.ShapeDtypeStruct((b, ATTN_KV_HEADS, s, HEAD_DIM), BF16),
        jax.ShapeDtypeStruct((b, ATTN_KV_HEADS, s, HEAD_DIM), BF16),
        f32_tok(HGRN_WIDTH), f32_tok(HGRN_WIDTH), f32_tok(HGRN_WIDTH),
        jax.ShapeDtypeStruct((b, s, HGRN_WIDTH), BF16),
        f32_tok(HGRN_WIDTH), f32_tok(d), f32_tok(d),
    )
    out_specs = (heads(ATTN_HEADS), heads(ATTN_KV_HEADS), heads(ATTN_KV_HEADS),
                 tok(HGRN_WIDTH), tok(HGRN_WIDTH), tok(HGRN_WIDTH), tok(HGRN_WIDTH), tok(HGRN_WIDTH),
                 tok(d), tok(d))
    return pl.pallas_call(
        _proj_kernel,
        grid=(b, s // tm),
        in_specs=[tok(d), _resident((1, d)), _resident((d, n_in)),
                  _resident((1, ATTN_WIDTH)), _resident((1, KV_WIDTH)),
                  _resident((ATTN_WIDTH, ATTN_WIDTH)), _resident((KV_WIDTH, KV_WIDTH)), pos, pos],
        out_specs=out_specs,
        out_shape=out_shape,
        compiler_params=_compiler_params(("parallel", "parallel")),
        name="in_proj",
    )(x1, norm_g, w_in, jnp.tile(q_g, (1, ATTN_HEADS)), jnp.tile(k_g, (1, ATTN_KV_HEADS)),
      _group_matrix(ATTN_WIDTH, HEAD_DIM), _group_matrix(KV_WIDTH, HEAD_DIM), cos2, sin2)


def _attn_kernel(q_ref, k_ref, v_ref, o_ref):
    tq = q_ref.shape[2]
    q = q_ref[0].reshape(ATTN_GROUP * tq, HEAD_DIM)
    s = _dot_nt(q, k_ref[0, 0])
    p = jnp.exp(s - jnp.max(s, axis=-1, keepdims=True))
    denom = jnp.sum(p, axis=-1, keepdims=True)
    o = _dot(p.astype(BF16), v_ref[0, 0]) / denom
    o = o.reshape(ATTN_GROUP, tq, HEAD_DIM)
    o_ref[0] = jnp.concatenate([o[r] for r in range(ATTN_GROUP)], axis=1).astype(BF16)


def _attention(q, k, v):
    b, _, s, _ = q.shape
    tq = ATTN_Q_ROWS
    assert s % tq == 0
    kv_spec = pl.BlockSpec((1, 1, s, HEAD_DIM), lambda bi, g, i: (bi, g, 0, 0))
    return pl.pallas_call(
        _attn_kernel,
        grid=(b, ATTN_KV_HEADS, s // tq),
        in_specs=[pl.BlockSpec((1, ATTN_GROUP, tq, HEAD_DIM), lambda bi, g, i: (bi, g, i, 0)),
                  kv_spec, kv_spec],
        out_specs=pl.BlockSpec((1, tq, ATTN_GROUP * HEAD_DIM), lambda bi, g, i: (bi, i, g)),
        out_shape=jax.ShapeDtypeStruct((b, s, ATTN_WIDTH), BF16),
        compiler_params=_compiler_params(("parallel", "parallel", "parallel")),
        name="attention",
    )(q, k, v)


def _hgrn_direction(qt, f_logit, v, lb, state_ref, reverse):
    rows = qt.shape[0]
    c = HGRN_CHUNK
    n_chunks = rows // c
    pair = 2 * HGRN_DIM
    f = lb + (1.0 - lb) * jax.nn.sigmoid(f_logit)
    kk = 1.0 - f
    log_f = jnp.log(f)
    r_i = lax.broadcasted_iota(jnp.int32, (rows, rows), 0)
    c_i = lax.broadcasted_iota(jnp.int32, (rows, rows), 1)
    same_chunk = (r_i // c) == (c_i // c)
    causal = same_chunk & ((c_i >= r_i) if reverse else (c_i <= r_i))
    tri = jnp.where(causal, 1.0, 0.0).astype(BF16)
    hi, mid, lo = _split3(log_f)
    bcum = _dot(tri, hi) + _dot(tri, mid) + _dot(tri, lo)
    b3 = bcum.reshape(n_chunks, c, bcum.shape[-1])
    mid_row = c // 2 if reverse else c // 2 - 1
    last_row = 0 if reverse else c - 1
    b_mid = b3[:, mid_row:mid_row + 1, :]
    b_last = b3[:, last_row:last_row + 1, :]
    q3 = qt.reshape(b3.shape)
    k3 = kk.reshape(b3.shape)
    q_mid = (q3 * jnp.exp(b3 - b_mid)).reshape(bcum.shape).astype(BF16)
    k_mid = (k3 * jnp.exp(b_mid - b3)).reshape(bcum.shape).astype(BF16)
    k_end = (k3 * jnp.exp(b_last - b3)).reshape(bcum.shape).astype(BF16)
    q_in = (qt * jnp.exp(bcum)).astype(BF16)
    decay = jnp.exp(b_last)

    lane = lax.broadcasted_iota(jnp.int32, (rows, pair), 1)
    lower_head = lane < HGRN_DIM
    sr = lax.broadcasted_iota(jnp.int32, (pair, pair), 0)
    sc = lax.broadcasted_iota(jnp.int32, (pair, pair), 1)
    same_head = (sr // HGRN_DIM) == (sc // HGRN_DIM)
    zero_bf = jnp.zeros((), BF16)
    outs = []
    for p in range(HGRN_HEADS // 2):
        sl = slice(p * pair, (p + 1) * pair)
        v_p = v[:, sl]
        q_mid_p = q_mid[:, sl]
        k_mid_p = k_mid[:, sl]
        o_heads = []
        for head_is_lower in (True, False):
            k_sel = jnp.where(lower_head == head_is_lower, k_mid_p, zero_bf)
            scores = jnp.where(causal, _dot_nt(q_mid_p, k_sel), 0.0).astype(BF16)
            o_heads.append(_dot(scores, v_p))
        o_intra = jnp.where(lower_head, o_heads[0], o_heads[1])
        st = state_ref[p]
        o_inter = [None] * n_chunks
        order = range(n_chunks - 1, -1, -1) if reverse else range(n_chunks)
        for n in order:
            rs = slice(n * c, (n + 1) * c)
            o_inter[n] = _dot_nt(q_in[rs, sl], st.astype(BF16))
            contrib = _dot_tn(v_p[rs], k_end[rs, sl])
            st = decay[n, :, sl] * st + jnp.where(same_head, contrib, 0.0)
        state_ref[p] = st
        outs.append(o_intra + jnp.concatenate(o_inter, axis=0))
    return jnp.concatenate(outs, axis=1)


def _hgrn_kernel(lbl_ref, qf_ref, ff_ref, vf_ref, qb_ref, fb_ref, vb_ref, of_ref, ob_ref,
                 st_f_ref, st_b_ref):
    @pl.when(pl.program_id(1) == 0)
    def _():
        st_f_ref[...] = jnp.zeros_like(st_f_ref)
        st_b_ref[...] = jnp.zeros_like(st_b_ref)

    logits = lbl_ref[...]
    e = jnp.exp(logits - jnp.max(logits, axis=1, keepdims=True))
    lb = e[:, 0, :] / jnp.sum(e, axis=1)
    of_ref[0] = _hgrn_direction(qf_ref[0], ff_ref[0], vf_ref[0], lb[0:1], st_f_ref, False)
    ob_ref[0] = _hgrn_direction(qb_ref[0], fb_ref[0], vb_ref[0], lb[1:2], st_b_ref, True)


def _hgrn(lb_logits, hq, hff, hfb, hi):
    b, s, w = hq.shape
    r = HGRN_ROWS
    assert s % r == 0 and r % HGRN_CHUNK == 0
    nb = s // r
    fwd = pl.BlockSpec((1, r, w), lambda bi, j: (bi, j, 0))
    bwd = pl.BlockSpec((1, r, w), lambda bi, j: (bi, nb - 1 - j, 0))
    pair = 2 * HGRN_DIM
    state = pltpu.VMEM((HGRN_HEADS // 2, pair, pair), F32)
    return pl.pallas_call(
        _hgrn_kernel,
        grid=(b, nb),
        in_specs=[_resident(lb_logits.shape), fwd, fwd, fwd, bwd, bwd, bwd],
        out_specs=(fwd, bwd),
        out_shape=(jax.ShapeDtypeStruct((b, s, w), F32), jax.ShapeDtypeStruct((b, s, w), F32)),
        scratch_shapes=[state, state],
        compiler_params=_compiler_params(("parallel", "arbitrary")),
        name="hgrn",
    )(lb_logits, hq, hff, hi, hq, hfb, hi)


def _merge_kernel(x_ref, ya_ref, of_ref, ob_ref, hg_ref, ga_ref, gb_ref, ng_ref, gm_ref,
                  wa_ref, wb_ref, wo_ref, o_ref):
    o = of_ref[...] + ob_ref[...]
    on = o * lax.rsqrt(_group_mean(o * o, gm_ref[...]) + NORM_EPS) * ng_ref[...]
    yb = (on * hg_ref[...]).astype(BF16)
    merged = ga_ref[...] * _dot(ya_ref[...], wa_ref[...]) + gb_ref[...] * _dot(yb, wb_ref[...])
    o_ref[...] = x_ref[...] + _dot(merged.astype(BF16), wo_ref[...])


def _merge(x1, y_attn, o_f, o_b, hg, ga, gb, norm_g, w_a, w_b, w_out):
    t, d = x1.shape
    tm = MERGE_ROWS
    assert t % tm == 0
    row = lambda width: pl.BlockSpec((tm, width), lambda i: (i, 0))
    return pl.pallas_call(
        _merge_kernel,
        grid=(t // tm,),
        in_specs=[row(d), row(ATTN_WIDTH), row(HGRN_WIDTH), row(HGRN_WIDTH), row(HGRN_WIDTH),
                  row(d), row(d), _resident((1, HGRN_WIDTH)), _resident((HGRN_WIDTH, HGRN_WIDTH)),
                  _resident((ATTN_WIDTH, d)), _resident((HGRN_WIDTH, d)), _resident((d, d))],
        out_specs=row(d),
        out_shape=jax.ShapeDtypeStruct((t, d), F32),
        compiler_params=_compiler_params(("parallel",)),
        name="merge",
    )(x1, y_attn, o_f, o_b, hg, ga, gb, norm_g, _group_matrix(HGRN_WIDTH, HGRN_DIM), w_a, w_b, w_out)


def kernel(x, ffn1_norm_g, ffn1_w_gate, ffn1_w_up, ffn1_w_down, mix_norm_g, w_in, q_norm_g, k_norm_g,
           hgrn_lb_logits, hgrn_out_norm_g, w_branch_attn, w_branch_hgrn, w_out, ffn2_norm_g,
           ffn2_w_gate, ffn2_w_up, ffn2_w_down, final_norm_g):
    b, s, d = x.shape
    t = b * s
    depth = w_in.shape[0]
    assert depth == 1 and hgrn_lb_logits.shape[1] == depth + 1
    bf = lambda w: w.astype(BF16)
    final_g = final_norm_g.reshape(1, d)

    x1 = _ffn(x.reshape(t, d), ffn1_norm_g, bf(ffn1_w_gate[0]), bf(ffn1_w_up[0]), bf(ffn1_w_down[0]),
              final_g, False)
    q, k, v, hq, hff, hfb, hi, hg, ga, gb = _in_proj(
        x1.reshape(b, s, d), mix_norm_g, bf(w_in[0]), q_norm_g, k_norm_g)
    y_attn = _attention(q, k, v)
    o_f, o_b = _hgrn(hgrn_lb_logits, hq, hff, hfb, hi)
    flat = lambda a: a.reshape(t, a.shape[-1])
    x2 = _merge(x1, flat(y_attn), flat(o_f), flat(o_b), flat(hg), flat(ga), flat(gb), hgrn_out_norm_g,
                bf(w_branch_attn[0]), bf(w_branch_hgrn[0]), bf(w_out[0]))
    out = _ffn(x2, ffn2_norm_g, bf(ffn2_w_gate[0]), bf(ffn2_w_up[0]), bf(ffn2_w_down[0]), final_g, True)
    return out.reshape(b, s, d)
```

```python
import functools

import jax
import jax.numpy as jnp
import numpy as np
from jax import lax
from jax.experimental import pallas as pl
from jax.experimental.pallas import tpu as pltpu

F32 = jnp.float32
BF16 = jnp.bfloat16

NORM_EPS = 1e-6
ROPE_THETA = 10000.0
GRID_W = 64
HEAD_DIM = 64
ATTN_HEADS = 8
ATTN_KV_HEADS = 2
ATTN_GROUP = ATTN_HEADS // ATTN_KV_HEADS
HGRN_HEADS = 8
HGRN_DIM = 64
HGRN_CHUNK = 32
ATTN_WIDTH = ATTN_HEADS * HEAD_DIM
KV_WIDTH = ATTN_KV_HEADS * HEAD_DIM
HGRN_WIDTH = HGRN_HEADS * HGRN_DIM

V7X_LANES = 128
V7X_MXU_DIM = 256
V7X_VMEM_LIMIT_BYTES = 56 * 1024 * 1024

FFN_ROWS = 512
FFN_COLS = V7X_MXU_DIM
PROJ_ROWS = 256
ATTN_Q_ROWS = 128
ATTN_KEY_CHUNK = 512
ATTN_SCORE_BOUND = 60.0
HGRN_ROWS = 256
MERGE_ROWS = 256


def _compiler_params(semantics):
    return pltpu.CompilerParams(dimension_semantics=semantics, vmem_limit_bytes=V7X_VMEM_LIMIT_BYTES)


def _resident(shape):
    zeros = (0,) * len(shape)
    return pl.BlockSpec(shape, lambda *_: zeros, pipeline_mode=pl.Buffered(1))


def _rmsnorm(x, g):
    return x * lax.rsqrt(jnp.mean(x * x, axis=-1, keepdims=True) + NORM_EPS) * g


def _dot(a, b):
    return jnp.dot(a, b, preferred_element_type=F32)


def _dot_nt(a, b):
    return lax.dot_general(a, b, (((1,), (1,)), ((), ())), preferred_element_type=F32)


def _dot_tn(a, b):
    return lax.dot_general(a, b, (((0,), (0,)), ((), ())), preferred_element_type=F32)


def _split2(x):
    hi = x.astype(BF16)
    lo = (x - hi.astype(F32)).astype(BF16)
    return hi, lo


def _split3(x):
    hi = x.astype(BF16)
    r = x - hi.astype(F32)
    mid = r.astype(BF16)
    lo = (r - mid.astype(F32)).astype(BF16)
    return hi, mid, lo


def _group_mean(x, gmat):
    hi, lo = _split2(x)
    return _dot(hi, gmat) + _dot(lo, gmat)


def _ffn_kernel(x_ref, g_ref, wg_ref, wu_ref, wd_ref, fg_ref, o_ref, *, final_norm):
    x = x_ref[...]
    xn = _rmsnorm(x, g_ref[...]).astype(BF16)
    d_ff = wg_ref.shape[1]
    acc = None
    for c in range(d_ff // FFN_COLS):
        sl = slice(c * FFN_COLS, (c + 1) * FFN_COLS)
        gate = _dot(xn, wg_ref[:, sl])
        up = _dot(xn, wu_ref[:, sl])
        act = (jax.nn.silu(gate) * up).astype(BF16)
        part = _dot(act, wd_ref[sl, :])
        acc = part if acc is None else acc + part
    y = x + 0.5 * acc
    if final_norm:
        y = _rmsnorm(y, fg_ref[...])
    o_ref[...] = y


def _ffn(x, norm_g, w_gate, w_up, w_down, final_g, final_norm):
    t, d = x.shape
    d_ff = w_gate.shape[1]
    assert t % FFN_ROWS == 0 and d_ff % FFN_COLS == 0
    row = pl.BlockSpec((FFN_ROWS, d), lambda i: (i, 0))
    return pl.pallas_call(
        functools.partial(_ffn_kernel, final_norm=final_norm),
        grid=(t // FFN_ROWS,),
        in_specs=[row, _resident((1, d)), _resident((d, d_ff)), _resident((d, d_ff)),
                  _resident((d_ff, d)), _resident((1, d))],
        out_specs=row,
        out_shape=jax.ShapeDtypeStruct((t, d), F32),
        compiler_params=_compiler_params(("parallel",)),
        name="ffn_final" if final_norm else "ffn",
    )(x, norm_g, w_gate, w_up, w_down, final_g)


def _rope(xn, cos_t, sin_t, width):
    half = HEAD_DIM // 2
    lane = lax.broadcasted_iota(jnp.int32, xn.shape, 1)
    first_half = (lane % HEAD_DIM) < half
    partner = jnp.where(first_half, pltpu.roll(xn, width - half, 1), pltpu.roll(xn, half, 1))
    return xn * cos_t + partner * sin_t


def _proj_kernel(x_ref, g_ref, w_ref, qg_ref, kg_ref, gq_ref, gk_ref, cos_ref, sin_ref,
                 q_ref, k_ref, v_ref, hq_ref, ff_ref, fb_ref, hi_ref, hg_ref, ga_ref, gb_ref):
    d = x_ref.shape[-1]
    h = _rmsnorm(x_ref[0], g_ref[...]).astype(BF16)

    def cols(start, width):
        return _dot(h, w_ref[:, start:start + width])

    cos2 = cos_ref[...]
    sin2 = sin_ref[...]
    off = 0
    q = cols(off, ATTN_WIDTH)
    off += ATTN_WIDTH
    reps = ATTN_WIDTH // V7X_LANES
    qn = q * lax.rsqrt(_group_mean(q * q, gq_ref[...]) + NORM_EPS) * qg_ref[...]
    qr = _rope(qn, jnp.concatenate([cos2] * reps, axis=1), jnp.concatenate([sin2] * reps, axis=1),
               ATTN_WIDTH) * (HEAD_DIM ** -0.5)
    for hd in range(ATTN_HEADS):
        q_ref[0, hd] = qr[:, hd * HEAD_DIM:(hd + 1) * HEAD_DIM].astype(BF16)
    k = cols(off, KV_WIDTH)
    off += KV_WIDTH
    kn = k * lax.rsqrt(_group_mean(k * k, gk_ref[...]) + NORM_EPS) * kg_ref[...]
    kr = _rope(kn, cos2, sin2, KV_WIDTH)
    v = cols(off, KV_WIDTH)
    off += KV_WIDTH
    lane = lax.broadcasted_iota(jnp.int32, v.shape, 1)
    for hd in range(ATTN_KV_HEADS):
        k_ref[0, hd] = kr[:, hd * HEAD_DIM:(hd + 1) * HEAD_DIM].astype(BF16)
        v_hd = v if hd == 0 else pltpu.roll(v, KV_WIDTH - hd * HEAD_DIM, 1)
        v_ref[0, hd] = jnp.where(lane < HEAD_DIM, v_hd, 1.0).astype(BF16)
    hq_ref[0] = jax.nn.silu(cols(off, HGRN_WIDTH))
    off += HGRN_WIDTH
    ff_ref[0] = cols(off, HGRN_WIDTH)
    off += HGRN_WIDTH
    fb_ref[0] = cols(off, HGRN_WIDTH)
    off += HGRN_WIDTH
    hi_ref[0] = cols(off, HGRN_WIDTH).astype(BF16)
    off += HGRN_WIDTH
    hg_ref[0] = jax.nn.silu(cols(off, HGRN_WIDTH))
    off += HGRN_WIDTH
    ga_ref[0] = jax.nn.sigmoid(cols(off, d))
    off += d
    gb_ref[0] = jax.nn.sigmoid(cols(off, d))


def _group_matrix(width, group):
    idx = np.arange(width) // group
    return jnp.asarray((idx[:, None] == idx[None, :]).astype(np.float32) / group, dtype=BF16)


def _rope_tables(seq_len):
    rows = seq_len // GRID_W
    row = jnp.repeat(jnp.arange(rows, dtype=F32), GRID_W)
    col = jnp.tile(jnp.arange(GRID_W, dtype=F32), rows)
    n_freq = HEAD_DIM // 4
    inv = ROPE_THETA ** (-jnp.arange(n_freq, dtype=F32) / n_freq)
    ang = jnp.concatenate([row[:, None] * inv, col[:, None] * inv], axis=-1)
    cos, sin = jnp.cos(ang), jnp.sin(ang)
    cos2 = jnp.concatenate([cos, cos] * 2, axis=-1)
    sin2 = jnp.concatenate([-sin, sin] * 2, axis=-1)
    return cos2, sin2


def _in_proj(x1, norm_g, w_in, q_g, k_g):
    b, s, d = x1.shape
    n_in = w_in.shape[1]
    assert s % PROJ_ROWS == 0
    cos2, sin2 = _rope_tables(s)
    tm = PROJ_ROWS
    tok = lambda width: pl.BlockSpec((1, tm, width), lambda bi, i: (bi, i, 0))
    heads = lambda n, width=HEAD_DIM: pl.BlockSpec((1, n, tm, width), lambda bi, i: (bi, 0, i, 0))
    pos = pl.BlockSpec((tm, V7X_LANES), lambda bi, i: (i, 0))
    f32_tok = lambda width: jax.ShapeDtypeStruct((b, s, width), F32)
    out_shape = (
        jax.ShapeDtypeStruct((b, ATTN_HEADS, s, HEAD_DIM), BF16),
        jax.ShapeDtypeStruct((b, ATTN_KV_HEADS, s, HEAD_DIM), BF16),
        jax.ShapeDtypeStruct((b, ATTN_KV_HEADS, s, V7X_LANES), BF16),
        f32_tok(HGRN_WIDTH), f32_tok(HGRN_WIDTH), f32_tok(HGRN_WIDTH),
        jax.ShapeDtypeStruct((b, s, HGRN_WIDTH), BF16),
        f32_tok(HGRN_WIDTH), f32_tok(d), f32_tok(d),
    )
    out_specs = (heads(ATTN_HEADS), heads(ATTN_KV_HEADS), heads(ATTN_KV_HEADS, V7X_LANES),
                 tok(HGRN_WIDTH), tok(HGRN_WIDTH), tok(HGRN_WIDTH), tok(HGRN_WIDTH), tok(HGRN_WIDTH),
                 tok(d), tok(d))
    return pl.pallas_call(
        _proj_kernel,
        grid=(b, s // tm),
        in_specs=[tok(d), _resident((1, d)), _resident((d, n_in)),
                  _resident((1, ATTN_WIDTH)), _resident((1, KV_WIDTH)),
                  _resident((ATTN_WIDTH, ATTN_WIDTH)), _resident((KV_WIDTH, KV_WIDTH)), pos, pos],
        out_specs=out_specs,
        out_shape=out_shape,
        compiler_params=_compiler_params(("parallel", "parallel")),
        name="in_proj",
    )(x1, norm_g, w_in, jnp.tile(q_g, (1, ATTN_HEADS)), jnp.tile(k_g, (1, ATTN_KV_HEADS)),
      _group_matrix(ATTN_WIDTH, HEAD_DIM), _group_matrix(KV_WIDTH, HEAD_DIM), cos2, sin2)


def _attn_kernel(bounded_ref, q_ref, k_ref, v_ref, o_ref):
    tq = q_ref.shape[2]
    s_len = k_ref.shape[2]
    q = q_ref[0].reshape(ATTN_GROUP * tq, HEAD_DIM)

    def finish(acc):
        lane = lax.broadcasted_iota(jnp.int32, (tq, V7X_LANES), 1)
        pairs = []
        for r in range(0, ATTN_GROUP, 2):
            lo, hi = acc[r * tq:(r + 1) * tq], acc[(r + 1) * tq:(r + 2) * tq]
            lo_sw, hi_sw = pltpu.roll(lo, HEAD_DIM, 1), pltpu.roll(hi, HEAD_DIM, 1)
            pairs.append(jnp.where(lane < HEAD_DIM, lo / lo_sw, hi_sw / hi))
        o_ref[0] = jnp.concatenate(pairs, axis=1).astype(BF16)

    @pl.when(bounded_ref[0] == 1)
    def _():
        acc = None
        for c in range(s_len // ATTN_KEY_CHUNK):
            ks = slice(c * ATTN_KEY_CHUNK, (c + 1) * ATTN_KEY_CHUNK)
            p = jnp.exp(_dot_nt(q, k_ref[0, 0, ks, :])).astype(BF16)
            part = _dot(p, v_ref[0, 0, ks, :])
            acc = part if acc is None else acc + part
        finish(acc)

    @pl.when(bounded_ref[0] != 1)
    def _():
        s = _dot_nt(q, k_ref[0, 0])
        p = jnp.exp(s - jnp.max(s, axis=-1, keepdims=True)).astype(BF16)
        finish(_dot(p, v_ref[0, 0]))


def _attention(bounded, q, k, v):
    b, _, s, _ = q.shape
    tq = ATTN_Q_ROWS
    assert s % tq == 0 and s % ATTN_KEY_CHUNK == 0
    kv_spec = lambda width: pl.BlockSpec((1, 1, s, width), lambda bi, g, i, *_: (bi, g, 0, 0))
    grid_spec = pltpu.PrefetchScalarGridSpec(
        num_scalar_prefetch=1,
        grid=(b, ATTN_KV_HEADS, s // tq),
        in_specs=[pl.BlockSpec((1, ATTN_GROUP, tq, HEAD_DIM), lambda bi, g, i, *_: (bi, g, i, 0)),
                  kv_spec(HEAD_DIM), kv_spec(V7X_LANES)],
        out_specs=pl.BlockSpec((1, tq, ATTN_GROUP * HEAD_DIM), lambda bi, g, i, *_: (bi, i, g)),
    )
    return pl.pallas_call(
        _attn_kernel,
        grid_spec=grid_spec,
        out_shape=jax.ShapeDtypeStruct((b, s, ATTN_WIDTH), BF16),
        compiler_params=_compiler_params(("parallel", "parallel", "parallel")),
        name="attention",
    )(bounded, q, k, v)


def _scores_bounded(q_g, k_g):
    bound = (HEAD_DIM ** 0.5) * jnp.max(jnp.abs(q_g)) * jnp.max(jnp.abs(k_g))
    return (bound <= ATTN_SCORE_BOUND).astype(jnp.int32).reshape(1)


def _hgrn_direction(qt, f_logit, v, lb, state_ref, reverse):
    rows = qt.shape[0]
    c = HGRN_CHUNK
    n_chunks = rows // c
    pair = 2 * HGRN_DIM
    f = lb + (1.0 - lb) * jax.nn.sigmoid(f_logit)
    kk = 1.0 - f
    log_f = jnp.log(f)
    r_i = lax.broadcasted_iota(jnp.int32, (rows, rows), 0)
    c_i = lax.broadcasted_iota(jnp.int32, (rows, rows), 1)
    same_chunk = (r_i // c) == (c_i // c)
    causal = same_chunk & ((c_i >= r_i) if reverse else (c_i <= r_i))
    tri = jnp.where(causal, 1.0, 0.0).astype(BF16)
    hi, mid, lo = _split3(log_f)
    bcum = _dot(tri, hi) + _dot(tri, mid) + _dot(tri, lo)
    b3 = bcum.reshape(n_chunks, c, bcum.shape[-1])
    mid_row = c // 2 if reverse else c // 2 - 1
    last_row = 0 if reverse else c - 1
    b_mid = b3[:, mid_row:mid_row + 1, :]
    b_last = b3[:, last_row:last_row + 1, :]
    q3 = qt.reshape(b3.shape)
    k3 = kk.reshape(b3.shape)
    q_mid = (q3 * jnp.exp(b3 - b_mid)).reshape(bcum.shape).astype(BF16)
    k_mid = (k3 * jnp.exp(b_mid - b3)).reshape(bcum.shape).astype(BF16)
    k_end = (k3 * jnp.exp(b_last - b3)).reshape(bcum.shape).astype(BF16)
    q_in = (qt * jnp.exp(bcum)).astype(BF16)
    decay = jnp.exp(b_last)

    lane = lax.broadcasted_iota(jnp.int32, (rows, pair), 1)
    lower_head = lane < HGRN_DIM
    sr = lax.broadcasted_iota(jnp.int32, (pair, pair), 0)
    sc = lax.broadcasted_iota(jnp.int32, (pair, pair), 1)
    same_head = (sr // HGRN_DIM) == (sc // HGRN_DIM)
    zero_bf = jnp.zeros((), BF16)
    outs = []
    for p in range(HGRN_HEADS // 2):
        sl = slice(p * pair, (p + 1) * pair)
        v_p = v[:, sl]
        q_mid_p = q_mid[:, sl]
        k_mid_p = k_mid[:, sl]
        o_heads = []
        for head_is_lower in (True, False):
            k_sel = jnp.where(lower_head == head_is_lower, k_mid_p, zero_bf)
            scores = jnp.where(causal, _dot_nt(q_mid_p, k_sel), 0.0).astype(BF16)
            o_heads.append(_dot(scores, v_p))
        o_intra = jnp.where(lower_head, o_heads[0], o_heads[1])
        st = state_ref[p]
        o_inter = [None] * n_chunks
        order = range(n_chunks - 1, -1, -1) if reverse else range(n_chunks)
        for n in order:
            rs = slice(n * c, (n + 1) * c)
            o_inter[n] = _dot_nt(q_in[rs, sl], st.astype(BF16))
            contrib = _dot_tn(v_p[rs], k_end[rs, sl])
            st = decay[n, :, sl] * st + jnp.where(same_head, contrib, 0.0)
        state_ref[p] = st
        outs.append(o_intra + jnp.concatenate(o_inter, axis=0))
    return jnp.concatenate(outs, axis=1)


def _hgrn_kernel(lbl_ref, qf_ref, ff_ref, vf_ref, qb_ref, fb_ref, vb_ref, of_ref, ob_ref,
                 st_f_ref, st_b_ref):
    @pl.when(pl.program_id(1) == 0)
    def _():
        st_f_ref[...] = jnp.zeros_like(st_f_ref)
        st_b_ref[...] = jnp.zeros_like(st_b_ref)

    logits = lbl_ref[...]
    e = jnp.exp(logits - jnp.max(logits, axis=1, keepdims=True))
    lb = e[:, 0, :] / jnp.sum(e, axis=1)
    of_ref[0] = _hgrn_direction(qf_ref[0], ff_ref[0], vf_ref[0], lb[0:1], st_f_ref, False)
    ob_ref[0] = _hgrn_direction(qb_ref[0], fb_ref[0], vb_ref[0], lb[1:2], st_b_ref, True)


def _hgrn(lb_logits, hq, hff, hfb, hi):
    b, s, w = hq.shape
    r = HGRN_ROWS
    assert s % r == 0 and r % HGRN_CHUNK == 0
    nb = s // r
    fwd = pl.BlockSpec((1, r, w), lambda bi, j: (bi, j, 0))
    bwd = pl.BlockSpec((1, r, w), lambda bi, j: (bi, nb - 1 - j, 0))
    pair = 2 * HGRN_DIM
    state = pltpu.VMEM((HGRN_HEADS // 2, pair, pair), F32)
    return pl.pallas_call(
        _hgrn_kernel,
        grid=(b, nb),
        in_specs=[_resident(lb_logits.shape), fwd, fwd, fwd, bwd, bwd, bwd],
        out_specs=(fwd, bwd),
        out_shape=(jax.ShapeDtypeStruct((b, s, w), F32), jax.ShapeDtypeStruct((b, s, w), F32)),
        scratch_shapes=[state, state],
        compiler_params=_compiler_params(("parallel", "arbitrary")),
        name="hgrn",
    )(lb_logits, hq, hff, hi, hq, hfb, hi)


def _merge_kernel(x_ref, ya_ref, of_ref, ob_ref, hg_ref, ga_ref, gb_ref, ng_ref, gm_ref,
                  wa_ref, wb_ref, wo_ref, o_ref):
    o = of_ref[...] + ob_ref[...]
    on = o * lax.rsqrt(_group_mean(o * o, gm_ref[...]) + NORM_EPS) * ng_ref[...]
    yb = (on * hg_ref[...]).astype(BF16)
    merged = ga_ref[...] * _dot(ya_ref[...], wa_ref[...]) + gb_ref[...] * _dot(yb, wb_ref[...])
    o_ref[...] = x_ref[...] + _dot(merged.astype(BF16), wo_ref[...])


def _merge(x1, y_attn, o_f, o_b, hg, ga, gb, norm_g, w_a, w_b, w_out):
    t, d = x1.shape
    tm = MERGE_ROWS
    assert t % tm == 0
    row = lambda width: pl.BlockSpec((tm, width), lambda i: (i, 0))
    return pl.pallas_call(
        _merge_kernel,
        grid=(t // tm,),
        in_specs=[row(d), row(ATTN_WIDTH), row(HGRN_WIDTH), row(HGRN_WIDTH), row(HGRN_WIDTH),
                  row(d), row(d), _resident((1, HGRN_WIDTH)), _resident((HGRN_WIDTH, HGRN_WIDTH)),
                  _resident((ATTN_WIDTH, d)), _resident((HGRN_WIDTH, d)), _resident((d, d))],
        out_specs=row(d),
        out_shape=jax.ShapeDtypeStruct((t, d), F32),
        compiler_params=_compiler_params(("parallel",)),
        name="merge",
    )(x1, y_attn, o_f, o_b, hg, ga, gb, norm_g, _group_matrix(HGRN_WIDTH, HGRN_DIM), w_a, w_b, w_out)


def kernel(x, ffn1_norm_g, ffn1_w_gate, ffn1_w_up, ffn1_w_down, mix_norm_g, w_in, q_norm_g, k_norm_g,
           hgrn_lb_logits, hgrn_out_norm_g, w_branch_attn, w_branch_hgrn, w_out, ffn2_norm_g,
           ffn2_w_gate, ffn2_w_up, ffn2_w_down, final_norm_g):
    b, s, d = x.shape
    t = b * s
    depth = w_in.shape[0]
    assert depth == 1 and hgrn_lb_logits.shape[1] == depth + 1
    bf = lambda w: w.astype(BF16)
    final_g = final_norm_g.reshape(1, d)

    x1 = _ffn(x.reshape(t, d), ffn1_norm_g, bf(ffn1_w_gate[0]), bf(ffn1_w_up[0]), bf(ffn1_w_down[0]),
              final_g, False)
    q, k, v, hq, hff, hfb, hi, hg, ga, gb = _in_proj(
        x1.reshape(b, s, d), mix_norm_g, bf(w_in[0]), q_norm_g, k_norm_g)
    y_attn = _attention(_scores_bounded(q_norm_g, k_norm_g), q, k, v)
    o_f, o_b = _hgrn(hgrn_lb_logits, hq, hff, hfb, hi)
    flat = lambda a: a.reshape(t, a.shape[-1])
    x2 = _merge(x1, flat(y_attn), flat(o_f), flat(o_b), flat(hg), flat(ga), flat(gb), hgrn_out_norm_g,
                bf(w_branch_attn[0]), bf(w_branch_hgrn[0]), bf(w_out[0]))
    out = _ffn(x2, ffn2_norm_g, bf(ffn2_w_gate[0]), bf(ffn2_w_up[0]), bf(ffn2_w_down[0]), final_g, True)
    return out.reshape(b, s, d)
```

```python
import functools

import jax
import jax.numpy as jnp
import numpy as np
from jax import lax
from jax.experimental import pallas as pl
from jax.experimental.pallas import tpu as pltpu

F32 = jnp.float32
BF16 = jnp.bfloat16

NORM_EPS = 1e-6
ROPE_THETA = 10000.0
GRID_W = 64
HEAD_DIM = 64
ATTN_HEADS = 8
ATTN_KV_HEADS = 2
ATTN_GROUP = ATTN_HEADS // ATTN_KV_HEADS
HGRN_HEADS = 8
HGRN_DIM = 64
HGRN_CHUNK = 32
ATTN_WIDTH = ATTN_HEADS * HEAD_DIM
KV_WIDTH = ATTN_KV_HEADS * HEAD_DIM
HGRN_WIDTH = HGRN_HEADS * HGRN_DIM
HGRN_PAIR = 2 * HGRN_DIM
HGRN_PAIRS = HGRN_HEADS // 2

V7X_LANES = 128
V7X_MXU_DIM = 256
V7X_VMEM_LIMIT_BYTES = 56 * 1024 * 1024

FFN_ROWS = 512
FFN_COLS = V7X_MXU_DIM
PROJ_ROWS = 512
ATTN_Q_ROWS = 256
ATTN_KEY_CHUNK = V7X_MXU_DIM
ATTN_SCORE_BOUND = 60.0
HGRN_ROWS = 256
MERGE_ROWS = 512


def _compiler_params(semantics):
    return pltpu.CompilerParams(dimension_semantics=semantics, vmem_limit_bytes=V7X_VMEM_LIMIT_BYTES)


def _resident(shape):
    zeros = (0,) * len(shape)
    return pl.BlockSpec(shape, lambda *_: zeros, pipeline_mode=pl.Buffered(1))


def _rmsnorm(x, g):
    return x * lax.rsqrt(jnp.mean(x * x, axis=-1, keepdims=True) + NORM_EPS) * g


def _dot(a, b):
    return jnp.dot(a, b, preferred_element_type=F32)


def _dot_nt(a, b):
    return lax.dot_general(a, b, (((1,), (1,)), ((), ())), preferred_element_type=F32)


def _dot_tn(a, b):
    return lax.dot_general(a, b, (((0,), (0,)), ((), ())), preferred_element_type=F32)


def _split2(x):
    hi = x.astype(BF16)
    lo = (x - hi.astype(F32)).astype(BF16)
    return hi, lo


def _split3(x):
    hi = x.astype(BF16)
    r = x - hi.astype(F32)
    mid = r.astype(BF16)
    lo = (r - mid.astype(F32)).astype(BF16)
    return hi, mid, lo


def _group_mean(x, gmat):
    hi, lo = _split2(x)
    return _dot(hi, gmat) + _dot(lo, gmat)


def _ffn_kernel(x_ref, g_ref, wg_ref, wu_ref, wd_ref, fg_ref, o_ref, *, final_norm):
    x = x_ref[...]
    xn = _rmsnorm(x, g_ref[...]).astype(BF16)
    d_ff = wg_ref.shape[1]
    acc = None
    for c in range(d_ff // FFN_COLS):
        sl = slice(c * FFN_COLS, (c + 1) * FFN_COLS)
        gate = _dot(xn, wg_ref[:, sl])
        up = _dot(xn, wu_ref[:, sl])
        act = (jax.nn.silu(gate) * up).astype(BF16)
        part = _dot(act, wd_ref[sl, :])
        acc = part if acc is None else acc + part
    y = x + 0.5 * acc
    if final_norm:
        y = _rmsnorm(y, fg_ref[...])
    o_ref[...] = y


def _ffn(x, norm_g, w_gate, w_up, w_down, final_g, final_norm):
    t, d = x.shape
    d_ff = w_gate.shape[1]
    assert t % FFN_ROWS == 0 and d_ff % FFN_COLS == 0
    row = pl.BlockSpec((FFN_ROWS, d), lambda i: (i, 0))
    return pl.pallas_call(
        functools.partial(_ffn_kernel, final_norm=final_norm),
        grid=(t // FFN_ROWS,),
        in_specs=[row, _resident((1, d)), _resident((d, d_ff)), _resident((d, d_ff)),
                  _resident((d_ff, d)), _resident((1, d))],
        out_specs=row,
        out_shape=jax.ShapeDtypeStruct((t, d), F32),
        compiler_params=_compiler_params(("parallel",)),
        name="ffn_final" if final_norm else "ffn",
    )(x, norm_g, w_gate, w_up, w_down, final_g)


def _rope(xn, cos_t, sin_t, width):
    half = HEAD_DIM // 2
    lane = lax.broadcasted_iota(jnp.int32, xn.shape, 1)
    first_half = (lane % HEAD_DIM) < half
    partner = jnp.where(first_half, pltpu.roll(xn, width - half, 1), pltpu.roll(xn, half, 1))
    return xn * cos_t + partner * sin_t


def _proj_kernel(x_ref, g_ref, w_ref, qg_ref, kg_ref, gq_ref, gk_ref, cos_ref, sin_ref,
                 q_ref, k_ref, v_ref, hq_ref, ff_ref, fb_ref, hi_ref, hg_ref):
    h = _rmsnorm(x_ref[0], g_ref[...]).astype(BF16)

    def cols(start, width):
        return _dot(h, w_ref[:, start:start + width])

    cos2 = cos_ref[...]
    sin2 = sin_ref[...]
    off = 0
    q = cols(off, ATTN_WIDTH)
    off += ATTN_WIDTH
    reps = ATTN_WIDTH // V7X_LANES
    qn = q * lax.rsqrt(_group_mean(q * q, gq_ref[...]) + NORM_EPS) * qg_ref[...]
    qr = _rope(qn, jnp.concatenate([cos2] * reps, axis=1), jnp.concatenate([sin2] * reps, axis=1),
               ATTN_WIDTH) * (HEAD_DIM ** -0.5)
    for hd in range(ATTN_HEADS):
        q_ref[0, hd] = qr[:, hd * HEAD_DIM:(hd + 1) * HEAD_DIM].astype(BF16)
    k = cols(off, KV_WIDTH)
    off += KV_WIDTH
    kn = k * lax.rsqrt(_group_mean(k * k, gk_ref[...]) + NORM_EPS) * kg_ref[...]
    kr = _rope(kn, cos2, sin2, KV_WIDTH)
    v = cols(off, KV_WIDTH)
    off += KV_WIDTH
    lane = lax.broadcasted_iota(jnp.int32, v.shape, 1)
    for hd in range(ATTN_KV_HEADS):
        k_ref[0, hd] = kr[:, hd * HEAD_DIM:(hd + 1) * HEAD_DIM].astype(BF16)
        v_hd = v if hd == 0 else pltpu.roll(v, KV_WIDTH - hd * HEAD_DIM, 1)
        v_ref[0, hd] = jnp.where(lane < HEAD_DIM, v_hd, 1.0).astype(BF16)
    hq_ref[0] = jax.nn.silu(cols(off, HGRN_WIDTH))
    off += HGRN_WIDTH
    ff_ref[0] = cols(off, HGRN_WIDTH)
    off += HGRN_WIDTH
    fb_ref[0] = cols(off, HGRN_WIDTH)
    off += HGRN_WIDTH
    hi_ref[0] = cols(off, HGRN_WIDTH).astype(BF16)
    off += HGRN_WIDTH
    hg_ref[0] = jax.nn.silu(cols(off, HGRN_WIDTH))


def _group_matrix(width, group):
    idx = np.arange(width) // group
    return jnp.asarray((idx[:, None] == idx[None, :]).astype(np.float32) / group, dtype=BF16)


def _rope_tables(seq_len):
    rows = seq_len // GRID_W
    row = jnp.repeat(jnp.arange(rows, dtype=F32), GRID_W)
    col = jnp.tile(jnp.arange(GRID_W, dtype=F32), rows)
    n_freq = HEAD_DIM // 4
    inv = ROPE_THETA ** (-jnp.arange(n_freq, dtype=F32) / n_freq)
    ang = jnp.concatenate([row[:, None] * inv, col[:, None] * inv], axis=-1)
    cos, sin = jnp.cos(ang), jnp.sin(ang)
    cos2 = jnp.concatenate([cos, cos] * 2, axis=-1)
    sin2 = jnp.concatenate([-sin, sin] * 2, axis=-1)
    return cos2, sin2


def _in_proj(x1, norm_g, w_mix, q_g, k_g):
    b, s, d = x1.shape
    n_in = w_mix.shape[1]
    assert s % PROJ_ROWS == 0 and n_in == ATTN_WIDTH + 2 * KV_WIDTH + 5 * HGRN_WIDTH
    cos2, sin2 = _rope_tables(s)
    tm = PROJ_ROWS
    tok = lambda width: pl.BlockSpec((1, tm, width), lambda bi, i: (bi, i, 0))
    heads = lambda n, width=HEAD_DIM: pl.BlockSpec((1, n, tm, width), lambda bi, i: (bi, 0, i, 0))
    pos = pl.BlockSpec((tm, V7X_LANES), lambda bi, i: (i, 0))
    f32_tok = lambda width: jax.ShapeDtypeStruct((b, s, width), F32)
    out_shape = (
        jax.ShapeDtypeStruct((b, ATTN_HEADS, s, HEAD_DIM), BF16),
        jax.ShapeDtypeStruct((b, ATTN_KV_HEADS, s, HEAD_DIM), BF16),
        jax.ShapeDtypeStruct((b, ATTN_KV_HEADS, s, V7X_LANES), BF16),
        f32_tok(HGRN_WIDTH), f32_tok(HGRN_WIDTH), f32_tok(HGRN_WIDTH),
        jax.ShapeDtypeStruct((b, s, HGRN_WIDTH), BF16),
        f32_tok(HGRN_WIDTH),
    )
    out_specs = (heads(ATTN_HEADS), heads(ATTN_KV_HEADS), heads(ATTN_KV_HEADS, V7X_LANES),
                 tok(HGRN_WIDTH), tok(HGRN_WIDTH), tok(HGRN_WIDTH), tok(HGRN_WIDTH), tok(HGRN_WIDTH))
    return pl.pallas_call(
        _proj_kernel,
        grid=(b, s // tm),
        in_specs=[tok(d), _resident((1, d)), _resident((d, n_in)),
                  _resident((1, ATTN_WIDTH)), _resident((1, KV_WIDTH)),
                  _resident((ATTN_WIDTH, ATTN_WIDTH)), _resident((KV_WIDTH, KV_WIDTH)), pos, pos],
        out_specs=out_specs,
        out_shape=out_shape,
        compiler_params=_compiler_params(("parallel", "parallel")),
        name="in_proj",
    )(x1, norm_g, w_mix, jnp.tile(q_g, (1, ATTN_HEADS)), jnp.tile(k_g, (1, ATTN_KV_HEADS)),
      _group_matrix(ATTN_WIDTH, HEAD_DIM), _group_matrix(KV_WIDTH, HEAD_DIM), cos2, sin2)


def _attn_kernel(bounded_ref, q_ref, k_ref, v_ref, o_ref):
    tq = q_ref.shape[2]
    s_len = k_ref.shape[2]
    q = q_ref[0].reshape(ATTN_GROUP * tq, HEAD_DIM)

    def finish(acc):
        lane = lax.broadcasted_iota(jnp.int32, (tq, V7X_LANES), 1)
        pairs = []
        for r in range(0, ATTN_GROUP, 2):
            lo, hi = acc[r * tq:(r + 1) * tq], acc[(r + 1) * tq:(r + 2) * tq]
            lo_sw, hi_sw = pltpu.roll(lo, HEAD_DIM, 1), pltpu.roll(hi, HEAD_DIM, 1)
            pairs.append(jnp.where(lane < HEAD_DIM, lo / lo_sw, hi_sw / hi))
        o_ref[0] = jnp.concatenate(pairs, axis=1).astype(BF16)

    @pl.when(bounded_ref[0] == 1)
    def _():
        acc = None
        for c in range(s_len // ATTN_KEY_CHUNK):
            ks = slice(c * ATTN_KEY_CHUNK, (c + 1) * ATTN_KEY_CHUNK)
            p = jnp.exp(_dot_nt(q, k_ref[0, 0, ks, :])).astype(BF16)
            part = _dot(p, v_ref[0, 0, ks, :])
            acc = part if acc is None else acc + part
        finish(acc)

    @pl.when(bounded_ref[0] != 1)
    def _():
        s = _dot_nt(q, k_ref[0, 0])
        p = jnp.exp(s - jnp.max(s, axis=-1, keepdims=True)).astype(BF16)
        finish(_dot(p, v_ref[0, 0]))


def _attention(bounded, q, k, v):
    b, _, s, _ = q.shape
    tq = ATTN_Q_ROWS
    assert s % tq == 0 and s % ATTN_KEY_CHUNK == 0
    kv_spec = lambda width: pl.BlockSpec((1, 1, s, width), lambda bi, g, i, *_: (bi, g, 0, 0))
    grid_spec = pltpu.PrefetchScalarGridSpec(
        num_scalar_prefetch=1,
        grid=(b, ATTN_KV_HEADS, s // tq),
        in_specs=[pl.BlockSpec((1, ATTN_GROUP, tq, HEAD_DIM), lambda bi, g, i, *_: (bi, g, i, 0)),
                  kv_spec(HEAD_DIM), kv_spec(V7X_LANES)],
        out_specs=pl.BlockSpec((1, tq, ATTN_GROUP * HEAD_DIM), lambda bi, g, i, *_: (bi, i, g)),
    )
    return pl.pallas_call(
        _attn_kernel,
        grid_spec=grid_spec,
        out_shape=jax.ShapeDtypeStruct((b, s, ATTN_WIDTH), BF16),
        compiler_params=_compiler_params(("parallel", "parallel", "parallel")),
        name="attention",
    )(bounded, q, k, v)


def _scores_bounded(q_g, k_g):
    bound = (HEAD_DIM ** 0.5) * jnp.max(jnp.abs(q_g)) * jnp.max(jnp.abs(k_g))
    return (bound <= ATTN_SCORE_BOUND).astype(jnp.int32).reshape(1)


def _hgrn_prepare(qt, f_logit, lb, reverse):
    rows = qt.shape[0]
    c = HGRN_CHUNK
    n_chunks = rows // c
    f = lb + (1.0 - lb) * jax.nn.sigmoid(f_logit)
    kk = 1.0 - f
    log_f = jnp.log(f)
    r_i = lax.broadcasted_iota(jnp.int32, (rows, rows), 0)
    c_i = lax.broadcasted_iota(jnp.int32, (rows, rows), 1)
    same_chunk = (r_i // c) == (c_i // c)
    causal = same_chunk & ((c_i >= r_i) if reverse else (c_i <= r_i))
    tri = jnp.where(causal, 1.0, 0.0).astype(BF16)
    hi, mid, lo = _split3(log_f)
    bcum = _dot(tri, hi) + _dot(tri, mid) + _dot(tri, lo)
    b3 = bcum.reshape(n_chunks, c, bcum.shape[-1])
    mid_row = c // 2 if reverse else c // 2 - 1
    last_row = 0 if reverse else c - 1
    b_mid = b3[:, mid_row:mid_row + 1, :]
    b_last = b3[:, last_row:last_row + 1, :]
    q3 = qt.reshape(b3.shape)
    k3 = kk.reshape(b3.shape)
    return dict(
        causal=causal,
        q_mid=(q3 * jnp.exp(b3 - b_mid)).reshape(bcum.shape).astype(BF16),
        k_mid=(k3 * jnp.exp(b_mid - b3)).reshape(bcum.shape).astype(BF16),
        k_end=(k3 * jnp.exp(b_last - b3)).reshape(bcum.shape).astype(BF16),
        q_in=(qt * jnp.exp(bcum)).astype(BF16),
        decay=jnp.exp(b_last),
    )


def _pair_slice(p):
    return slice(p * HGRN_PAIR, (p + 1) * HGRN_PAIR)


def _hgrn_intra(ops, v, p):
    rows = v.shape[0]
    lower_head = lax.broadcasted_iota(jnp.int32, (rows, HGRN_PAIR), 1) < HGRN_DIM
    zero_bf = jnp.zeros((), BF16)
    sl = _pair_slice(p)
    o_heads = []
    for head_is_lower in (True, False):
        k_sel = jnp.where(lower_head == head_is_lower, ops["k_mid"][:, sl], zero_bf)
        scores = jnp.where(ops["causal"], _dot_nt(ops["q_mid"][:, sl], k_sel), 0.0).astype(BF16)
        o_heads.append(_dot(scores, v[:, sl]))
    return jnp.where(lower_head, o_heads[0], o_heads[1])


def _hgrn_contribs(ops, v):
    c = HGRN_CHUNK
    n_chunks = v.shape[0] // c
    return [[_dot_tn(v[n * c:(n + 1) * c, _pair_slice(p)], ops["k_end"][n * c:(n + 1) * c, _pair_slice(p)])
             for n in range(n_chunks)] for p in range(HGRN_PAIRS)]


def _hgrn_scan(ops, contribs, state_ref, reverse):
    n_chunks = len(contribs[0])
    sr = lax.broadcasted_iota(jnp.int32, (HGRN_PAIR, HGRN_PAIR), 0)
    sc = lax.broadcasted_iota(jnp.int32, (HGRN_PAIR, HGRN_PAIR), 1)
    same_head = (sr // HGRN_DIM) == (sc // HGRN_DIM)
    order = range(n_chunks - 1, -1, -1) if reverse else range(n_chunks)
    states = [[None] * n_chunks for _ in range(HGRN_PAIRS)]
    for p in range(HGRN_PAIRS):
        st = state_ref[p]
        for n in order:
            states[p][n] = st.astype(BF16)
            st = ops["decay"][n, :, _pair_slice(p)] * st + jnp.where(same_head, contribs[p][n], 0.0)
        state_ref[p] = st
    return states


def _hgrn_inter(ops, states, p):
    c = HGRN_CHUNK
    parts = [_dot_nt(ops["q_in"][n * c:(n + 1) * c, _pair_slice(p)], st) for n, st in enumerate(states[p])]
    return jnp.concatenate(parts, axis=0)


def _hgrn_kernel(lbl_ref, qf_ref, ff_ref, vf_ref, qb_ref, fb_ref, vb_ref, of_ref, ob_ref,
                 st_f_ref, st_b_ref):
    @pl.when(pl.program_id(1) == 0)
    def _():
        st_f_ref[...] = jnp.zeros_like(st_f_ref)
        st_b_ref[...] = jnp.zeros_like(st_b_ref)

    logits = lbl_ref[...]
    e = jnp.exp(logits - jnp.max(logits, axis=1, keepdims=True))
    lb = e[:, 0, :] / jnp.sum(e, axis=1)
    v_f, v_b = vf_ref[0], vb_ref[0]
    ops_f = _hgrn_prepare(qf_ref[0], ff_ref[0], lb[0:1], False)
    ops_b = _hgrn_prepare(qb_ref[0], fb_ref[0], lb[1:2], True)
    con_f = _hgrn_contribs(ops_f, v_f)
    con_b = _hgrn_contribs(ops_b, v_b)
    states_f = _hgrn_scan(ops_f, con_f, st_f_ref, False)
    states_b = _hgrn_scan(ops_b, con_b, st_b_ref, True)
    pairs = range(HGRN_PAIRS)
    inter_f = [_hgrn_inter(ops_f, states_f, p) for p in pairs]
    inter_b = [_hgrn_inter(ops_b, states_b, p) for p in pairs]
    intra_f = [_hgrn_intra(ops_f, v_f, p) for p in pairs]
    intra_b = [_hgrn_intra(ops_b, v_b, p) for p in pairs]
    of_ref[0] = jnp.concatenate([a + b for a, b in zip(intra_f, inter_f)], axis=1)
    ob_ref[0] = jnp.concatenate([a + b for a, b in zip(intra_b, inter_b)], axis=1)


def _hgrn(lb_logits, hq, hff, hfb, hi):
    b, s, w = hq.shape
    r = HGRN_ROWS
    assert s % r == 0 and r % HGRN_CHUNK == 0
    nb = s // r
    fwd = pl.BlockSpec((1, r, w), lambda bi, j: (bi, j, 0))
    bwd = pl.BlockSpec((1, r, w), lambda bi, j: (bi, nb - 1 - j, 0))
    state = pltpu.VMEM((HGRN_PAIRS, HGRN_PAIR, HGRN_PAIR), F32)
    return pl.pallas_call(
        _hgrn_kernel,
        grid=(b, nb),
        in_specs=[_resident(lb_logits.shape), fwd, fwd, fwd, bwd, bwd, bwd],
        out_specs=(fwd, bwd),
        out_shape=(jax.ShapeDtypeStruct((b, s, w), F32), jax.ShapeDtypeStruct((b, s, w), F32)),
        scratch_shapes=[state, state],
        compiler_params=_compiler_params(("parallel", "arbitrary")),
        name="hgrn",
    )(lb_logits, hq, hff, hi, hq, hfb, hi)


def _merge_kernel(x_ref, ya_ref, of_ref, ob_ref, hg_ref, mg_ref, wg_ref, ng_ref, gm_ref,
                  wa_ref, wb_ref, wo_ref, o_ref):
    x = x_ref[...]
    d = x.shape[-1]
    h = _rmsnorm(x, mg_ref[...]).astype(BF16)
    gate_a = jax.nn.sigmoid(_dot(h, wg_ref[:, :d]))
    gate_b = jax.nn.sigmoid(_dot(h, wg_ref[:, d:]))
    o = of_ref[...] + ob_ref[...]
    on = o * lax.rsqrt(_group_mean(o * o, gm_ref[...]) + NORM_EPS) * ng_ref[...]
    yb = (on * hg_ref[...]).astype(BF16)
    merged = gate_a * _dot(ya_ref[...], wa_ref[...]) + gate_b * _dot(yb, wb_ref[...])
    o_ref[...] = x + _dot(merged.astype(BF16), wo_ref[...])


def _merge(x1, y_attn, o_f, o_b, hg, mix_g, w_gates, norm_g, w_a, w_b, w_out):
    t, d = x1.shape
    tm = MERGE_ROWS
    assert t % tm == 0
    row = lambda width: pl.BlockSpec((tm, width), lambda i: (i, 0))
    return pl.pallas_call(
        _merge_kernel,
        grid=(t // tm,),
        in_specs=[row(d), row(ATTN_WIDTH), row(HGRN_WIDTH), row(HGRN_WIDTH), row(HGRN_WIDTH),
                  _resident((1, d)), _resident((d, 2 * d)),
                  _resident((1, HGRN_WIDTH)), _resident((HGRN_WIDTH, HGRN_WIDTH)),
                  _resident((ATTN_WIDTH, d)), _resident((HGRN_WIDTH, d)), _resident((d, d))],
        out_specs=row(d),
        out_shape=jax.ShapeDtypeStruct((t, d), F32),
        compiler_params=_compiler_params(("parallel",)),
        name="merge",
    )(x1, y_attn, o_f, o_b, hg, mix_g, w_gates, norm_g, _group_matrix(HGRN_WIDTH, HGRN_DIM), w_a, w_b, w_out)


def kernel(x, ffn1_norm_g, ffn1_w_gate, ffn1_w_up, ffn1_w_down, mix_norm_g, w_in, q_norm_g, k_norm_g,
           hgrn_lb_logits, hgrn_out_norm_g, w_branch_attn, w_branch_hgrn, w_out, ffn2_norm_g,
           ffn2_w_gate, ffn2_w_up, ffn2_w_down, final_norm_g):
    b, s, d = x.shape
    t = b * s
    depth = w_in.shape[0]
    assert depth == 1 and hgrn_lb_logits.shape[1] == depth + 1
    bf = lambda w: w.astype(BF16)
    final_g = final_norm_g.reshape(1, d)
    n_mix = w_in.shape[2] - 2 * d

    x1 = _ffn(x.reshape(t, d), ffn1_norm_g, bf(ffn1_w_gate[0]), bf(ffn1_w_up[0]), bf(ffn1_w_down[0]),
              final_g, False)
    q, k, v, hq, hff, hfb, hi, hg = _in_proj(
        x1.reshape(b, s, d), mix_norm_g, bf(w_in[0, :, :n_mix]), q_norm_g, k_norm_g)
    y_attn = _attention(_scores_bounded(q_norm_g, k_norm_g), q, k, v)
    o_f, o_b = _hgrn(hgrn_lb_logits, hq, hff, hfb, hi)
    flat = lambda a: a.reshape(t, a.shape[-1])
    x2 = _merge(x1, flat(y_attn), flat(o_f), flat(o_b), flat(hg), mix_norm_g, bf(w_in[0, :, n_mix:]),
                hgrn_out_norm_g, bf(w_branch_attn[0]), bf(w_branch_hgrn[0]), bf(w_out[0]))
    out = _ffn(x2, ffn2_norm_g, bf(ffn2_w_gate[0]), bf(ffn2_w_up[0]), bf(ffn2_w_down[0]), final_g, True)
    return out.reshape(b, s, d)
```

```python
import functools

import jax
import jax.numpy as jnp
import numpy as np
from jax import lax
from jax.experimental import pallas as pl
from jax.experimental.pallas import tpu as pltpu

F32 = jnp.float32
BF16 = jnp.bfloat16

NORM_EPS = 1e-6
ROPE_THETA = 10000.0
GRID_W = 64
HEAD_DIM = 64
ATTN_HEADS = 8
ATTN_KV_HEADS = 2
ATTN_GROUP = ATTN_HEADS // ATTN_KV_HEADS
HGRN_HEADS = 8
HGRN_DIM = 64
HGRN_CHUNK = 32
ATTN_WIDTH = ATTN_HEADS * HEAD_DIM
KV_WIDTH = ATTN_KV_HEADS * HEAD_DIM
HGRN_WIDTH = HGRN_HEADS * HGRN_DIM
HGRN_PAIR = 2 * HGRN_DIM
HGRN_PAIRS = HGRN_HEADS // 2

V7X_LANES = 128
V7X_MXU_DIM = 256
V7X_VMEM_LIMIT_BYTES = 56 * 1024 * 1024

FFN_ROWS = 512
FFN_COLS = V7X_MXU_DIM
PROJ_ROWS = 512
ATTN_Q_ROWS = 256
ATTN_KEY_SPLIT = 2
ATTN_SCORE_BOUND = 60.0
HGRN_ROWS = 256
MERGE_ROWS = 512


def _compiler_params(semantics):
    return pltpu.CompilerParams(dimension_semantics=semantics, vmem_limit_bytes=V7X_VMEM_LIMIT_BYTES)


def _resident(shape):
    zeros = (0,) * len(shape)
    return pl.BlockSpec(shape, lambda *_: zeros, pipeline_mode=pl.Buffered(1))


def _rmsnorm(x, g):
    return x * lax.rsqrt(jnp.mean(x * x, axis=-1, keepdims=True) + NORM_EPS) * g


def _dot(a, b):
    return jnp.dot(a, b, preferred_element_type=F32)


def _dot_nt(a, b):
    return lax.dot_general(a, b, (((1,), (1,)), ((), ())), preferred_element_type=F32)


def _dot_tn(a, b):
    return lax.dot_general(a, b, (((0,), (0,)), ((), ())), preferred_element_type=F32)


def _split2(x):
    hi = x.astype(BF16)
    lo = (x - hi.astype(F32)).astype(BF16)
    return hi, lo


def _split3(x):
    hi = x.astype(BF16)
    r = x - hi.astype(F32)
    mid = r.astype(BF16)
    lo = (r - mid.astype(F32)).astype(BF16)
    return hi, mid, lo


def _group_mean(x, gmat):
    hi, lo = _split2(x)
    return _dot(hi, gmat) + _dot(lo, gmat)


def _ffn_kernel(x_ref, g_ref, wg_ref, wu_ref, wd_ref, fg_ref, o_ref, *, final_norm):
    x = x_ref[...]
    xn = _rmsnorm(x, g_ref[...]).astype(BF16)
    d_ff = wg_ref.shape[1]
    acc = None
    for c in range(d_ff // FFN_COLS):
        sl = slice(c * FFN_COLS, (c + 1) * FFN_COLS)
        gate = _dot(xn, wg_ref[:, sl])
        up = _dot(xn, wu_ref[:, sl])
        act = (jax.nn.silu(gate) * up).astype(BF16)
        part = _dot(act, wd_ref[sl, :])
        acc = part if acc is None else acc + part
    y = x + 0.5 * acc
    if final_norm:
        y = _rmsnorm(y, fg_ref[...])
    o_ref[...] = y


def _ffn(x, norm_g, w_gate, w_up, w_down, final_g, final_norm):
    t, d = x.shape
    d_ff = w_gate.shape[1]
    assert t % FFN_ROWS == 0 and d_ff % FFN_COLS == 0
    row = pl.BlockSpec((FFN_ROWS, d), lambda i: (i, 0))
    return pl.pallas_call(
        functools.partial(_ffn_kernel, final_norm=final_norm),
        grid=(t // FFN_ROWS,),
        in_specs=[row, _resident((1, d)), _resident((d, d_ff)), _resident((d, d_ff)),
                  _resident((d_ff, d)), _resident((1, d))],
        out_specs=row,
        out_shape=jax.ShapeDtypeStruct((t, d), F32),
        compiler_params=_compiler_params(("parallel",)),
        name="ffn_final" if final_norm else "ffn",
    )(x, norm_g, w_gate, w_up, w_down, final_g)


def _rope(xn, cos_t, sin_t, width):
    half = HEAD_DIM // 2
    lane = lax.broadcasted_iota(jnp.int32, xn.shape, 1)
    first_half = (lane % HEAD_DIM) < half
    partner = jnp.where(first_half, pltpu.roll(xn, width - half, 1), pltpu.roll(xn, half, 1))
    return xn * cos_t + partner * sin_t


def _proj_kernel(x_ref, g_ref, w_ref, qg_ref, kg_ref, gq_ref, gk_ref, cos_ref, sin_ref,
                 q_ref, k_ref, v_ref, hq_ref, ff_ref, fb_ref, hi_ref, hg_ref):
    h = _rmsnorm(x_ref[0], g_ref[...]).astype(BF16)

    def cols(start, width):
        return _dot(h, w_ref[:, start:start + width])

    cos2 = cos_ref[...]
    sin2 = sin_ref[...]
    off = 0
    q = cols(off, ATTN_WIDTH)
    off += ATTN_WIDTH
    reps = ATTN_WIDTH // V7X_LANES
    qn = q * lax.rsqrt(_group_mean(q * q, gq_ref[...]) + NORM_EPS) * qg_ref[...]
    qr = _rope(qn, jnp.concatenate([cos2] * reps, axis=1), jnp.concatenate([sin2] * reps, axis=1),
               ATTN_WIDTH) * (HEAD_DIM ** -0.5)
    for hd in range(ATTN_HEADS):
        q_ref[0, hd] = qr[:, hd * HEAD_DIM:(hd + 1) * HEAD_DIM].astype(BF16)
    k = cols(off, KV_WIDTH)
    off += KV_WIDTH
    kn = k * lax.rsqrt(_group_mean(k * k, gk_ref[...]) + NORM_EPS) * kg_ref[...]
    kr = _rope(kn, cos2, sin2, KV_WIDTH)
    v = cols(off, KV_WIDTH)
    off += KV_WIDTH
    lane = lax.broadcasted_iota(jnp.int32, v.shape, 1)
    for hd in range(ATTN_KV_HEADS):
        k_ref[0, hd] = kr[:, hd * HEAD_DIM:(hd + 1) * HEAD_DIM].astype(BF16)
        v_hd = v if hd == 0 else pltpu.roll(v, KV_WIDTH - hd * HEAD_DIM, 1)
        v_ref[0, hd] = jnp.where(lane < HEAD_DIM, v_hd, 1.0).T.astype(BF16)
    hq_ref[0] = jax.nn.silu(cols(off, HGRN_WIDTH))
    off += HGRN_WIDTH
    ff_ref[0] = cols(off, HGRN_WIDTH)
    off += HGRN_WIDTH
    fb_ref[0] = cols(off, HGRN_WIDTH)
    off += HGRN_WIDTH
    hi_ref[0] = cols(off, HGRN_WIDTH).astype(BF16)
    off += HGRN_WIDTH
    hg_ref[0] = jax.nn.silu(cols(off, HGRN_WIDTH))


def _group_matrix(width, group):
    idx = np.arange(width) // group
    return jnp.asarray((idx[:, None] == idx[None, :]).astype(np.float32) / group, dtype=BF16)


def _rope_tables(seq_len):
    rows = seq_len // GRID_W
    row = jnp.repeat(jnp.arange(rows, dtype=F32), GRID_W)
    col = jnp.tile(jnp.arange(GRID_W, dtype=F32), rows)
    n_freq = HEAD_DIM // 4
    inv = ROPE_THETA ** (-jnp.arange(n_freq, dtype=F32) / n_freq)
    ang = jnp.concatenate([row[:, None] * inv, col[:, None] * inv], axis=-1)
    cos, sin = jnp.cos(ang), jnp.sin(ang)
    cos2 = jnp.concatenate([cos, cos] * 2, axis=-1)
    sin2 = jnp.concatenate([-sin, sin] * 2, axis=-1)
    return cos2, sin2


def _in_proj(x1, norm_g, w_mix, q_g, k_g):
    b, s, d = x1.shape
    n_in = w_mix.shape[1]
    assert s % PROJ_ROWS == 0 and n_in == ATTN_WIDTH + 2 * KV_WIDTH + 5 * HGRN_WIDTH
    cos2, sin2 = _rope_tables(s)
    tm = PROJ_ROWS
    tok = lambda width: pl.BlockSpec((1, tm, width), lambda bi, i: (bi, i, 0))
    heads = lambda n, width=HEAD_DIM: pl.BlockSpec((1, n, tm, width), lambda bi, i: (bi, 0, i, 0))
    pos = pl.BlockSpec((tm, V7X_LANES), lambda bi, i: (i, 0))
    f32_tok = lambda width: jax.ShapeDtypeStruct((b, s, width), F32)
    out_shape = (
        jax.ShapeDtypeStruct((b, ATTN_HEADS, s, HEAD_DIM), BF16),
        jax.ShapeDtypeStruct((b, ATTN_KV_HEADS, s, HEAD_DIM), BF16),
        jax.ShapeDtypeStruct((b, ATTN_KV_HEADS, V7X_LANES, s), BF16),
        f32_tok(HGRN_WIDTH), f32_tok(HGRN_WIDTH), f32_tok(HGRN_WIDTH),
        jax.ShapeDtypeStruct((b, s, HGRN_WIDTH), BF16),
        f32_tok(HGRN_WIDTH),
    )
    vt_spec = pl.BlockSpec((1, ATTN_KV_HEADS, V7X_LANES, tm), lambda bi, i: (bi, 0, 0, i))
    out_specs = (heads(ATTN_HEADS), heads(ATTN_KV_HEADS), vt_spec,
                 tok(HGRN_WIDTH), tok(HGRN_WIDTH), tok(HGRN_WIDTH), tok(HGRN_WIDTH), tok(HGRN_WIDTH))
    return pl.pallas_call(
        _proj_kernel,
        grid=(b, s // tm),
        in_specs=[tok(d), _resident((1, d)), _resident((d, n_in)),
                  _resident((1, ATTN_WIDTH)), _resident((1, KV_WIDTH)),
                  _resident((ATTN_WIDTH, ATTN_WIDTH)), _resident((KV_WIDTH, KV_WIDTH)), pos, pos],
        out_specs=out_specs,
        out_shape=out_shape,
        compiler_params=_compiler_params(("parallel", "parallel")),
        name="in_proj",
    )(x1, norm_g, w_mix, jnp.tile(q_g, (1, ATTN_HEADS)), jnp.tile(k_g, (1, ATTN_KV_HEADS)),
      _group_matrix(ATTN_WIDTH, HEAD_DIM), _group_matrix(KV_WIDTH, HEAD_DIM), cos2, sin2)


def _attn_kernel(bounded_ref, q_ref, k_ref, vt_ref, o_ref):
    tq = q_ref.shape[2]
    s_len = k_ref.shape[2]
    q = q_ref[0].reshape(ATTN_GROUP * tq, HEAD_DIM)

    def finish(accs):
        lane = lax.broadcasted_iota(jnp.int32, (tq, V7X_LANES), 1)
        pairs = []
        for r in range(0, ATTN_GROUP, 2):
            lo, hi = accs[r].T, accs[r + 1].T
            lo_sw, hi_sw = pltpu.roll(lo, HEAD_DIM, 1), pltpu.roll(hi, HEAD_DIM, 1)
            pairs.append(jnp.where(lane < HEAD_DIM, lo / lo_sw, hi_sw / hi))
        o_ref[0] = jnp.concatenate(pairs, axis=1).astype(BF16)

    @pl.when(bounded_ref[0] == 1)
    def _():
        accs = [None] * ATTN_GROUP
        kc = s_len // ATTN_KEY_SPLIT
        for c in range(ATTN_KEY_SPLIT):
            ks = slice(c * kc, (c + 1) * kc)
            p_t = jnp.exp(_dot_nt(k_ref[0, 0, ks, :], q)).astype(BF16)
            vt_c = vt_ref[0, 0, :, ks]
            for r in range(ATTN_GROUP):
                part = _dot(vt_c, p_t[:, r * tq:(r + 1) * tq])
                accs[r] = part if accs[r] is None else accs[r] + part
        finish(accs)

    @pl.when(bounded_ref[0] != 1)
    def _():
        s_t = _dot_nt(k_ref[0, 0], q)
        p_t = jnp.exp(s_t - jnp.max(s_t, axis=0, keepdims=True)).astype(BF16)
        finish([_dot(vt_ref[0, 0], p_t[:, r * tq:(r + 1) * tq]) for r in range(ATTN_GROUP)])


def _attention(bounded, q, k, v):
    b, _, s, _ = q.shape
    tq = ATTN_Q_ROWS
    assert s % tq == 0 and s % (ATTN_KEY_SPLIT * V7X_LANES) == 0
    k_spec = pl.BlockSpec((1, 1, s, HEAD_DIM), lambda bi, g, i, *_: (bi, g, 0, 0))
    vt_spec = pl.BlockSpec((1, 1, V7X_LANES, s), lambda bi, g, i, *_: (bi, g, 0, 0))
    grid_spec = pltpu.PrefetchScalarGridSpec(
        num_scalar_prefetch=1,
        grid=(b, ATTN_KV_HEADS, s // tq),
        in_specs=[pl.BlockSpec((1, ATTN_GROUP, tq, HEAD_DIM), lambda bi, g, i, *_: (bi, g, i, 0)),
                  k_spec, vt_spec],
        out_specs=pl.BlockSpec((1, tq, ATTN_GROUP * HEAD_DIM), lambda bi, g, i, *_: (bi, i, g)),
    )
    return pl.pallas_call(
        _attn_kernel,
        grid_spec=grid_spec,
        out_shape=jax.ShapeDtypeStruct((b, s, ATTN_WIDTH), BF16),
        compiler_params=_compiler_params(("parallel", "parallel", "parallel")),
        name="attention",
    )(bounded, q, k, v)


def _scores_bounded(q_g, k_g):
    bound = (HEAD_DIM ** 0.5) * jnp.max(jnp.abs(q_g)) * jnp.max(jnp.abs(k_g))
    return (bound <= ATTN_SCORE_BOUND).astype(jnp.int32).reshape(1)


def _hgrn_prepare(qt, f_logit, lb, reverse):
    rows = qt.shape[0]
    c = HGRN_CHUNK
    n_chunks = rows // c
    f = lb + (1.0 - lb) * jax.nn.sigmoid(f_logit)
    kk = 1.0 - f
    log_f = jnp.log(f)
    r_i = lax.broadcasted_iota(jnp.int32, (rows, rows), 0)
    c_i = lax.broadcasted_iota(jnp.int32, (rows, rows), 1)
    same_chunk = (r_i // c) == (c_i // c)
    causal = same_chunk & ((c_i >= r_i) if reverse else (c_i <= r_i))
    tri = jnp.where(causal, 1.0, 0.0).astype(BF16)
    hi, mid, lo = _split3(log_f)
    bcum = _dot(tri, hi) + _dot(tri, mid) + _dot(tri, lo)
    b3 = bcum.reshape(n_chunks, c, bcum.shape[-1])
    mid_row = c // 2 if reverse else c // 2 - 1
    last_row = 0 if reverse else c - 1
    b_mid = b3[:, mid_row:mid_row + 1, :]
    b_last = b3[:, last_row:last_row + 1, :]
    q3 = qt.reshape(b3.shape)
    k3 = kk.reshape(b3.shape)
    return dict(
        causal=causal,
        q_mid=(q3 * jnp.exp(b3 - b_mid)).reshape(bcum.shape).astype(BF16),
        k_mid=(k3 * jnp.exp(b_mid - b3)).reshape(bcum.shape).astype(BF16),
        k_end=(k3 * jnp.exp(b_last - b3)).reshape(bcum.shape).astype(BF16),
        q_in=(qt * jnp.exp(bcum)).astype(BF16),
        decay=jnp.exp(b_last),
    )


def _pair_slice(p):
    return slice(p * HGRN_PAIR, (p + 1) * HGRN_PAIR)


def _hgrn_intra(ops, v, p):
    rows = v.shape[0]
    lower_head = lax.broadcasted_iota(jnp.int32, (rows, HGRN_PAIR), 1) < HGRN_DIM
    zero_bf = jnp.zeros((), BF16)
    sl = _pair_slice(p)
    o_heads = []
    for head_is_lower in (True, False):
        k_sel = jnp.where(lower_head == head_is_lower, ops["k_mid"][:, sl], zero_bf)
        scores = jnp.where(ops["causal"], _dot_nt(ops["q_mid"][:, sl], k_sel), 0.0).astype(BF16)
        o_heads.append(_dot(scores, v[:, sl]))
    return jnp.where(lower_head, o_heads[0], o_heads[1])


def _hgrn_contribs(ops, v):
    c = HGRN_CHUNK
    n_chunks = v.shape[0] // c
    return [[_dot_tn(v[n * c:(n + 1) * c, _pair_slice(p)], ops["k_end"][n * c:(n + 1) * c, _pair_slice(p)])
             for n in range(n_chunks)] for p in range(HGRN_PAIRS)]


def _hgrn_scan(ops, contribs, state_ref, reverse):
    n_chunks = len(contribs[0])
    sr = lax.broadcasted_iota(jnp.int32, (HGRN_PAIR, HGRN_PAIR), 0)
    sc = lax.broadcasted_iota(jnp.int32, (HGRN_PAIR, HGRN_PAIR), 1)
    same_head = (sr // HGRN_DIM) == (sc // HGRN_DIM)
    order = range(n_chunks - 1, -1, -1) if reverse else range(n_chunks)
    states = [[None] * n_chunks for _ in range(HGRN_PAIRS)]
    for p in range(HGRN_PAIRS):
        st = state_ref[p]
        for n in order:
            states[p][n] = st.astype(BF16)
            st = ops["decay"][n, :, _pair_slice(p)] * st + jnp.where(same_head, contribs[p][n], 0.0)
        state_ref[p] = st
    return states


def _hgrn_inter(ops, states, p):
    c = HGRN_CHUNK
    parts = [_dot_nt(ops["q_in"][n * c:(n + 1) * c, _pair_slice(p)], st) for n, st in enumerate(states[p])]
    return jnp.concatenate(parts, axis=0)


def _hgrn_kernel(lbl_ref, qf_ref, ff_ref, vf_ref, qb_ref, fb_ref, vb_ref, of_ref, ob_ref,
                 st_f_ref, st_b_ref):
    @pl.when(pl.program_id(1) == 0)
    def _():
        st_f_ref[...] = jnp.zeros_like(st_f_ref)
        st_b_ref[...] = jnp.zeros_like(st_b_ref)

    logits = lbl_ref[...]
    e = jnp.exp(logits - jnp.max(logits, axis=1, keepdims=True))
    lb = e[:, 0, :] / jnp.sum(e, axis=1)
    v_f, v_b = vf_ref[0], vb_ref[0]
    ops_f = _hgrn_prepare(qf_ref[0], ff_ref[0], lb[0:1], False)
    ops_b = _hgrn_prepare(qb_ref[0], fb_ref[0], lb[1:2], True)
    con_f = _hgrn_contribs(ops_f, v_f)
    con_b = _hgrn_contribs(ops_b, v_b)
    states_f = _hgrn_scan(ops_f, con_f, st_f_ref, False)
    states_b = _hgrn_scan(ops_b, con_b, st_b_ref, True)
    pairs = range(HGRN_PAIRS)
    inter_f = [_hgrn_inter(ops_f, states_f, p) for p in pairs]
    inter_b = [_hgrn_inter(ops_b, states_b, p) for p in pairs]
    intra_f = [_hgrn_intra(ops_f, v_f, p) for p in pairs]
    intra_b = [_hgrn_intra(ops_b, v_b, p) for p in pairs]
    of_ref[0] = jnp.concatenate([a + b for a, b in zip(intra_f, inter_f)], axis=1)
    ob_ref[0] = jnp.concatenate([a + b for a, b in zip(intra_b, inter_b)], axis=1)


def _hgrn(lb_logits, hq, hff, hfb, hi):
    b, s, w = hq.shape
    r = HGRN_ROWS
    assert s % r == 0 and r % HGRN_CHUNK == 0
    nb = s // r
    fwd = pl.BlockSpec((1, r, w), lambda bi, j: (bi, j, 0))
    bwd = pl.BlockSpec((1, r, w), lambda bi, j: (bi, nb - 1 - j, 0))
    state = pltpu.VMEM((HGRN_PAIRS, HGRN_PAIR, HGRN_PAIR), F32)
    return pl.pallas_call(
        _hgrn_kernel,
        grid=(b, nb),
        in_specs=[_resident(lb_logits.shape), fwd, fwd, fwd, bwd, bwd, bwd],
        out_specs=(fwd, bwd),
        out_shape=(jax.ShapeDtypeStruct((b, s, w), F32), jax.ShapeDtypeStruct((b, s, w), F32)),
        scratch_shapes=[state, state],
        compiler_params=_compiler_params(("parallel", "arbitrary")),
        name="hgrn",
    )(lb_logits, hq, hff, hi, hq, hfb, hi)


def _merge_kernel(x_ref, ya_ref, of_ref, ob_ref, hg_ref, mg_ref, wg_ref, ng_ref, gm_ref,
                  wa_ref, wb_ref, wo_ref, o_ref):
    x = x_ref[...]
    d = x.shape[-1]
    h = _rmsnorm(x, mg_ref[...]).astype(BF16)
    gate_a = jax.nn.sigmoid(_dot(h, wg_ref[:, :d]))
    gate_b = jax.nn.sigmoid(_dot(h, wg_ref[:, d:]))
    o = of_ref[...] + ob_ref[...]
    on = o * lax.rsqrt(_group_mean(o * o, gm_ref[...]) + NORM_EPS) * ng_ref[...]
    yb = (on * hg_ref[...]).astype(BF16)
    merged = gate_a * _dot(ya_ref[...], wa_ref[...]) + gate_b * _dot(yb, wb_ref[...])
    o_ref[...] = x + _dot(merged.astype(BF16), wo_ref[...])


def _merge(x1, y_attn, o_f, o_b, hg, mix_g, w_gates, norm_g, w_a, w_b, w_out):
    t, d = x1.shape
    tm = MERGE_ROWS
    assert t % tm == 0
    row = lambda width: pl.BlockSpec((tm, width), lambda i: (i, 0))
    return pl.pallas_call(
        _merge_kernel,
        grid=(t // tm,),
        in_specs=[row(d), row(ATTN_WIDTH), row(HGRN_WIDTH), row(HGRN_WIDTH), row(HGRN_WIDTH),
                  _resident((1, d)), _resident((d, 2 * d)),
                  _resident((1, HGRN_WIDTH)), _resident((HGRN_WIDTH, HGRN_WIDTH)),
                  _resident((ATTN_WIDTH, d)), _resident((HGRN_WIDTH, d)), _resident((d, d))],
        out_specs=row(d),
        out_shape=jax.ShapeDtypeStruct((t, d), F32),
        compiler_params=_compiler_params(("parallel",)),
        name="merge",
    )(x1, y_attn, o_f, o_b, hg, mix_g, w_gates, norm_g, _group_matrix(HGRN_WIDTH, HGRN_DIM), w_a, w_b, w_out)


def kernel(x, ffn1_norm_g, ffn1_w_gate, ffn1_w_up, ffn1_w_down, mix_norm_g, w_in, q_norm_g, k_norm_g,
           hgrn_lb_logits, hgrn_out_norm_g, w_branch_attn, w_branch_hgrn, w_out, ffn2_norm_g,
           ffn2_w_gate, ffn2_w_up, ffn2_w_down, final_norm_g):
    b, s, d = x.shape
    t = b * s
    depth = w_in.shape[0]
    assert depth == 1 and hgrn_lb_logits.shape[1] == depth + 1
    bf = lambda w: w.astype(BF16)
    final_g = final_norm_g.reshape(1, d)
    n_mix = w_in.shape[2] - 2 * d

    x1 = _ffn(x.reshape(t, d), ffn1_norm_g, bf(ffn1_w_gate[0]), bf(ffn1_w_up[0]), bf(ffn1_w_down[0]),
              final_g, False)
    q, k, v, hq, hff, hfb, hi, hg = _in_proj(
        x1.reshape(b, s, d), mix_norm_g, bf(w_in[0, :, :n_mix]), q_norm_g, k_norm_g)
    y_attn = _attention(_scores_bounded(q_norm_g, k_norm_g), q, k, v)
    o_f, o_b = _hgrn(hgrn_lb_logits, hq, hff, hfb, hi)
    flat = lambda a: a.reshape(t, a.shape[-1])
    x2 = _merge(x1, flat(y_attn), flat(o_f), flat(o_b), flat(hg), mix_norm_g, bf(w_in[0, :, n_mix:]),
                hgrn_out_norm_g, bf(w_branch_attn[0]), bf(w_branch_hgrn[0]), bf(w_out[0]))
    out = _ffn(x2, ffn2_norm_g, bf(ffn2_w_gate[0]), bf(ffn2_w_up[0]), bf(ffn2_w_down[0]), final_g, True)
    return out.reshape(b, s, d)
```

```python
import functools

import jax
import jax.numpy as jnp
import numpy as np
from jax import lax
from jax.experimental import pallas as pl
from jax.experimental.pallas import tpu as pltpu

F32 = jnp.float32
BF16 = jnp.bfloat16

NORM_EPS = 1e-6
ROPE_THETA = 10000.0
GRID_W = 64
HEAD_DIM = 64
ATTN_HEADS = 8
ATTN_KV_HEADS = 2
ATTN_GROUP = ATTN_HEADS // ATTN_KV_HEADS
HGRN_HEADS = 8
HGRN_DIM = 64
HGRN_CHUNK = 32
ATTN_WIDTH = ATTN_HEADS * HEAD_DIM
KV_WIDTH = ATTN_KV_HEADS * HEAD_DIM
HGRN_WIDTH = HGRN_HEADS * HGRN_DIM
HGRN_PAIR = 2 * HGRN_DIM
HGRN_PAIRS = HGRN_HEADS // 2

V7X_LANES = 128
V7X_MXU_DIM = 256
V7X_VMEM_LIMIT_BYTES = 56 * 1024 * 1024

FFN_ROWS = 512
FFN_COLS = V7X_MXU_DIM
PROJ_ROWS = 512
ATTN_Q_ROWS = 256
ATTN_KEY_SPLIT = 2
ATTN_SCORE_BOUND = 60.0
HGRN_ROWS = 256
HGRN_INTRA_ROWS = V7X_MXU_DIM // 2
MERGE_ROWS = 512


def _compiler_params(semantics):
    return pltpu.CompilerParams(dimension_semantics=semantics, vmem_limit_bytes=V7X_VMEM_LIMIT_BYTES)


def _resident(shape):
    zeros = (0,) * len(shape)
    return pl.BlockSpec(shape, lambda *_: zeros, pipeline_mode=pl.Buffered(1))


def _rmsnorm(x, g):
    return x * lax.rsqrt(jnp.mean(x * x, axis=-1, keepdims=True) + NORM_EPS) * g


def _dot(a, b):
    return jnp.dot(a, b, preferred_element_type=F32)


def _dot_nt(a, b):
    return lax.dot_general(a, b, (((1,), (1,)), ((), ())), preferred_element_type=F32)


def _dot_tn(a, b):
    return lax.dot_general(a, b, (((0,), (0,)), ((), ())), preferred_element_type=F32)


def _split2(x):
    hi = x.astype(BF16)
    lo = (x - hi.astype(F32)).astype(BF16)
    return hi, lo


def _split3(x):
    hi = x.astype(BF16)
    r = x - hi.astype(F32)
    mid = r.astype(BF16)
    lo = (r - mid.astype(F32)).astype(BF16)
    return hi, mid, lo


def _group_mean(x, gmat):
    hi, lo = _split2(x)
    return _dot(hi, gmat) + _dot(lo, gmat)


def _ffn_kernel(x_ref, g_ref, wg_ref, wu_ref, wd_ref, fg_ref, o_ref, *, final_norm):
    x = x_ref[...]
    xn = _rmsnorm(x, g_ref[...]).astype(BF16)
    d_ff = wg_ref.shape[1]
    acc = None
    for c in range(d_ff // FFN_COLS):
        sl = slice(c * FFN_COLS, (c + 1) * FFN_COLS)
        gate = _dot(xn, wg_ref[:, sl])
        up = _dot(xn, wu_ref[:, sl])
        act = (jax.nn.silu(gate) * up).astype(BF16)
        part = _dot(act, wd_ref[sl, :])
        acc = part if acc is None else acc + part
    y = x + 0.5 * acc
    if final_norm:
        y = _rmsnorm(y, fg_ref[...])
    o_ref[...] = y


def _ffn(x, norm_g, w_gate, w_up, w_down, final_g, final_norm):
    t, d = x.shape
    d_ff = w_gate.shape[1]
    assert t % FFN_ROWS == 0 and d_ff % FFN_COLS == 0
    row = pl.BlockSpec((FFN_ROWS, d), lambda i: (i, 0))
    return pl.pallas_call(
        functools.partial(_ffn_kernel, final_norm=final_norm),
        grid=(t // FFN_ROWS,),
        in_specs=[row, _resident((1, d)), _resident((d, d_ff)), _resident((d, d_ff)),
                  _resident((d_ff, d)), _resident((1, d))],
        out_specs=row,
        out_shape=jax.ShapeDtypeStruct((t, d), F32),
        compiler_params=_compiler_params(("parallel",)),
        name="ffn_final" if final_norm else "ffn",
    )(x, norm_g, w_gate, w_up, w_down, final_g)


def _rope(xn, cos_t, sin_t, width):
    half = HEAD_DIM // 2
    lane = lax.broadcasted_iota(jnp.int32, xn.shape, 1)
    first_half = (lane % HEAD_DIM) < half
    partner = jnp.where(first_half, pltpu.roll(xn, width - half, 1), pltpu.roll(xn, half, 1))
    return xn * cos_t + partner * sin_t


def _proj_kernel(x_ref, g_ref, w_ref, qg_ref, kg_ref, gq_ref, gk_ref, cos_ref, sin_ref,
                 q_ref, k_ref, v_ref, hq_ref, ff_ref, fb_ref, hi_ref, hg_ref):
    h = _rmsnorm(x_ref[0], g_ref[...]).astype(BF16)

    def cols(start, width):
        return _dot(h, w_ref[:, start:start + width])

    cos2 = cos_ref[...]
    sin2 = sin_ref[...]
    off = 0
    q = cols(off, ATTN_WIDTH)
    off += ATTN_WIDTH
    reps = ATTN_WIDTH // V7X_LANES
    qn = q * lax.rsqrt(_group_mean(q * q, gq_ref[...]) + NORM_EPS) * qg_ref[...]
    qr = _rope(qn, jnp.concatenate([cos2] * reps, axis=1), jnp.concatenate([sin2] * reps, axis=1),
               ATTN_WIDTH) * (HEAD_DIM ** -0.5)
    for hd in range(ATTN_HEADS):
        q_ref[0, hd] = qr[:, hd * HEAD_DIM:(hd + 1) * HEAD_DIM].astype(BF16)
    k = cols(off, KV_WIDTH)
    off += KV_WIDTH
    kn = k * lax.rsqrt(_group_mean(k * k, gk_ref[...]) + NORM_EPS) * kg_ref[...]
    kr = _rope(kn, cos2, sin2, KV_WIDTH)
    v = cols(off, KV_WIDTH)
    off += KV_WIDTH
    lane = lax.broadcasted_iota(jnp.int32, v.shape, 1)
    for hd in range(ATTN_KV_HEADS):
        k_ref[0, hd] = kr[:, hd * HEAD_DIM:(hd + 1) * HEAD_DIM].astype(BF16)
        v_hd = v if hd == 0 else pltpu.roll(v, KV_WIDTH - hd * HEAD_DIM, 1)
        v_ref[0, hd] = jnp.where(lane < HEAD_DIM, v_hd, 1.0).T.astype(BF16)
    hq_ref[0] = jax.nn.silu(cols(off, HGRN_WIDTH))
    off += HGRN_WIDTH
    ff_ref[0] = cols(off, HGRN_WIDTH)
    off += HGRN_WIDTH
    fb_ref[0] = cols(off, HGRN_WIDTH)
    off += HGRN_WIDTH
    hi_ref[0] = cols(off, HGRN_WIDTH).astype(BF16)
    off += HGRN_WIDTH
    hg_ref[0] = jax.nn.silu(cols(off, HGRN_WIDTH))


def _group_matrix(width, group):
    idx = np.arange(width) // group
    return jnp.asarray((idx[:, None] == idx[None, :]).astype(np.float32) / group, dtype=BF16)


def _rope_tables(seq_len):
    rows = seq_len // GRID_W
    row = jnp.repeat(jnp.arange(rows, dtype=F32), GRID_W)
    col = jnp.tile(jnp.arange(GRID_W, dtype=F32), rows)
    n_freq = HEAD_DIM // 4
    inv = ROPE_THETA ** (-jnp.arange(n_freq, dtype=F32) / n_freq)
    ang = jnp.concatenate([row[:, None] * inv, col[:, None] * inv], axis=-1)
    cos, sin = jnp.cos(ang), jnp.sin(ang)
    cos2 = jnp.concatenate([cos, cos] * 2, axis=-1)
    sin2 = jnp.concatenate([-sin, sin] * 2, axis=-1)
    return cos2, sin2


def _in_proj(x1, norm_g, w_mix, q_g, k_g):
    b, s, d = x1.shape
    n_in = w_mix.shape[1]
    assert s % PROJ_ROWS == 0 and n_in == ATTN_WIDTH + 2 * KV_WIDTH + 5 * HGRN_WIDTH
    cos2, sin2 = _rope_tables(s)
    tm = PROJ_ROWS
    tok = lambda width: pl.BlockSpec((1, tm, width), lambda bi, i: (bi, i, 0))
    heads = lambda n, width=HEAD_DIM: pl.BlockSpec((1, n, tm, width), lambda bi, i: (bi, 0, i, 0))
    pos = pl.BlockSpec((tm, V7X_LANES), lambda bi, i: (i, 0))
    f32_tok = lambda width: jax.ShapeDtypeStruct((b, s, width), F32)
    out_shape = (
        jax.ShapeDtypeStruct((b, ATTN_HEADS, s, HEAD_DIM), BF16),
        jax.ShapeDtypeStruct((b, ATTN_KV_HEADS, s, HEAD_DIM), BF16),
        jax.ShapeDtypeStruct((b, ATTN_KV_HEADS, V7X_LANES, s), BF16),
        f32_tok(HGRN_WIDTH), f32_tok(HGRN_WIDTH), f32_tok(HGRN_WIDTH),
        jax.ShapeDtypeStruct((b, s, HGRN_WIDTH), BF16),
        f32_tok(HGRN_WIDTH),
    )
    vt_spec = pl.BlockSpec((1, ATTN_KV_HEADS, V7X_LANES, tm), lambda bi, i: (bi, 0, 0, i))
    out_specs = (heads(ATTN_HEADS), heads(ATTN_KV_HEADS), vt_spec,
                 tok(HGRN_WIDTH), tok(HGRN_WIDTH), tok(HGRN_WIDTH), tok(HGRN_WIDTH), tok(HGRN_WIDTH))
    return pl.pallas_call(
        _proj_kernel,
        grid=(b, s // tm),
        in_specs=[tok(d), _resident((1, d)), _resident((d, n_in)),
                  _resident((1, ATTN_WIDTH)), _resident((1, KV_WIDTH)),
                  _resident((ATTN_WIDTH, ATTN_WIDTH)), _resident((KV_WIDTH, KV_WIDTH)), pos, pos],
        out_specs=out_specs,
        out_shape=out_shape,
        compiler_params=_compiler_params(("parallel", "parallel")),
        name="in_proj",
    )(x1, norm_g, w_mix, jnp.tile(q_g, (1, ATTN_HEADS)), jnp.tile(k_g, (1, ATTN_KV_HEADS)),
      _group_matrix(ATTN_WIDTH, HEAD_DIM), _group_matrix(KV_WIDTH, HEAD_DIM), cos2, sin2)


def _attn_kernel(bounded_ref, q_ref, k_ref, vt_ref, o_ref):
    tq = q_ref.shape[2]
    s_len = k_ref.shape[2]
    q = q_ref[0].reshape(ATTN_GROUP * tq, HEAD_DIM)

    def finish(accs):
        lane = lax.broadcasted_iota(jnp.int32, (tq, V7X_LANES), 1)
        pairs = []
        for r in range(0, ATTN_GROUP, 2):
            lo, hi = accs[r].T, accs[r + 1].T
            lo_sw, hi_sw = pltpu.roll(lo, HEAD_DIM, 1), pltpu.roll(hi, HEAD_DIM, 1)
            pairs.append(jnp.where(lane < HEAD_DIM, lo / lo_sw, hi_sw / hi))
        o_ref[0] = jnp.concatenate(pairs, axis=1).astype(BF16)

    @pl.when(bounded_ref[0] == 1)
    def _():
        accs = [None] * ATTN_GROUP
        kc = s_len // ATTN_KEY_SPLIT
        for c in range(ATTN_KEY_SPLIT):
            ks = slice(c * kc, (c + 1) * kc)
            p_t = jnp.exp(_dot_nt(k_ref[0, 0, ks, :], q)).astype(BF16)
            vt_c = vt_ref[0, 0, :, ks]
            for r in range(ATTN_GROUP):
                part = _dot(vt_c, p_t[:, r * tq:(r + 1) * tq])
                accs[r] = part if accs[r] is None else accs[r] + part
        finish(accs)

    @pl.when(bounded_ref[0] != 1)
    def _():
        s_t = _dot_nt(k_ref[0, 0], q)
        p_t = jnp.exp(s_t - jnp.max(s_t, axis=0, keepdims=True)).astype(BF16)
        finish([_dot(vt_ref[0, 0], p_t[:, r * tq:(r + 1) * tq]) for r in range(ATTN_GROUP)])


def _attention(bounded, q, k, v):
    b, _, s, _ = q.shape
    tq = ATTN_Q_ROWS
    assert s % tq == 0 and s % (ATTN_KEY_SPLIT * V7X_LANES) == 0
    k_spec = pl.BlockSpec((1, 1, s, HEAD_DIM), lambda bi, g, i, *_: (bi, g, 0, 0))
    vt_spec = pl.BlockSpec((1, 1, V7X_LANES, s), lambda bi, g, i, *_: (bi, g, 0, 0))
    grid_spec = pltpu.PrefetchScalarGridSpec(
        num_scalar_prefetch=1,
        grid=(b, ATTN_KV_HEADS, s // tq),
        in_specs=[pl.BlockSpec((1, ATTN_GROUP, tq, HEAD_DIM), lambda bi, g, i, *_: (bi, g, i, 0)),
                  k_spec, vt_spec],
        out_specs=pl.BlockSpec((1, tq, ATTN_GROUP * HEAD_DIM), lambda bi, g, i, *_: (bi, i, g)),
    )
    return pl.pallas_call(
        _attn_kernel,
        grid_spec=grid_spec,
        out_shape=jax.ShapeDtypeStruct((b, s, ATTN_WIDTH), BF16),
        compiler_params=_compiler_params(("parallel", "parallel", "parallel")),
        name="attention",
    )(bounded, q, k, v)


def _scores_bounded(q_g, k_g):
    bound = (HEAD_DIM ** 0.5) * jnp.max(jnp.abs(q_g)) * jnp.max(jnp.abs(k_g))
    return (bound <= ATTN_SCORE_BOUND).astype(jnp.int32).reshape(1)


def _hgrn_prepare(qt, f_logit, lb, reverse):
    rows = qt.shape[0]
    c = HGRN_CHUNK
    n_chunks = rows // c
    f = lb + (1.0 - lb) * jax.nn.sigmoid(f_logit)
    kk = 1.0 - f
    log_f = jnp.log(f)
    r_i = lax.broadcasted_iota(jnp.int32, (rows, rows), 0)
    c_i = lax.broadcasted_iota(jnp.int32, (rows, rows), 1)
    same_chunk = (r_i // c) == (c_i // c)
    causal = same_chunk & ((c_i >= r_i) if reverse else (c_i <= r_i))
    tri = jnp.where(causal, 1.0, 0.0).astype(BF16)
    hi, mid, lo = _split3(log_f)
    bcum = _dot(tri, hi) + _dot(tri, mid) + _dot(tri, lo)
    b3 = bcum.reshape(n_chunks, c, bcum.shape[-1])
    mid_row = c // 2 if reverse else c // 2 - 1
    last_row = 0 if reverse else c - 1
    b_mid = b3[:, mid_row:mid_row + 1, :]
    b_last = b3[:, last_row:last_row + 1, :]
    q3 = qt.reshape(b3.shape)
    k3 = kk.reshape(b3.shape)
    d_mid = b3 - b_mid
    q_mid = q3 * jnp.exp(d_mid)
    k_mid = k3 * jnp.exp(-d_mid)
    k_end = k_mid * jnp.exp(b_last - b_mid)
    q_in = q_mid * jnp.exp(b_mid)
    flat = lambda a: a.reshape(bcum.shape).astype(BF16)
    sub = HGRN_INTRA_ROWS
    causal_sub = jnp.concatenate([causal[:sub, :sub]] * 2, axis=1)
    return dict(causal_sub=causal_sub, q_mid=flat(q_mid), k_mid=flat(k_mid), k_end=flat(k_end), q_in=flat(q_in),
                decay=jnp.exp(b_last))


def _pair_slice(p):
    return slice(p * HGRN_PAIR, (p + 1) * HGRN_PAIR)


def _hgrn_intra(ops, v, p):
    rows = v.shape[0]
    sub = HGRN_INTRA_ROWS
    lower_head = lax.broadcasted_iota(jnp.int32, (sub, HGRN_PAIR), 1) < HGRN_DIM
    zero_bf = jnp.zeros((), BF16)
    stack = lambda x: jnp.concatenate([jnp.where(lower_head, x, zero_bf), jnp.where(lower_head, zero_bf, x)], axis=0)
    causal = ops["causal_sub"]
    sl = _pair_slice(p)
    outs = []
    for h in range(rows // sub):
        rs = slice(h * sub, (h + 1) * sub)
        scores = _dot_nt(ops["q_mid"][rs, sl], stack(ops["k_mid"][rs, sl]))
        scores = jnp.where(causal, scores, 0.0).astype(BF16)
        outs.append(_dot(scores, stack(v[rs, sl])))
    return jnp.concatenate(outs, axis=0)


def _hgrn_contribs(ops, v):
    c = HGRN_CHUNK
    n_chunks = v.shape[0] // c
    return [[_dot_tn(v[n * c:(n + 1) * c, _pair_slice(p)], ops["k_end"][n * c:(n + 1) * c, _pair_slice(p)])
             for n in range(n_chunks)] for p in range(HGRN_PAIRS)]


def _hgrn_scan(ops, contribs, state_ref, reverse):
    n_chunks = len(contribs[0])
    sr = lax.broadcasted_iota(jnp.int32, (HGRN_PAIR, HGRN_PAIR), 0)
    sc = lax.broadcasted_iota(jnp.int32, (HGRN_PAIR, HGRN_PAIR), 1)
    same_head = (sr // HGRN_DIM) == (sc // HGRN_DIM)
    order = range(n_chunks - 1, -1, -1) if reverse else range(n_chunks)
    states = [[None] * n_chunks for _ in range(HGRN_PAIRS)]
    for p in range(HGRN_PAIRS):
        st = state_ref[p]
        for n in order:
            states[p][n] = st.astype(BF16)
            st = ops["decay"][n, :, _pair_slice(p)] * st + jnp.where(same_head, contribs[p][n], 0.0)
        state_ref[p] = st
    return states


def _hgrn_inter(ops, states, p):
    c = HGRN_CHUNK
    parts = [_dot_nt(ops["q_in"][n * c:(n + 1) * c, _pair_slice(p)], st) for n, st in enumerate(states[p])]
    return jnp.concatenate(parts, axis=0)


def _hgrn_kernel(lbl_ref, qf_ref, ff_ref, vf_ref, qb_ref, fb_ref, vb_ref, of_ref, ob_ref,
                 st_f_ref, st_b_ref):
    @pl.when(pl.program_id(1) == 0)
    def _():
        st_f_ref[...] = jnp.zeros_like(st_f_ref)
        st_b_ref[...] = jnp.zeros_like(st_b_ref)

    logits = lbl_ref[...]
    e = jnp.exp(logits - jnp.max(logits, axis=1, keepdims=True))
    lb = e[:, 0, :] / jnp.sum(e, axis=1)
    v_f, v_b = vf_ref[0], vb_ref[0]
    ops_f = _hgrn_prepare(qf_ref[0], ff_ref[0], lb[0:1], False)
    ops_b = _hgrn_prepare(qb_ref[0], fb_ref[0], lb[1:2], True)
    con_f = _hgrn_contribs(ops_f, v_f)
    con_b = _hgrn_contribs(ops_b, v_b)
    states_f = _hgrn_scan(ops_f, con_f, st_f_ref, False)
    states_b = _hgrn_scan(ops_b, con_b, st_b_ref, True)
    pairs = range(HGRN_PAIRS)
    inter_f = [_hgrn_inter(ops_f, states_f, p) for p in pairs]
    inter_b = [_hgrn_inter(ops_b, states_b, p) for p in pairs]
    intra_f = [_hgrn_intra(ops_f, v_f, p) for p in pairs]
    intra_b = [_hgrn_intra(ops_b, v_b, p) for p in pairs]
    of_ref[0] = jnp.concatenate([a + b for a, b in zip(intra_f, inter_f)], axis=1)
    ob_ref[0] = jnp.concatenate([a + b for a, b in zip(intra_b, inter_b)], axis=1)


def _hgrn(lb_logits, hq, hff, hfb, hi):
    b, s, w = hq.shape
    r = HGRN_ROWS
    assert s % r == 0 and r % HGRN_INTRA_ROWS == 0 and HGRN_INTRA_ROWS % HGRN_CHUNK == 0
    nb = s // r
    fwd = pl.BlockSpec((1, r, w), lambda bi, j: (bi, j, 0))
    bwd = pl.BlockSpec((1, r, w), lambda bi, j: (bi, nb - 1 - j, 0))
    state = pltpu.VMEM((HGRN_PAIRS, HGRN_PAIR, HGRN_PAIR), F32)
    return pl.pallas_call(
        _hgrn_kernel,
        grid=(b, nb),
        in_specs=[_resident(lb_logits.shape), fwd, fwd, fwd, bwd, bwd, bwd],
        out_specs=(fwd, bwd),
        out_shape=(jax.ShapeDtypeStruct((b, s, w), F32), jax.ShapeDtypeStruct((b, s, w), F32)),
        scratch_shapes=[state, state],
        compiler_params=_compiler_params(("parallel", "arbitrary")),
        name="hgrn",
    )(lb_logits, hq, hff, hi, hq, hfb, hi)


def _merge_kernel(x_ref, ya_ref, of_ref, ob_ref, hg_ref, mg_ref, wg_ref, ng_ref, gm_ref,
                  wa_ref, wb_ref, wo_ref, o_ref):
    x = x_ref[...]
    d = x.shape[-1]
    h = _rmsnorm(x, mg_ref[...]).astype(BF16)
    gate_a = jax.nn.sigmoid(_dot(h, wg_ref[:, :d]))
    gate_b = jax.nn.sigmoid(_dot(h, wg_ref[:, d:]))
    o = of_ref[...] + ob_ref[...]
    on = o * lax.rsqrt(_group_mean(o * o, gm_ref[...]) + NORM_EPS) * ng_ref[...]
    yb = (on * hg_ref[...]).astype(BF16)
    merged = gate_a * _dot(ya_ref[...], wa_ref[...]) + gate_b * _dot(yb, wb_ref[...])
    o_ref[...] = x + _dot(merged.astype(BF16), wo_ref[...])


def _merge(x1, y_attn, o_f, o_b, hg, mix_g, w_gates, norm_g, w_a, w_b, w_out):
    t, d = x1.shape
    tm = MERGE_ROWS
    assert t % tm == 0
    row = lambda width: pl.BlockSpec((tm, width), lambda i: (i, 0))
    return pl.pallas_call(
        _merge_kernel,
        grid=(t // tm,),
        in_specs=[row(d), row(ATTN_WIDTH), row(HGRN_WIDTH), row(HGRN_WIDTH), row(HGRN_WIDTH),
                  _resident((1, d)), _resident((d, 2 * d)),
                  _resident((1, HGRN_WIDTH)), _resident((HGRN_WIDTH, HGRN_WIDTH)),
                  _resident((ATTN_WIDTH, d)), _resident((HGRN_WIDTH, d)), _resident((d, d))],
        out_specs=row(d),
        out_shape=jax.ShapeDtypeStruct((t, d), F32),
        compiler_params=_compiler_params(("parallel",)),
        name="merge",
    )(x1, y_attn, o_f, o_b, hg, mix_g, w_gates, norm_g, _group_matrix(HGRN_WIDTH, HGRN_DIM), w_a, w_b, w_out)


def kernel(x, ffn1_norm_g, ffn1_w_gate, ffn1_w_up, ffn1_w_down, mix_norm_g, w_in, q_norm_g, k_norm_g,
           hgrn_lb_logits, hgrn_out_norm_g, w_branch_attn, w_branch_hgrn, w_out, ffn2_norm_g,
           ffn2_w_gate, ffn2_w_up, ffn2_w_down, final_norm_g):
    b, s, d = x.shape
    t = b * s
    depth = w_in.shape[0]
    assert depth == 1 and hgrn_lb_logits.shape[1] == depth + 1
    bf = lambda w: w.astype(BF16)
    final_g = final_norm_g.reshape(1, d)
    n_mix = w_in.shape[2] - 2 * d

    x1 = _ffn(x.reshape(t, d), ffn1_norm_g, bf(ffn1_w_gate[0]), bf(ffn1_w_up[0]), bf(ffn1_w_down[0]),
              final_g, False)
    q, k, v, hq, hff, hfb, hi, hg = _in_proj(
        x1.reshape(b, s, d), mix_norm_g, bf(w_in[0, :, :n_mix]), q_norm_g, k_norm_g)
    y_attn = _attention(_scores_bounded(q_norm_g, k_norm_g), q, k, v)
    o_f, o_b = _hgrn(hgrn_lb_logits, hq, hff, hfb, hi)
    flat = lambda a: a.reshape(t, a.shape[-1])
    x2 = _merge(x1, flat(y_attn), flat(o_f), flat(o_b), flat(hg), mix_norm_g, bf(w_in[0, :, n_mix:]),
                hgrn_out_norm_g, bf(w_branch_attn[0]), bf(w_branch_hgrn[0]), bf(w_out[0]))
    out = _ffn(x2, ffn2_norm_g, bf(ffn2_w_gate[0]), bf(ffn2_w_up[0]), bf(ffn2_w_down[0]), final_g, True)
    return out.reshape(b, s, d)
```

```python
import functools

import jax
import jax.numpy as jnp
import numpy as np
from jax import lax
from jax.experimental import pallas as pl
from jax.experimental.pallas import tpu as pltpu

F32 = jnp.float32
BF16 = jnp.bfloat16

NORM_EPS = 1e-6
ROPE_THETA = 10000.0
GRID_W = 64
HEAD_DIM = 64
ATTN_HEADS = 8
ATTN_KV_HEADS = 2
ATTN_GROUP = ATTN_HEADS // ATTN_KV_HEADS
HGRN_HEADS = 8
HGRN_DIM = 64
HGRN_CHUNK = 32
ATTN_WIDTH = ATTN_HEADS * HEAD_DIM
KV_WIDTH = ATTN_KV_HEADS * HEAD_DIM
HGRN_WIDTH = HGRN_HEADS * HGRN_DIM
HGRN_PAIR = 2 * HGRN_DIM
HGRN_PAIRS = HGRN_HEADS // 2

V7X_LANES = 128
V7X_MXU_DIM = 256
V7X_VMEM_LIMIT_BYTES = 56 * 1024 * 1024

FFN_ROWS = 512
FFN_COLS = V7X_MXU_DIM
PROJ_ROWS = 1024
ATTN_Q_ROWS = 512
ATTN_KEY_SPLIT = 4
ATTN_SCORE_BOUND = 60.0
HGRN_ROWS = 256
HGRN_INTRA_ROWS = V7X_MXU_DIM // 2
MERGE_ROWS = 1024


def _compiler_params(semantics):
    return pltpu.CompilerParams(dimension_semantics=semantics, vmem_limit_bytes=V7X_VMEM_LIMIT_BYTES)


def _resident(shape):
    zeros = (0,) * len(shape)
    return pl.BlockSpec(shape, lambda *_: zeros, pipeline_mode=pl.Buffered(1))


def _rmsnorm(x, g):
    return x * lax.rsqrt(jnp.mean(x * x, axis=-1, keepdims=True) + NORM_EPS) * g


def _dot(a, b):
    return jnp.dot(a, b, preferred_element_type=F32)


def _dot_nt(a, b):
    return lax.dot_general(a, b, (((1,), (1,)), ((), ())), preferred_element_type=F32)


def _dot_tn(a, b):
    return lax.dot_general(a, b, (((0,), (0,)), ((), ())), preferred_element_type=F32)


def _split2(x):
    hi = x.astype(BF16)
    lo = (x - hi.astype(F32)).astype(BF16)
    return hi, lo


def _split3(x):
    hi = x.astype(BF16)
    r = x - hi.astype(F32)
    mid = r.astype(BF16)
    lo = (r - mid.astype(F32)).astype(BF16)
    return hi, mid, lo


def _group_mean(x, gmat):
    hi, lo = _split2(x)
    return _dot(hi, gmat) + _dot(lo, gmat)


def _ffn_kernel(x_ref, g_ref, wg_ref, wu_ref, wd_ref, fg_ref, o_ref, *, final_norm):
    x = x_ref[...]
    xn = _rmsnorm(x, g_ref[...]).astype(BF16)
    d_ff = wg_ref.shape[1]
    acc = None
    for c in range(d_ff // FFN_COLS):
        sl = slice(c * FFN_COLS, (c + 1) * FFN_COLS)
        gate = _dot(xn, wg_ref[:, sl])
        up = _dot(xn, wu_ref[:, sl])
        act = (jax.nn.silu(gate) * up).astype(BF16)
        part = _dot(act, wd_ref[sl, :])
        acc = part if acc is None else acc + part
    y = x + 0.5 * acc
    if final_norm:
        y = _rmsnorm(y, fg_ref[...])
    o_ref[...] = y


def _ffn(x, norm_g, w_gate, w_up, w_down, final_g, final_norm):
    t, d = x.shape
    d_ff = w_gate.shape[1]
    assert t % FFN_ROWS == 0 and d_ff % FFN_COLS == 0
    row = pl.BlockSpec((FFN_ROWS, d), lambda i: (i, 0))
    return pl.pallas_call(
        functools.partial(_ffn_kernel, final_norm=final_norm),
        grid=(t // FFN_ROWS,),
        in_specs=[row, _resident((1, d)), _resident((d, d_ff)), _resident((d, d_ff)),
                  _resident((d_ff, d)), _resident((1, d))],
        out_specs=row,
        out_shape=jax.ShapeDtypeStruct((t, d), F32),
        compiler_params=_compiler_params(("parallel",)),
        name="ffn_final" if final_norm else "ffn",
    )(x, norm_g, w_gate, w_up, w_down, final_g)


def _rope(xn, cos_t, sin_t, width):
    half = HEAD_DIM // 2
    lane = lax.broadcasted_iota(jnp.int32, xn.shape, 1)
    first_half = (lane % HEAD_DIM) < half
    partner = jnp.where(first_half, pltpu.roll(xn, width - half, 1), pltpu.roll(xn, half, 1))
    return xn * cos_t + partner * sin_t


def _proj_kernel(x_ref, g_ref, w_ref, qg_ref, kg_ref, gq_ref, gk_ref, cos_ref, sin_ref,
                 q_ref, k_ref, v_ref, hq_ref, ff_ref, fb_ref, hi_ref, hg_ref):
    h = _rmsnorm(x_ref[0], g_ref[...]).astype(BF16)

    def cols(start, width):
        return _dot(h, w_ref[:, start:start + width])

    cos2 = cos_ref[...]
    sin2 = sin_ref[...]
    off = 0
    q = cols(off, ATTN_WIDTH)
    off += ATTN_WIDTH
    reps = ATTN_WIDTH // V7X_LANES
    qn = q * lax.rsqrt(_group_mean(q * q, gq_ref[...]) + NORM_EPS) * qg_ref[...]
    qr = _rope(qn, jnp.concatenate([cos2] * reps, axis=1), jnp.concatenate([sin2] * reps, axis=1),
               ATTN_WIDTH) * (HEAD_DIM ** -0.5)
    for hd in range(ATTN_HEADS):
        q_ref[0, hd] = qr[:, hd * HEAD_DIM:(hd + 1) * HEAD_DIM].astype(BF16)
    k = cols(off, KV_WIDTH)
    off += KV_WIDTH
    kn = k * lax.rsqrt(_group_mean(k * k, gk_ref[...]) + NORM_EPS) * kg_ref[...]
    kr = _rope(kn, cos2, sin2, KV_WIDTH)
    v = cols(off, KV_WIDTH)
    off += KV_WIDTH
    lane = lax.broadcasted_iota(jnp.int32, v.shape, 1)
    for hd in range(ATTN_KV_HEADS):
        k_ref[0, hd] = kr[:, hd * HEAD_DIM:(hd + 1) * HEAD_DIM].astype(BF16)
        v_hd = v if hd == 0 else pltpu.roll(v, KV_WIDTH - hd * HEAD_DIM, 1)
        v_ref[0, hd] = jnp.where(lane < HEAD_DIM, v_hd, 1.0).T.astype(BF16)
    hq_ref[0] = jax.nn.silu(cols(off, HGRN_WIDTH))
    off += HGRN_WIDTH
    ff_ref[0] = cols(off, HGRN_WIDTH)
    off += HGRN_WIDTH
    fb_ref[0] = cols(off, HGRN_WIDTH)
    off += HGRN_WIDTH
    hi_ref[0] = cols(off, HGRN_WIDTH).astype(BF16)
    off += HGRN_WIDTH
    hg_ref[0] = jax.nn.silu(cols(off, HGRN_WIDTH))


def _group_matrix(width, group):
    idx = np.arange(width) // group
    return jnp.asarray((idx[:, None] == idx[None, :]).astype(np.float32) / group, dtype=BF16)


def _rope_tables(seq_len):
    rows = seq_len // GRID_W
    row = jnp.repeat(jnp.arange(rows, dtype=F32), GRID_W)
    col = jnp.tile(jnp.arange(GRID_W, dtype=F32), rows)
    n_freq = HEAD_DIM // 4
    inv = ROPE_THETA ** (-jnp.arange(n_freq, dtype=F32) / n_freq)
    ang = jnp.concatenate([row[:, None] * inv, col[:, None] * inv], axis=-1)
    cos, sin = jnp.cos(ang), jnp.sin(ang)
    cos2 = jnp.concatenate([cos, cos] * 2, axis=-1)
    sin2 = jnp.concatenate([-sin, sin] * 2, axis=-1)
    return cos2, sin2


def _in_proj(x1, norm_g, w_mix, q_g, k_g):
    b, s, d = x1.shape
    n_in = w_mix.shape[1]
    assert s % PROJ_ROWS == 0 and n_in == ATTN_WIDTH + 2 * KV_WIDTH + 5 * HGRN_WIDTH
    cos2, sin2 = _rope_tables(s)
    tm = PROJ_ROWS
    tok = lambda width: pl.BlockSpec((1, tm, width), lambda bi, i: (bi, i, 0))
    heads = lambda n, width=HEAD_DIM: pl.BlockSpec((1, n, tm, width), lambda bi, i: (bi, 0, i, 0))
    pos = pl.BlockSpec((tm, V7X_LANES), lambda bi, i: (i, 0))
    f32_tok = lambda width: jax.ShapeDtypeStruct((b, s, width), F32)
    out_shape = (
        jax.ShapeDtypeStruct((b, ATTN_HEADS, s, HEAD_DIM), BF16),
        jax.ShapeDtypeStruct((b, ATTN_KV_HEADS, s, HEAD_DIM), BF16),
        jax.ShapeDtypeStruct((b, ATTN_KV_HEADS, V7X_LANES, s), BF16),
        f32_tok(HGRN_WIDTH), f32_tok(HGRN_WIDTH), f32_tok(HGRN_WIDTH),
        jax.ShapeDtypeStruct((b, s, HGRN_WIDTH), BF16),
        f32_tok(HGRN_WIDTH),
    )
    vt_spec = pl.BlockSpec((1, ATTN_KV_HEADS, V7X_LANES, tm), lambda bi, i: (bi, 0, 0, i))
    out_specs = (heads(ATTN_HEADS), heads(ATTN_KV_HEADS), vt_spec,
                 tok(HGRN_WIDTH), tok(HGRN_WIDTH), tok(HGRN_WIDTH), tok(HGRN_WIDTH), tok(HGRN_WIDTH))
    return pl.pallas_call(
        _proj_kernel,
        grid=(b, s // tm),
        in_specs=[tok(d), _resident((1, d)), _resident((d, n_in)),
                  _resident((1, ATTN_WIDTH)), _resident((1, KV_WIDTH)),
                  _resident((ATTN_WIDTH, ATTN_WIDTH)), _resident((KV_WIDTH, KV_WIDTH)), pos, pos],
        out_specs=out_specs,
        out_shape=out_shape,
        compiler_params=_compiler_params(("parallel", "parallel")),
        name="in_proj",
    )(x1, norm_g, w_mix, jnp.tile(q_g, (1, ATTN_HEADS)), jnp.tile(k_g, (1, ATTN_KV_HEADS)),
      _group_matrix(ATTN_WIDTH, HEAD_DIM), _group_matrix(KV_WIDTH, HEAD_DIM), cos2, sin2)


def _attn_kernel(bounded_ref, q_ref, k_ref, vt_ref, o_ref):
    tq = q_ref.shape[2]
    s_len = k_ref.shape[2]
    q = q_ref[0].reshape(ATTN_GROUP * tq, HEAD_DIM)

    def finish(accs):
        lane = lax.broadcasted_iota(jnp.int32, (tq, V7X_LANES), 1)
        pairs = []
        for r in range(0, ATTN_GROUP, 2):
            lo, hi = accs[r].T, accs[r + 1].T
            lo_sw, hi_sw = pltpu.roll(lo, HEAD_DIM, 1), pltpu.roll(hi, HEAD_DIM, 1)
            pairs.append(jnp.where(lane < HEAD_DIM, lo / lo_sw, hi_sw / hi))
        o_ref[0] = jnp.concatenate(pairs, axis=1).astype(BF16)

    @pl.when(bounded_ref[0] == 1)
    def _():
        accs = [None] * ATTN_GROUP
        kc = s_len // ATTN_KEY_SPLIT
        for c in range(ATTN_KEY_SPLIT):
            ks = slice(c * kc, (c + 1) * kc)
            p_t = jnp.exp(_dot_nt(k_ref[0, 0, ks, :], q)).astype(BF16)
            vt_c = vt_ref[0, 0, :, ks]
            for r in range(ATTN_GROUP):
                part = _dot(vt_c, p_t[:, r * tq:(r + 1) * tq])
                accs[r] = part if accs[r] is None else accs[r] + part
        finish(accs)

    @pl.when(bounded_ref[0] != 1)
    def _():
        s_t = _dot_nt(k_ref[0, 0], q)
        p_t = jnp.exp(s_t - jnp.max(s_t, axis=0, keepdims=True)).astype(BF16)
        finish([_dot(vt_ref[0, 0], p_t[:, r * tq:(r + 1) * tq]) for r in range(ATTN_GROUP)])


def _attention(bounded, q, k, v):
    b, _, s, _ = q.shape
    tq = ATTN_Q_ROWS
    assert s % tq == 0 and s % (ATTN_KEY_SPLIT * V7X_LANES) == 0
    k_spec = pl.BlockSpec((1, 1, s, HEAD_DIM), lambda bi, g, i, *_: (bi, g, 0, 0))
    vt_spec = pl.BlockSpec((1, 1, V7X_LANES, s), lambda bi, g, i, *_: (bi, g, 0, 0))
    grid_spec = pltpu.PrefetchScalarGridSpec(
        num_scalar_prefetch=1,
        grid=(b, ATTN_KV_HEADS, s // tq),
        in_specs=[pl.BlockSpec((1, ATTN_GROUP, tq, HEAD_DIM), lambda bi, g, i, *_: (bi, g, i, 0)),
                  k_spec, vt_spec],
        out_specs=pl.BlockSpec((1, tq, ATTN_GROUP * HEAD_DIM), lambda bi, g, i, *_: (bi, i, g)),
    )
    return pl.pallas_call(
        _attn_kernel,
        grid_spec=grid_spec,
        out_shape=jax.ShapeDtypeStruct((b, s, ATTN_WIDTH), BF16),
        compiler_params=_compiler_params(("parallel", "parallel", "parallel")),
        name="attention",
    )(bounded, q, k, v)


def _scores_bounded(q_g, k_g):
    bound = (HEAD_DIM ** 0.5) * jnp.max(jnp.abs(q_g)) * jnp.max(jnp.abs(k_g))
    return (bound <= ATTN_SCORE_BOUND).astype(jnp.int32).reshape(1)


def _hgrn_prepare(qt, f_logit, lb, reverse):
    rows = qt.shape[0]
    c = HGRN_CHUNK
    n_chunks = rows // c
    f = lb + (1.0 - lb) * jax.nn.sigmoid(f_logit)
    kk = 1.0 - f
    log_f = jnp.log(f)
    r_i = lax.broadcasted_iota(jnp.int32, (rows, rows), 0)
    c_i = lax.broadcasted_iota(jnp.int32, (rows, rows), 1)
    same_chunk = (r_i // c) == (c_i // c)
    causal = same_chunk & ((c_i >= r_i) if reverse else (c_i <= r_i))
    tri = jnp.where(causal, 1.0, 0.0).astype(BF16)
    hi, mid, lo = _split3(log_f)
    bcum = _dot(tri, hi) + _dot(tri, mid) + _dot(tri, lo)
    b3 = bcum.reshape(n_chunks, c, bcum.shape[-1])
    mid_row = c // 2 if reverse else c // 2 - 1
    last_row = 0 if reverse else c - 1
    b_mid = b3[:, mid_row:mid_row + 1, :]
    b_last = b3[:, last_row:last_row + 1, :]
    q3 = qt.reshape(b3.shape)
    k3 = kk.reshape(b3.shape)
    d_mid = b3 - b_mid
    q_mid = q3 * jnp.exp(d_mid)
    k_mid = k3 * jnp.exp(-d_mid)
    k_end = k_mid * jnp.exp(b_last - b_mid)
    q_in = q_mid * jnp.exp(b_mid)
    flat = lambda a: a.reshape(bcum.shape).astype(BF16)
    sub = HGRN_INTRA_ROWS
    causal_sub = jnp.concatenate([causal[:sub, :sub]] * 2, axis=1)
    return dict(causal_sub=causal_sub, q_mid=flat(q_mid), k_mid=flat(k_mid), k_end=flat(k_end), q_in=flat(q_in),
                decay=jnp.exp(b_last))


def _pair_slice(p):
    return slice(p * HGRN_PAIR, (p + 1) * HGRN_PAIR)


def _hgrn_intra(ops, v, p):
    rows = v.shape[0]
    sub = HGRN_INTRA_ROWS
    lower_head = lax.broadcasted_iota(jnp.int32, (sub, HGRN_PAIR), 1) < HGRN_DIM
    zero_bf = jnp.zeros((), BF16)
    stack = lambda x: jnp.concatenate([jnp.where(lower_head, x, zero_bf), jnp.where(lower_head, zero_bf, x)], axis=0)
    causal = ops["causal_sub"]
    sl = _pair_slice(p)
    outs = []
    for h in range(rows // sub):
        rs = slice(h * sub, (h + 1) * sub)
        scores = _dot_nt(ops["q_mid"][rs, sl], stack(ops["k_mid"][rs, sl]))
        scores = jnp.where(causal, scores, 0.0).astype(BF16)
        outs.append(_dot(scores, stack(v[rs, sl])))
    return jnp.concatenate(outs, axis=0)


def _hgrn_contribs(ops, v):
    c = HGRN_CHUNK
    n_chunks = v.shape[0] // c
    return [[_dot_tn(v[n * c:(n + 1) * c, _pair_slice(p)], ops["k_end"][n * c:(n + 1) * c, _pair_slice(p)])
             for n in range(n_chunks)] for p in range(HGRN_PAIRS)]


def _hgrn_scan(ops, contribs, state_ref, reverse):
    n_chunks = len(contribs[0])
    sr = lax.broadcasted_iota(jnp.int32, (HGRN_PAIR, HGRN_PAIR), 0)
    sc = lax.broadcasted_iota(jnp.int32, (HGRN_PAIR, HGRN_PAIR), 1)
    same_head = (sr // HGRN_DIM) == (sc // HGRN_DIM)
    order = range(n_chunks - 1, -1, -1) if reverse else range(n_chunks)
    states = [[None] * n_chunks for _ in range(HGRN_PAIRS)]
    for p in range(HGRN_PAIRS):
        st = state_ref[p]
        for n in order:
            states[p][n] = st.astype(BF16)
            st = ops["decay"][n, :, _pair_slice(p)] * st + jnp.where(same_head, contribs[p][n], 0.0)
        state_ref[p] = st
    return states


def _hgrn_inter(ops, states, p):
    c = HGRN_CHUNK
    parts = [_dot_nt(ops["q_in"][n * c:(n + 1) * c, _pair_slice(p)], st) for n, st in enumerate(states[p])]
    return jnp.concatenate(parts, axis=0)


def _hgrn_kernel(lbl_ref, qf_ref, ff_ref, vf_ref, qb_ref, fb_ref, vb_ref, of_ref, ob_ref,
                 st_f_ref, st_b_ref):
    @pl.when(pl.program_id(1) == 0)
    def _():
        st_f_ref[...] = jnp.zeros_like(st_f_ref)
        st_b_ref[...] = jnp.zeros_like(st_b_ref)

    logits = lbl_ref[...]
    e = jnp.exp(logits - jnp.max(logits, axis=1, keepdims=True))
    lb = e[:, 0, :] / jnp.sum(e, axis=1)
    v_f, v_b = vf_ref[0], vb_ref[0]
    ops_f = _hgrn_prepare(qf_ref[0], ff_ref[0], lb[0:1], False)
    ops_b = _hgrn_prepare(qb_ref[0], fb_ref[0], lb[1:2], True)
    con_f = _hgrn_contribs(ops_f, v_f)
    con_b = _hgrn_contribs(ops_b, v_b)
    states_f = _hgrn_scan(ops_f, con_f, st_f_ref, False)
    states_b = _hgrn_scan(ops_b, con_b, st_b_ref, True)
    pairs = range(HGRN_PAIRS)
    inter_f = [_hgrn_inter(ops_f, states_f, p) for p in pairs]
    inter_b = [_hgrn_inter(ops_b, states_b, p) for p in pairs]
    intra_f = [_hgrn_intra(ops_f, v_f, p) for p in pairs]
    intra_b = [_hgrn_intra(ops_b, v_b, p) for p in pairs]
    of_ref[0] = jnp.concatenate([a + b for a, b in zip(intra_f, inter_f)], axis=1)
    ob_ref[0] = jnp.concatenate([a + b for a, b in zip(intra_b, inter_b)], axis=1)


def _hgrn(lb_logits, hq, hff, hfb, hi):
    b, s, w = hq.shape
    r = HGRN_ROWS
    assert s % r == 0 and r % HGRN_INTRA_ROWS == 0 and HGRN_INTRA_ROWS % HGRN_CHUNK == 0
    nb = s // r
    fwd = pl.BlockSpec((1, r, w), lambda bi, j: (bi, j, 0))
    bwd = pl.BlockSpec((1, r, w), lambda bi, j: (bi, nb - 1 - j, 0))
    state = pltpu.VMEM((HGRN_PAIRS, HGRN_PAIR, HGRN_PAIR), F32)
    return pl.pallas_call(
        _hgrn_kernel,
        grid=(b, nb),
        in_specs=[_resident(lb_logits.shape), fwd, fwd, fwd, bwd, bwd, bwd],
        out_specs=(fwd, bwd),
        out_shape=(jax.ShapeDtypeStruct((b, s, w), F32), jax.ShapeDtypeStruct((b, s, w), F32)),
        scratch_shapes=[state, state],
        compiler_params=_compiler_params(("parallel", "arbitrary")),
        name="hgrn",
    )(lb_logits, hq, hff, hi, hq, hfb, hi)


def _merge_kernel(x_ref, ya_ref, of_ref, ob_ref, hg_ref, mg_ref, wg_ref, ng_ref, gm_ref,
                  wa_ref, wb_ref, wo_ref, o_ref):
    x = x_ref[...]
    d = x.shape[-1]
    h = _rmsnorm(x, mg_ref[...]).astype(BF16)
    gate_a = jax.nn.sigmoid(_dot(h, wg_ref[:, :d]))
    gate_b = jax.nn.sigmoid(_dot(h, wg_ref[:, d:]))
    o = of_ref[...] + ob_ref[...]
    on = o * lax.rsqrt(_group_mean(o * o, gm_ref[...]) + NORM_EPS) * ng_ref[...]
    yb = (on * hg_ref[...]).astype(BF16)
    merged = gate_a * _dot(ya_ref[...], wa_ref[...]) + gate_b * _dot(yb, wb_ref[...])
    o_ref[...] = x + _dot(merged.astype(BF16), wo_ref[...])


def _merge(x1, y_attn, o_f, o_b, hg, mix_g, w_gates, norm_g, w_a, w_b, w_out):
    t, d = x1.shape
    tm = MERGE_ROWS
    assert t % tm == 0
    row = lambda width: pl.BlockSpec((tm, width), lambda i: (i, 0))
    return pl.pallas_call(
        _merge_kernel,
        grid=(t // tm,),
        in_specs=[row(d), row(ATTN_WIDTH), row(HGRN_WIDTH), row(HGRN_WIDTH), row(HGRN_WIDTH),
                  _resident((1, d)), _resident((d, 2 * d)),
                  _resident((1, HGRN_WIDTH)), _resident((HGRN_WIDTH, HGRN_WIDTH)),
                  _resident((ATTN_WIDTH, d)), _resident((HGRN_WIDTH, d)), _resident((d, d))],
        out_specs=row(d),
        out_shape=jax.ShapeDtypeStruct((t, d), F32),
        compiler_params=_compiler_params(("parallel",)),
        name="merge",
    )(x1, y_attn, o_f, o_b, hg, mix_g, w_gates, norm_g, _group_matrix(HGRN_WIDTH, HGRN_DIM), w_a, w_b, w_out)


def kernel(x, ffn1_norm_g, ffn1_w_gate, ffn1_w_up, ffn1_w_down, mix_norm_g, w_in, q_norm_g, k_norm_g,
           hgrn_lb_logits, hgrn_out_norm_g, w_branch_attn, w_branch_hgrn, w_out, ffn2_norm_g,
           ffn2_w_gate, ffn2_w_up, ffn2_w_down, final_norm_g):
    b, s, d = x.shape
    t = b * s
    depth = w_in.shape[0]
    assert depth == 1 and hgrn_lb_logits.shape[1] == depth + 1
    bf = lambda w: w.astype(BF16)
    final_g = final_norm_g.reshape(1, d)
    n_mix = w_in.shape[2] - 2 * d

    x1 = _ffn(x.reshape(t, d), ffn1_norm_g, bf(ffn1_w_gate[0]), bf(ffn1_w_up[0]), bf(ffn1_w_down[0]),
              final_g, False)
    q, k, v, hq, hff, hfb, hi, hg = _in_proj(
        x1.reshape(b, s, d), mix_norm_g, bf(w_in[0, :, :n_mix]), q_norm_g, k_norm_g)
    y_attn = _attention(_scores_bounded(q_norm_g, k_norm_g), q, k, v)
    o_f, o_b = _hgrn(hgrn_lb_logits, hq, hff, hfb, hi)
    flat = lambda a: a.reshape(t, a.shape[-1])
    x2 = _merge(x1, flat(y_attn), flat(o_f), flat(o_b), flat(hg), mix_norm_g, bf(w_in[0, :, n_mix:]),
                hgrn_out_norm_g, bf(w_branch_attn[0]), bf(w_branch_hgrn[0]), bf(w_out[0]))
    out = _ffn(x2, ffn2_norm_g, bf(ffn2_w_gate[0]), bf(ffn2_w_up[0]), bf(ffn2_w_down[0]), final_g, True)
    return out.reshape(b, s, d)
```

```python
import functools

import jax
import jax.numpy as jnp
import numpy as np
from jax import lax
from jax.experimental import pallas as pl
from jax.experimental.pallas import tpu as pltpu

F32 = jnp.float32
BF16 = jnp.bfloat16

NORM_EPS = 1e-6
ROPE_THETA = 10000.0
GRID_W = 64
HEAD_DIM = 64
ATTN_HEADS = 8
ATTN_KV_HEADS = 2
ATTN_GROUP = ATTN_HEADS // ATTN_KV_HEADS
HGRN_HEADS = 8
HGRN_DIM = 64
HGRN_CHUNK = 32
ATTN_WIDTH = ATTN_HEADS * HEAD_DIM
KV_WIDTH = ATTN_KV_HEADS * HEAD_DIM
HGRN_WIDTH = HGRN_HEADS * HGRN_DIM
HGRN_PAIR = 2 * HGRN_DIM
HGRN_PAIRS = HGRN_HEADS // 2

V7X_LANES = 128
V7X_MXU_DIM = 256
V7X_VMEM_LIMIT_BYTES = 56 * 1024 * 1024

FFN_ROWS = 512
FFN_COLS = V7X_MXU_DIM
FFN_CAST_ROWS = 256
PROJ_ROWS = 1024
ATTN_Q_ROWS = 512
ATTN_KEY_SPLIT = 4
ATTN_SCORE_BOUND = 60.0
HGRN_ROWS = 256
HGRN_INTRA_ROWS = V7X_MXU_DIM // 2
MERGE_ROWS = 1024


def _compiler_params(semantics):
    return pltpu.CompilerParams(dimension_semantics=semantics, vmem_limit_bytes=V7X_VMEM_LIMIT_BYTES)


def _resident(shape):
    zeros = (0,) * len(shape)
    return pl.BlockSpec(shape, lambda *_: zeros, pipeline_mode=pl.Buffered(1))


def _rmsnorm(x, g):
    return x * lax.rsqrt(jnp.mean(x * x, axis=-1, keepdims=True) + NORM_EPS) * g


def _dot(a, b):
    return jnp.dot(a, b, preferred_element_type=F32)


def _dot_nt(a, b):
    return lax.dot_general(a, b, (((1,), (1,)), ((), ())), preferred_element_type=F32)


def _dot_tn(a, b):
    return lax.dot_general(a, b, (((0,), (0,)), ((), ())), preferred_element_type=F32)


def _split2(x):
    hi = x.astype(BF16)
    lo = (x - hi.astype(F32)).astype(BF16)
    return hi, lo


def _split3(x):
    hi = x.astype(BF16)
    r = x - hi.astype(F32)
    mid = r.astype(BF16)
    lo = (r - mid.astype(F32)).astype(BF16)
    return hi, mid, lo


def _group_mean(x, gmat):
    hi, lo = _split2(x)
    return _dot(hi, gmat) + _dot(lo, gmat)


def _cast_rows(src_hbm, dst, stage, sem, chunk):
    n = src_hbm.shape[0] // chunk

    def copy(k):
        return pltpu.make_async_copy(src_hbm.at[pl.ds(k * chunk, chunk), :], stage.at[k % 2], sem.at[k % 2])

    copy(0).start()
    for k in range(n):
        if k + 1 < n:
            copy(k + 1).start()
        copy(k).wait()
        dst[pl.ds(k * chunk, chunk), :] = stage[k % 2].astype(BF16)


def _ffn_kernel(x_ref, g_ref, wg_hbm, wu_hbm, wd_hbm, fg_ref, o_ref,
                wg_ref, wu_ref, wd_ref, stage_in, stage_out, sem_in, sem_out, *, final_norm):
    @pl.when(pl.program_id(0) == 0)
    def _():
        _cast_rows(wg_hbm, wg_ref, stage_in, sem_in, FFN_CAST_ROWS)
        _cast_rows(wu_hbm, wu_ref, stage_in, sem_in, FFN_CAST_ROWS)
        _cast_rows(wd_hbm, wd_ref, stage_out, sem_out, FFN_CAST_ROWS)

    x = x_ref[...]
    xn = _rmsnorm(x, g_ref[...]).astype(BF16)
    d_ff = wg_ref.shape[1]
    acc = None
    for c in range(d_ff // FFN_COLS):
        sl = slice(c * FFN_COLS, (c + 1) * FFN_COLS)
        gate = _dot(xn, wg_ref[:, sl])
        up = _dot(xn, wu_ref[:, sl])
        act = (jax.nn.silu(gate) * up).astype(BF16)
        part = _dot(act, wd_ref[sl, :])
        acc = part if acc is None else acc + part
    y = x + 0.5 * acc
    if final_norm:
        y = _rmsnorm(y, fg_ref[...])
    o_ref[...] = y


def _ffn(x, norm_g, w_gate, w_up, w_down, final_g, final_norm):
    t, d = x.shape
    d_ff = w_gate.shape[1]
    assert t % FFN_ROWS == 0 and d_ff % FFN_COLS == 0 and d % FFN_CAST_ROWS == 0 and d_ff % FFN_CAST_ROWS == 0
    row = pl.BlockSpec((FFN_ROWS, d), lambda i: (i, 0))
    hbm = pl.BlockSpec(memory_space=pl.ANY)
    return pl.pallas_call(
        functools.partial(_ffn_kernel, final_norm=final_norm),
        grid=(t // FFN_ROWS,),
        in_specs=[row, _resident((1, d)), hbm, hbm, hbm, _resident((1, d))],
        out_specs=row,
        out_shape=jax.ShapeDtypeStruct((t, d), F32),
        scratch_shapes=[pltpu.VMEM((d, d_ff), BF16), pltpu.VMEM((d, d_ff), BF16), pltpu.VMEM((d_ff, d), BF16),
                        pltpu.VMEM((2, FFN_CAST_ROWS, d_ff), F32), pltpu.VMEM((2, FFN_CAST_ROWS, d), F32),
                        pltpu.SemaphoreType.DMA((2,)), pltpu.SemaphoreType.DMA((2,))],
        compiler_params=_compiler_params(("arbitrary",)),
        name="ffn_final" if final_norm else "ffn",
    )(x, norm_g, w_gate, w_up, w_down, final_g)


def _rope(xn, cos_t, sin_t, width):
    half = HEAD_DIM // 2
    lane = lax.broadcasted_iota(jnp.int32, xn.shape, 1)
    first_half = (lane % HEAD_DIM) < half
    partner = jnp.where(first_half, pltpu.roll(xn, width - half, 1), pltpu.roll(xn, half, 1))
    return xn * cos_t + partner * sin_t


def _proj_kernel(x_ref, g_ref, w_ref, qg_ref, kg_ref, gq_ref, gk_ref, cos_ref, sin_ref,
                 q_ref, k_ref, v_ref, hq_ref, ff_ref, fb_ref, hi_ref, hg_ref):
    h = _rmsnorm(x_ref[0], g_ref[...]).astype(BF16)

    def cols(start, width):
        return _dot(h, w_ref[:, start:start + width])

    cos2 = cos_ref[...]
    sin2 = sin_ref[...]
    off = 0
    q = cols(off, ATTN_WIDTH)
    off += ATTN_WIDTH
    reps = ATTN_WIDTH // V7X_LANES
    qn = q * lax.rsqrt(_group_mean(q * q, gq_ref[...]) + NORM_EPS) * qg_ref[...]
    qr = _rope(qn, jnp.concatenate([cos2] * reps, axis=1), jnp.concatenate([sin2] * reps, axis=1),
               ATTN_WIDTH) * (HEAD_DIM ** -0.5)
    for hd in range(ATTN_HEADS):
        q_ref[0, hd] = qr[:, hd * HEAD_DIM:(hd + 1) * HEAD_DIM].astype(BF16)
    k = cols(off, KV_WIDTH)
    off += KV_WIDTH
    kn = k * lax.rsqrt(_group_mean(k * k, gk_ref[...]) + NORM_EPS) * kg_ref[...]
    kr = _rope(kn, cos2, sin2, KV_WIDTH)
    v = cols(off, KV_WIDTH)
    off += KV_WIDTH
    lane = lax.broadcasted_iota(jnp.int32, v.shape, 1)
    for hd in range(ATTN_KV_HEADS):
        k_ref[0, hd] = kr[:, hd * HEAD_DIM:(hd + 1) * HEAD_DIM].astype(BF16)
        v_hd = v if hd == 0 else pltpu.roll(v, KV_WIDTH - hd * HEAD_DIM, 1)
        v_ref[0, hd] = jnp.where(lane < HEAD_DIM, v_hd, 1.0).T.astype(BF16)
    hq_ref[0] = jax.nn.silu(cols(off, HGRN_WIDTH))
    off += HGRN_WIDTH
    ff_ref[0] = cols(off, HGRN_WIDTH)
    off += HGRN_WIDTH
    fb_ref[0] = cols(off, HGRN_WIDTH)
    off += HGRN_WIDTH
    hi_ref[0] = cols(off, HGRN_WIDTH).astype(BF16)
    off += HGRN_WIDTH
    hg_ref[0] = jax.nn.silu(cols(off, HGRN_WIDTH))


def _group_matrix(width, group):
    idx = np.arange(width) // group
    return jnp.asarray((idx[:, None] == idx[None, :]).astype(np.float32) / group, dtype=BF16)


def _rope_tables(seq_len):
    rows = seq_len // GRID_W
    row = jnp.repeat(jnp.arange(rows, dtype=F32), GRID_W)
    col = jnp.tile(jnp.arange(GRID_W, dtype=F32), rows)
    n_freq = HEAD_DIM // 4
    inv = ROPE_THETA ** (-jnp.arange(n_freq, dtype=F32) / n_freq)
    ang = jnp.concatenate([row[:, None] * inv, col[:, None] * inv], axis=-1)
    cos, sin = jnp.cos(ang), jnp.sin(ang)
    cos2 = jnp.concatenate([cos, cos] * 2, axis=-1)
    sin2 = jnp.concatenate([-sin, sin] * 2, axis=-1)
    return cos2, sin2


def _in_proj(x1, norm_g, w_mix, q_g, k_g):
    b, s, d = x1.shape
    n_in = w_mix.shape[1]
    assert s % PROJ_ROWS == 0 and n_in == ATTN_WIDTH + 2 * KV_WIDTH + 5 * HGRN_WIDTH
    cos2, sin2 = _rope_tables(s)
    tm = PROJ_ROWS
    tok = lambda width: pl.BlockSpec((1, tm, width), lambda bi, i: (bi, i, 0))
    heads = lambda n, width=HEAD_DIM: pl.BlockSpec((1, n, tm, width), lambda bi, i: (bi, 0, i, 0))
    pos = pl.BlockSpec((tm, V7X_LANES), lambda bi, i: (i, 0))
    f32_tok = lambda width: jax.ShapeDtypeStruct((b, s, width), F32)
    out_shape = (
        jax.ShapeDtypeStruct((b, ATTN_HEADS, s, HEAD_DIM), BF16),
        jax.ShapeDtypeStruct((b, ATTN_KV_HEADS, s, HEAD_DIM), BF16),
        jax.ShapeDtypeStruct((b, ATTN_KV_HEADS, V7X_LANES, s), BF16),
        f32_tok(HGRN_WIDTH), f32_tok(HGRN_WIDTH), f32_tok(HGRN_WIDTH),
        jax.ShapeDtypeStruct((b, s, HGRN_WIDTH), BF16),
        f32_tok(HGRN_WIDTH),
    )
    vt_spec = pl.BlockSpec((1, ATTN_KV_HEADS, V7X_LANES, tm), lambda bi, i: (bi, 0, 0, i))
    out_specs = (heads(ATTN_HEADS), heads(ATTN_KV_HEADS), vt_spec,
                 tok(HGRN_WIDTH), tok(HGRN_WIDTH), tok(HGRN_WIDTH), tok(HGRN_WIDTH), tok(HGRN_WIDTH))
    return pl.pallas_call(
        _proj_kernel,
        grid=(b, s // tm),
        in_specs=[tok(d), _resident((1, d)), _resident((d, n_in)),
                  _resident((1, ATTN_WIDTH)), _resident((1, KV_WIDTH)),
                  _resident((ATTN_WIDTH, ATTN_WIDTH)), _resident((KV_WIDTH, KV_WIDTH)), pos, pos],
        out_specs=out_specs,
        out_shape=out_shape,
        compiler_params=_compiler_params(("parallel", "parallel")),
        name="in_proj",
    )(x1, norm_g, w_mix, jnp.tile(q_g, (1, ATTN_HEADS)), jnp.tile(k_g, (1, ATTN_KV_HEADS)),
      _group_matrix(ATTN_WIDTH, HEAD_DIM), _group_matrix(KV_WIDTH, HEAD_DIM), cos2, sin2)


def _attn_kernel(bounded_ref, q_ref, k_ref, vt_ref, o_ref):
    tq = q_ref.shape[2]
    s_len = k_ref.shape[2]
    q = q_ref[0].reshape(ATTN_GROUP * tq, HEAD_DIM)

    def finish(accs):
        lane = lax.broadcasted_iota(jnp.int32, (tq, V7X_LANES), 1)
        pairs = []
        for r in range(0, ATTN_GROUP, 2):
            lo, hi = accs[r].T, accs[r + 1].T
            lo_sw, hi_sw = pltpu.roll(lo, HEAD_DIM, 1), pltpu.roll(hi, HEAD_DIM, 1)
            pairs.append(jnp.where(lane < HEAD_DIM, lo / lo_sw, hi_sw / hi))
        o_ref[0] = jnp.concatenate(pairs, axis=1).astype(BF16)

    @pl.when(bounded_ref[0] == 1)
    def _():
        accs = [None] * ATTN_GROUP
        kc = s_len // ATTN_KEY_SPLIT
        for c in range(ATTN_KEY_SPLIT):
            ks = slice(c * kc, (c + 1) * kc)
            p_t = jnp.exp(_dot_nt(k_ref[0, 0, ks, :], q)).astype(BF16)
            vt_c = vt_ref[0, 0, :, ks]
            for r in range(ATTN_GROUP):
                part = _dot(vt_c, p_t[:, r * tq:(r + 1) * tq])
                accs[r] = part if accs[r] is None else accs[r] + part
        finish(accs)

    @pl.when(bounded_ref[0] != 1)
    def _():
        s_t = _dot_nt(k_ref[0, 0], q)
        p_t = jnp.exp(s_t - jnp.max(s_t, axis=0, keepdims=True)).astype(BF16)
        finish([_dot(vt_ref[0, 0], p_t[:, r * tq:(r + 1) * tq]) for r in range(ATTN_GROUP)])


def _attention(bounded, q, k, v):
    b, _, s, _ = q.shape
    tq = ATTN_Q_ROWS
    assert s % tq == 0 and s % (ATTN_KEY_SPLIT * V7X_LANES) == 0
    k_spec = pl.BlockSpec((1, 1, s, HEAD_DIM), lambda bi, g, i, *_: (bi, g, 0, 0))
    vt_spec = pl.BlockSpec((1, 1, V7X_LANES, s), lambda bi, g, i, *_: (bi, g, 0, 0))
    grid_spec = pltpu.PrefetchScalarGridSpec(
        num_scalar_prefetch=1,
        grid=(b, ATTN_KV_HEADS, s // tq),
        in_specs=[pl.BlockSpec((1, ATTN_GROUP, tq, HEAD_DIM), lambda bi, g, i, *_: (bi, g, i, 0)),
                  k_spec, vt_spec],
        out_specs=pl.BlockSpec((1, tq, ATTN_GROUP * HEAD_DIM), lambda bi, g, i, *_: (bi, i, g)),
    )
    return pl.pallas_call(
        _attn_kernel,
        grid_spec=grid_spec,
        out_shape=jax.ShapeDtypeStruct((b, s, ATTN_WIDTH), BF16),
        compiler_params=_compiler_params(("parallel", "parallel", "parallel")),
        name="attention",
    )(bounded, q, k, v)


def _scores_bounded(q_g, k_g):
    bound = (HEAD_DIM ** 0.5) * jnp.max(jnp.abs(q_g)) * jnp.max(jnp.abs(k_g))
    return (bound <= ATTN_SCORE_BOUND).astype(jnp.int32).reshape(1)


def _hgrn_prepare(qt, f_logit, lb, reverse):
    rows = qt.shape[0]
    c = HGRN_CHUNK
    n_chunks = rows // c
    f = lb + (1.0 - lb) * jax.nn.sigmoid(f_logit)
    kk = 1.0 - f
    log_f = jnp.log(f)
    r_i = lax.broadcasted_iota(jnp.int32, (rows, rows), 0)
    c_i = lax.broadcasted_iota(jnp.int32, (rows, rows), 1)
    same_chunk = (r_i // c) == (c_i // c)
    causal = same_chunk & ((c_i >= r_i) if reverse else (c_i <= r_i))
    tri = jnp.where(causal, 1.0, 0.0).astype(BF16)
    hi, mid, lo = _split3(log_f)
    bcum = _dot(tri, hi) + _dot(tri, mid) + _dot(tri, lo)
    b3 = bcum.reshape(n_chunks, c, bcum.shape[-1])
    mid_row = c // 2 if reverse else c // 2 - 1
    last_row = 0 if reverse else c - 1
    b_mid = b3[:, mid_row:mid_row + 1, :]
    b_last = b3[:, last_row:last_row + 1, :]
    q3 = qt.reshape(b3.shape)
    k3 = kk.reshape(b3.shape)
    d_mid = b3 - b_mid
    q_mid = q3 * jnp.exp(d_mid)
    k_mid = k3 * jnp.exp(-d_mid)
    k_end = k_mid * jnp.exp(b_last - b_mid)
    q_in = q_mid * jnp.exp(b_mid)
    flat = lambda a: a.reshape(bcum.shape).astype(BF16)
    sub = HGRN_INTRA_ROWS
    causal_sub = jnp.concatenate([causal[:sub, :sub]] * 2, axis=1)
    return dict(causal_sub=causal_sub, q_mid=flat(q_mid), k_mid=flat(k_mid), k_end=flat(k_end), q_in=flat(q_in),
                decay=jnp.exp(b_last))


def _pair_slice(p):
    return slice(p * HGRN_PAIR, (p + 1) * HGRN_PAIR)


def _hgrn_intra(ops, v, p):
    rows = v.shape[0]
    sub = HGRN_INTRA_ROWS
    lower_head = lax.broadcasted_iota(jnp.int32, (sub, HGRN_PAIR), 1) < HGRN_DIM
    zero_bf = jnp.zeros((), BF16)
    stack = lambda x: jnp.concatenate([jnp.where(lower_head, x, zero_bf), jnp.where(lower_head, zero_bf, x)], axis=0)
    causal = ops["causal_sub"]
    sl = _pair_slice(p)
    outs = []
    for h in range(rows // sub):
        rs = slice(h * sub, (h + 1) * sub)
        scores = _dot_nt(ops["q_mid"][rs, sl], stack(ops["k_mid"][rs, sl]))
        scores = jnp.where(causal, scores, 0.0).astype(BF16)
        outs.append(_dot(scores, stack(v[rs, sl])))
    return jnp.concatenate(outs, axis=0)


def _hgrn_contribs(ops, v):
    c = HGRN_CHUNK
    n_chunks = v.shape[0] // c
    return [[_dot_tn(v[n * c:(n + 1) * c, _pair_slice(p)], ops["k_end"][n * c:(n + 1) * c, _pair_slice(p)])
             for n in range(n_chunks)] for p in range(HGRN_PAIRS)]


def _hgrn_scan(ops, contribs, state_ref, reverse):
    n_chunks = len(contribs[0])
    sr = lax.broadcasted_iota(jnp.int32, (HGRN_PAIR, HGRN_PAIR), 0)
    sc = lax.broadcasted_iota(jnp.int32, (HGRN_PAIR, HGRN_PAIR), 1)
    same_head = (sr // HGRN_DIM) == (sc // HGRN_DIM)
    order = range(n_chunks - 1, -1, -1) if reverse else range(n_chunks)
    states = [[None] * n_chunks for _ in range(HGRN_PAIRS)]
    for p in range(HGRN_PAIRS):
        st = state_ref[p]
        for n in order:
            states[p][n] = st.astype(BF16)
            st = ops["decay"][n, :, _pair_slice(p)] * st + jnp.where(same_head, contribs[p][n], 0.0)
        state_ref[p] = st
    return states


def _hgrn_inter(ops, states, p):
    c = HGRN_CHUNK
    parts = [_dot_nt(ops["q_in"][n * c:(n + 1) * c, _pair_slice(p)], st) for n, st in enumerate(states[p])]
    return jnp.concatenate(parts, axis=0)


def _hgrn_kernel(lbl_ref, qf_ref, ff_ref, vf_ref, qb_ref, fb_ref, vb_ref, of_ref, ob_ref,
                 st_f_ref, st_b_ref):
    @pl.when(pl.program_id(1) == 0)
    def _():
        st_f_ref[...] = jnp.zeros_like(st_f_ref)
        st_b_ref[...] = jnp.zeros_like(st_b_ref)

    logits = lbl_ref[...]
    e = jnp.exp(logits - jnp.max(logits, axis=1, keepdims=True))
    lb = e[:, 0, :] / jnp.sum(e, axis=1)
    v_f, v_b = vf_ref[0], vb_ref[0]
    ops_f = _hgrn_prepare(qf_ref[0], ff_ref[0], lb[0:1], False)
    ops_b = _hgrn_prepare(qb_ref[0], fb_ref[0], lb[1:2], True)
    con_f = _hgrn_contribs(ops_f, v_f)
    con_b = _hgrn_contribs(ops_b, v_b)
    states_f = _hgrn_scan(ops_f, con_f, st_f_ref, False)
    states_b = _hgrn_scan(ops_b, con_b, st_b_ref, True)
    pairs = range(HGRN_PAIRS)
    inter_f = [_hgrn_inter(ops_f, states_f, p) for p in pairs]
    inter_b = [_hgrn_inter(ops_b, states_b, p) for p in pairs]
    intra_f = [_hgrn_intra(ops_f, v_f, p) for p in pairs]
    intra_b = [_hgrn_intra(ops_b, v_b, p) for p in pairs]
    of_ref[0] = jnp.concatenate([a + b for a, b in zip(intra_f, inter_f)], axis=1)
    ob_ref[0] = jnp.concatenate([a + b for a, b in zip(intra_b, inter_b)], axis=1)


def _hgrn(lb_logits, hq, hff, hfb, hi):
    b, s, w = hq.shape
    r = HGRN_ROWS
    assert s % r == 0 and r % HGRN_INTRA_ROWS == 0 and HGRN_INTRA_ROWS % HGRN_CHUNK == 0
    nb = s // r
    fwd = pl.BlockSpec((1, r, w), lambda bi, j: (bi, j, 0))
    bwd = pl.BlockSpec((1, r, w), lambda bi, j: (bi, nb - 1 - j, 0))
    state = pltpu.VMEM((HGRN_PAIRS, HGRN_PAIR, HGRN_PAIR), F32)
    return pl.pallas_call(
        _hgrn_kernel,
        grid=(b, nb),
        in_specs=[_resident(lb_logits.shape), fwd, fwd, fwd, bwd, bwd, bwd],
        out_specs=(fwd, bwd),
        out_shape=(jax.ShapeDtypeStruct((b, s, w), F32), jax.ShapeDtypeStruct((b, s, w), F32)),
        scratch_shapes=[state, state],
        compiler_params=_compiler_params(("parallel", "arbitrary")),
        name="hgrn",
    )(lb_logits, hq, hff, hi, hq, hfb, hi)


def _merge_kernel(x_ref, ya_ref, of_ref, ob_ref, hg_ref, mg_ref, wg_ref, ng_ref, gm_ref,
                  wa_ref, wb_ref, wo_ref, o_ref):
    x = x_ref[...]
    d = x.shape[-1]
    h = _rmsnorm(x, mg_ref[...]).astype(BF16)
    gate_a = jax.nn.sigmoid(_dot(h, wg_ref[:, :d]))
    gate_b = jax.nn.sigmoid(_dot(h, wg_ref[:, d:]))
    o = of_ref[...] + ob_ref[...]
    on = o * lax.rsqrt(_group_mean(o * o, gm_ref[...]) + NORM_EPS) * ng_ref[...]
    yb = (on * hg_ref[...]).astype(BF16)
    merged = gate_a * _dot(ya_ref[...], wa_ref[...]) + gate_b * _dot(yb, wb_ref[...])
    o_ref[...] = x + _dot(merged.astype(BF16), wo_ref[...])


def _merge(x1, y_attn, o_f, o_b, hg, mix_g, w_gates, norm_g, w_a, w_b, w_out):
    t, d = x1.shape
    tm = MERGE_ROWS
    assert t % tm == 0
    row = lambda width: pl.BlockSpec((tm, width), lambda i: (i, 0))
    return pl.pallas_call(
        _merge_kernel,
        grid=(t // tm,),
        in_specs=[row(d), row(ATTN_WIDTH), row(HGRN_WIDTH), row(HGRN_WIDTH), row(HGRN_WIDTH),
                  _resident((1, d)), _resident((d, 2 * d)),
                  _resident((1, HGRN_WIDTH)), _resident((HGRN_WIDTH, HGRN_WIDTH)),
                  _resident((ATTN_WIDTH, d)), _resident((HGRN_WIDTH, d)), _resident((d, d))],
        out_specs=row(d),
        out_shape=jax.ShapeDtypeStruct((t, d), F32),
        compiler_params=_compiler_params(("parallel",)),
        name="merge",
    )(x1, y_attn, o_f, o_b, hg, mix_g, w_gates, norm_g, _group_matrix(HGRN_WIDTH, HGRN_DIM), w_a, w_b, w_out)


def kernel(x, ffn1_norm_g, ffn1_w_gate, ffn1_w_up, ffn1_w_down, mix_norm_g, w_in, q_norm_g, k_norm_g,
           hgrn_lb_logits, hgrn_out_norm_g, w_branch_attn, w_branch_hgrn, w_out, ffn2_norm_g,
           ffn2_w_gate, ffn2_w_up, ffn2_w_down, final_norm_g):
    b, s, d = x.shape
    t = b * s
    depth = w_in.shape[0]
    assert depth == 1 and hgrn_lb_logits.shape[1] == depth + 1
    bf = lambda w: w.astype(BF16)
    final_g = final_norm_g.reshape(1, d)
    n_mix = w_in.shape[2] - 2 * d

    x1 = _ffn(x.reshape(t, d), ffn1_norm_g, ffn1_w_gate[0], ffn1_w_up[0], ffn1_w_down[0], final_g, False)
    q, k, v, hq, hff, hfb, hi, hg = _in_proj(
        x1.reshape(b, s, d), mix_norm_g, bf(w_in[0, :, :n_mix]), q_norm_g, k_norm_g)
    y_attn = _attention(_scores_bounded(q_norm_g, k_norm_g), q, k, v)
    o_f, o_b = _hgrn(hgrn_lb_logits, hq, hff, hfb, hi)
    flat = lambda a: a.reshape(t, a.shape[-1])
    x2 = _merge(x1, flat(y_attn), flat(o_f), flat(o_b), flat(hg), mix_norm_g, bf(w_in[0, :, n_mix:]),
                hgrn_out_norm_g, bf(w_branch_attn[0]), bf(w_branch_hgrn[0]), bf(w_out[0]))
    out = _ffn(x2, ffn2_norm_g, ffn2_w_gate[0], ffn2_w_up[0], ffn2_w_down[0], final_g, True)
    return out.reshape(b, s, d)
```

```python
import functools

import jax
import jax.numpy as jnp
import numpy as np
from jax import lax
from jax.experimental import pallas as pl
from jax.experimental.pallas import tpu as pltpu

F32 = jnp.float32
BF16 = jnp.bfloat16

NORM_EPS = 1e-6
ROPE_THETA = 10000.0
GRID_W = 64
HEAD_DIM = 64
ATTN_HEADS = 8
ATTN_KV_HEADS = 2
ATTN_GROUP = ATTN_HEADS // ATTN_KV_HEADS
HGRN_HEADS = 8
HGRN_DIM = 64
HGRN_CHUNK = 32
ATTN_WIDTH = ATTN_HEADS * HEAD_DIM
KV_WIDTH = ATTN_KV_HEADS * HEAD_DIM
HGRN_WIDTH = HGRN_HEADS * HGRN_DIM
HGRN_PAIR = 2 * HGRN_DIM
HGRN_PAIRS = HGRN_HEADS // 2

V7X_LANES = 128
V7X_MXU_DIM = 256
V7X_VMEM_LIMIT_BYTES = 56 * 1024 * 1024

FFN_ROWS = 512
FFN_COLS = V7X_MXU_DIM
PROJ_ROWS = 1024
ATTN_Q_ROWS = 512
ATTN_KEY_SPLIT = 4
ATTN_SCORE_BOUND = 60.0
HGRN_ROWS = 256
HGRN_INTRA_ROWS = V7X_MXU_DIM // 2
MERGE_ROWS = 1024


def _compiler_params(semantics):
    return pltpu.CompilerParams(dimension_semantics=semantics, vmem_limit_bytes=V7X_VMEM_LIMIT_BYTES)


def _resident(shape):
    zeros = (0,) * len(shape)
    return pl.BlockSpec(shape, lambda *_: zeros, pipeline_mode=pl.Buffered(1))


def _rmsnorm(x, g):
    return x * lax.rsqrt(jnp.mean(x * x, axis=-1, keepdims=True) + NORM_EPS) * g


def _dot(a, b):
    return jnp.dot(a, b, preferred_element_type=F32)


def _dot_nt(a, b):
    return lax.dot_general(a, b, (((1,), (1,)), ((), ())), preferred_element_type=F32)


def _dot_tn(a, b):
    return lax.dot_general(a, b, (((0,), (0,)), ((), ())), preferred_element_type=F32)


def _split2(x):
    hi = x.astype(BF16)
    lo = (x - hi.astype(F32)).astype(BF16)
    return hi, lo


def _split3(x):
    hi = x.astype(BF16)
    r = x - hi.astype(F32)
    mid = r.astype(BF16)
    lo = (r - mid.astype(F32)).astype(BF16)
    return hi, mid, lo


def _group_mean(x, gmat):
    hi, lo = _split2(x)
    return _dot(hi, gmat) + _dot(lo, gmat)


def _ffn_kernel(x_ref, g_ref, wg_hbm, wu_hbm, wd_hbm, fg_ref, o_ref,
                wg_ref, wu_ref, wd_ref, stage_g, stage_u, stage_d, sem, *, final_norm):
    d_ff = wg_ref.shape[1]
    n_chunks = d_ff // FFN_COLS

    def chunk_copies(c):
        cols = pl.ds(c * FFN_COLS, FFN_COLS)
        slot = c % 2
        return (pltpu.make_async_copy(wg_hbm.at[:, cols], stage_g.at[slot], sem.at[0, slot]),
                pltpu.make_async_copy(wu_hbm.at[:, cols], stage_u.at[slot], sem.at[1, slot]),
                pltpu.make_async_copy(wd_hbm.at[cols, :], stage_d.at[slot], sem.at[2, slot]))

    def tile(stream_weights):
        if stream_weights:
            for cp in chunk_copies(0):
                cp.start()
        x = x_ref[...]
        xn = _rmsnorm(x, g_ref[...]).astype(BF16)
        acc = None
        for c in range(n_chunks):
            sl = slice(c * FFN_COLS, (c + 1) * FFN_COLS)
            if stream_weights:
                if c + 1 < n_chunks:
                    for cp in chunk_copies(c + 1):
                        cp.start()
                for cp in chunk_copies(c):
                    cp.wait()
                wg_ref[:, sl] = stage_g[c % 2].astype(BF16)
                wu_ref[:, sl] = stage_u[c % 2].astype(BF16)
                wd_ref[sl, :] = stage_d[c % 2].astype(BF16)
            gate = _dot(xn, wg_ref[:, sl])
            up = _dot(xn, wu_ref[:, sl])
            act = (jax.nn.silu(gate) * up).astype(BF16)
            part = _dot(act, wd_ref[sl, :])
            acc = part if acc is None else acc + part
        y = x + 0.5 * acc
        if final_norm:
            y = _rmsnorm(y, fg_ref[...])
        o_ref[...] = y

    pl.when(pl.program_id(0) == 0)(functools.partial(tile, True))
    pl.when(pl.program_id(0) != 0)(functools.partial(tile, False))


def _ffn(x, norm_g, w_gate, w_up, w_down, final_g, final_norm):
    t, d = x.shape
    d_ff = w_gate.shape[1]
    assert t % FFN_ROWS == 0 and d_ff % FFN_COLS == 0
    row = pl.BlockSpec((FFN_ROWS, d), lambda i: (i, 0))
    hbm = pl.BlockSpec(memory_space=pl.ANY)
    return pl.pallas_call(
        functools.partial(_ffn_kernel, final_norm=final_norm),
        grid=(t // FFN_ROWS,),
        in_specs=[row, _resident((1, d)), hbm, hbm, hbm, _resident((1, d))],
        out_specs=row,
        out_shape=jax.ShapeDtypeStruct((t, d), F32),
        scratch_shapes=[pltpu.VMEM((d, d_ff), BF16), pltpu.VMEM((d, d_ff), BF16), pltpu.VMEM((d_ff, d), BF16),
                        pltpu.VMEM((2, d, FFN_COLS), F32), pltpu.VMEM((2, d, FFN_COLS), F32),
                        pltpu.VMEM((2, FFN_COLS, d), F32), pltpu.SemaphoreType.DMA((3, 2))],
        compiler_params=_compiler_params(("arbitrary",)),
        name="ffn_final" if final_norm else "ffn",
    )(x, norm_g, w_gate, w_up, w_down, final_g)


def _rope(xn, cos_t, sin_t, width):
    half = HEAD_DIM // 2
    lane = lax.broadcasted_iota(jnp.int32, xn.shape, 1)
    first_half = (lane % HEAD_DIM) < half
    partner = jnp.where(first_half, pltpu.roll(xn, width - half, 1), pltpu.roll(xn, half, 1))
    return xn * cos_t + partner * sin_t


def _proj_kernel(x_ref, g_ref, w_ref, qg_ref, kg_ref, gq_ref, gk_ref, cos_ref, sin_ref,
                 q_ref, k_ref, v_ref, hq_ref, ff_ref, fb_ref, hi_ref, hg_ref):
    h = _rmsnorm(x_ref[0], g_ref[...]).astype(BF16)

    def cols(start, width):
        return _dot(h, w_ref[:, start:start + width])

    cos2 = cos_ref[...]
    sin2 = sin_ref[...]
    off = 0
    q = cols(off, ATTN_WIDTH)
    off += ATTN_WIDTH
    reps = ATTN_WIDTH // V7X_LANES
    qn = q * lax.rsqrt(_group_mean(q * q, gq_ref[...]) + NORM_EPS) * qg_ref[...]
    qr = _rope(qn, jnp.concatenate([cos2] * reps, axis=1), jnp.concatenate([sin2] * reps, axis=1),
               ATTN_WIDTH) * (HEAD_DIM ** -0.5)
    for hd in range(ATTN_HEADS):
        q_ref[0, hd] = qr[:, hd * HEAD_DIM:(hd + 1) * HEAD_DIM].astype(BF16)
    k = cols(off, KV_WIDTH)
    off += KV_WIDTH
    kn = k * lax.rsqrt(_group_mean(k * k, gk_ref[...]) + NORM_EPS) * kg_ref[...]
    kr = _rope(kn, cos2, sin2, KV_WIDTH)
    v = cols(off, KV_WIDTH)
    off += KV_WIDTH
    lane = lax.broadcasted_iota(jnp.int32, v.shape, 1)
    for hd in range(ATTN_KV_HEADS):
        k_ref[0, hd] = kr[:, hd * HEAD_DIM:(hd + 1) * HEAD_DIM].astype(BF16)
        v_hd = v if hd == 0 else pltpu.roll(v, KV_WIDTH - hd * HEAD_DIM, 1)
        v_ref[0, hd] = jnp.where(lane < HEAD_DIM, v_hd, 1.0).T.astype(BF16)
    hq_ref[0] = jax.nn.silu(cols(off, HGRN_WIDTH))
    off += HGRN_WIDTH
    ff_ref[0] = cols(off, HGRN_WIDTH)
    off += HGRN_WIDTH
    fb_ref[0] = cols(off, HGRN_WIDTH)
    off += HGRN_WIDTH
    hi_ref[0] = cols(off, HGRN_WIDTH).astype(BF16)
    off += HGRN_WIDTH
    hg_ref[0] = jax.nn.silu(cols(off, HGRN_WIDTH))


def _group_matrix(width, group):
    idx = np.arange(width) // group
    return jnp.asarray((idx[:, None] == idx[None, :]).astype(np.float32) / group, dtype=BF16)


def _rope_tables(seq_len):
    f32 = np.float32
    rows = seq_len // GRID_W
    row = np.repeat(np.arange(rows, dtype=f32), GRID_W)
    col = np.tile(np.arange(GRID_W, dtype=f32), rows)
    n_freq = HEAD_DIM // 4
    inv = (f32(ROPE_THETA) ** (-np.arange(n_freq, dtype=f32) / f32(n_freq))).astype(f32)
    ang = np.concatenate([row[:, None] * inv, col[:, None] * inv], axis=-1)
    cos, sin = np.cos(ang).astype(f32), np.sin(ang).astype(f32)
    cos2 = np.concatenate([cos, cos] * 2, axis=-1)
    sin2 = np.concatenate([-sin, sin] * 2, axis=-1)
    return jnp.asarray(cos2), jnp.asarray(sin2)


def _in_proj(x1, norm_g, w_mix, q_g, k_g):
    b, s, d = x1.shape
    n_in = w_mix.shape[1]
    assert s % PROJ_ROWS == 0 and n_in == ATTN_WIDTH + 2 * KV_WIDTH + 5 * HGRN_WIDTH
    cos2, sin2 = _rope_tables(s)
    tm = PROJ_ROWS
    tok = lambda width: pl.BlockSpec((1, tm, width), lambda bi, i: (bi, i, 0))
    heads = lambda n, width=HEAD_DIM: pl.BlockSpec((1, n, tm, width), lambda bi, i: (bi, 0, i, 0))
    pos = pl.BlockSpec((tm, V7X_LANES), lambda bi, i: (i, 0))
    f32_tok = lambda width: jax.ShapeDtypeStruct((b, s, width), F32)
    out_shape = (
        jax.ShapeDtypeStruct((b, ATTN_HEADS, s, HEAD_DIM), BF16),
        jax.ShapeDtypeStruct((b, ATTN_KV_HEADS, s, HEAD_DIM), BF16),
        jax.ShapeDtypeStruct((b, ATTN_KV_HEADS, V7X_LANES, s), BF16),
        f32_tok(HGRN_WIDTH), f32_tok(HGRN_WIDTH), f32_tok(HGRN_WIDTH),
        jax.ShapeDtypeStruct((b, s, HGRN_WIDTH), BF16),
        f32_tok(HGRN_WIDTH),
    )
    vt_spec = pl.BlockSpec((1, ATTN_KV_HEADS, V7X_LANES, tm), lambda bi, i: (bi, 0, 0, i))
    out_specs = (heads(ATTN_HEADS), heads(ATTN_KV_HEADS), vt_spec,
                 tok(HGRN_WIDTH), tok(HGRN_WIDTH), tok(HGRN_WIDTH), tok(HGRN_WIDTH), tok(HGRN_WIDTH))
    return pl.pallas_call(
        _proj_kernel,
        grid=(b, s // tm),
        in_specs=[tok(d), _resident((1, d)), _resident((d, n_in)),
                  _resident((1, ATTN_WIDTH)), _resident((1, KV_WIDTH)),
                  _resident((ATTN_WIDTH, ATTN_WIDTH)), _resident((KV_WIDTH, KV_WIDTH)), pos, pos],
        out_specs=out_specs,
        out_shape=out_shape,
        compiler_params=_compiler_params(("parallel", "parallel")),
        name="in_proj",
    )(x1, norm_g, w_mix, jnp.tile(q_g, (1, ATTN_HEADS)), jnp.tile(k_g, (1, ATTN_KV_HEADS)),
      _group_matrix(ATTN_WIDTH, HEAD_DIM), _group_matrix(KV_WIDTH, HEAD_DIM), cos2, sin2)


def _attn_kernel(bounded_ref, q_ref, k_ref, vt_ref, o_ref):
    tq = q_ref.shape[2]
    s_len = k_ref.shape[2]
    q = q_ref[0].reshape(ATTN_GROUP * tq, HEAD_DIM)

    def finish(accs):
        lane = lax.broadcasted_iota(jnp.int32, (tq, V7X_LANES), 1)
        pairs = []
        for r in range(0, ATTN_GROUP, 2):
            lo, hi = accs[r].T, accs[r + 1].T
            lo_sw, hi_sw = pltpu.roll(lo, HEAD_DIM, 1), pltpu.roll(hi, HEAD_DIM, 1)
            pairs.append(jnp.where(lane < HEAD_DIM, lo / lo_sw, hi_sw / hi))
        o_ref[0] = jnp.concatenate(pairs, axis=1).astype(BF16)

    @pl.when(bounded_ref[0] == 1)
    def _():
        accs = [None] * ATTN_GROUP
        kc = s_len // ATTN_KEY_SPLIT
        for c in range(ATTN_KEY_SPLIT):
            ks = slice(c * kc, (c + 1) * kc)
            p_t = jnp.exp(_dot_nt(k_ref[0, 0, ks, :], q)).astype(BF16)
            vt_c = vt_ref[0, 0, :, ks]
            for r in range(ATTN_GROUP):
                part = _dot(vt_c, p_t[:, r * tq:(r + 1) * tq])
                accs[r] = part if accs[r] is None else accs[r] + part
        finish(accs)

    @pl.when(bounded_ref[0] != 1)
    def _():
        s_t = _dot_nt(k_ref[0, 0], q)
        p_t = jnp.exp(s_t - jnp.max(s_t, axis=0, keepdims=True)).astype(BF16)
        finish([_dot(vt_ref[0, 0], p_t[:, r * tq:(r + 1) * tq]) for r in range(ATTN_GROUP)])


def _attention(bounded, q, k, v):
    b, _, s, _ = q.shape
    tq = ATTN_Q_ROWS
    assert s % tq == 0 and s % (ATTN_KEY_SPLIT * V7X_LANES) == 0
    k_spec = pl.BlockSpec((1, 1, s, HEAD_DIM), lambda bi, g, i, *_: (bi, g, 0, 0))
    vt_spec = pl.BlockSpec((1, 1, V7X_LANES, s), lambda bi, g, i, *_: (bi, g, 0, 0))
    grid_spec = pltpu.PrefetchScalarGridSpec(
        num_scalar_prefetch=1,
        grid=(b, ATTN_KV_HEADS, s // tq),
        in_specs=[pl.BlockSpec((1, ATTN_GROUP, tq, HEAD_DIM), lambda bi, g, i, *_: (bi, g, i, 0)),
                  k_spec, vt_spec],
        out_specs=pl.BlockSpec((1, tq, ATTN_GROUP * HEAD_DIM), lambda bi, g, i, *_: (bi, i, g)),
    )
    return pl.pallas_call(
        _attn_kernel,
        grid_spec=grid_spec,
        out_shape=jax.ShapeDtypeStruct((b, s, ATTN_WIDTH), BF16),
        compiler_params=_compiler_params(("parallel", "parallel", "parallel")),
        name="attention",
    )(bounded, q, k, v)


def _scores_bounded(q_g, k_g):
    bound = (HEAD_DIM ** 0.5) * jnp.max(jnp.abs(q_g)) * jnp.max(jnp.abs(k_g))
    return (bound <= ATTN_SCORE_BOUND).astype(jnp.int32).reshape(1)


def _hgrn_prepare(qt, f_logit, lb, reverse):
    rows = qt.shape[0]
    c = HGRN_CHUNK
    n_chunks = rows // c
    f = lb + (1.0 - lb) * jax.nn.sigmoid(f_logit)
    kk = 1.0 - f
    log_f = jnp.log(f)
    r_i = lax.broadcasted_iota(jnp.int32, (rows, rows), 0)
    c_i = lax.broadcasted_iota(jnp.int32, (rows, rows), 1)
    same_chunk = (r_i // c) == (c_i // c)
    causal = same_chunk & ((c_i >= r_i) if reverse else (c_i <= r_i))
    tri = jnp.where(causal, 1.0, 0.0).astype(BF16)
    hi, mid, lo = _split3(log_f)
    bcum = _dot(tri, hi) + _dot(tri, mid) + _dot(tri, lo)
    b3 = bcum.reshape(n_chunks, c, bcum.shape[-1])
    mid_row = c // 2 if reverse else c // 2 - 1
    last_row = 0 if reverse else c - 1
    b_mid = b3[:, mid_row:mid_row + 1, :]
    b_last = b3[:, last_row:last_row + 1, :]
    q3 = qt.reshape(b3.shape)
    k3 = kk.reshape(b3.shape)
    d_mid = b3 - b_mid
    q_mid = q3 * jnp.exp(d_mid)
    k_mid = k3 * jnp.exp(-d_mid)
    k_end = k_mid * jnp.exp(b_last - b_mid)
    q_in = q_mid * jnp.exp(b_mid)
    flat = lambda a: a.reshape(bcum.shape).astype(BF16)
    sub = HGRN_INTRA_ROWS
    causal_sub = jnp.concatenate([causal[:sub, :sub]] * 2, axis=1)
    return dict(causal_sub=causal_sub, q_mid=flat(q_mid), k_mid=flat(k_mid), k_end=flat(k_end), q_in=flat(q_in),
                decay=jnp.exp(b_last))


def _pair_slice(p):
    return slice(p * HGRN_PAIR, (p + 1) * HGRN_PAIR)


def _hgrn_intra(ops, v, p):
    rows = v.shape[0]
    sub = HGRN_INTRA_ROWS
    lower_head = lax.broadcasted_iota(jnp.int32, (sub, HGRN_PAIR), 1) < HGRN_DIM
    zero_bf = jnp.zeros((), BF16)
    stack = lambda x: jnp.concatenate([jnp.where(lower_head, x, zero_bf), jnp.where(lower_head, zero_bf, x)], axis=0)
    causal = ops["causal_sub"]
    sl = _pair_slice(p)
    outs = []
    for h in range(rows // sub):
        rs = slice(h * sub, (h + 1) * sub)
        scores = _dot_nt(ops["q_mid"][rs, sl], stack(ops["k_mid"][rs, sl]))
        scores = jnp.where(causal, scores, 0.0).astype(BF16)
        outs.append(_dot(scores, stack(v[rs, sl])))
    return jnp.concatenate(outs, axis=0)


def _hgrn_contribs(ops, v):
    c = HGRN_CHUNK
    n_chunks = v.shape[0] // c
    return [[_dot_tn(v[n * c:(n + 1) * c, _pair_slice(p)], ops["k_end"][n * c:(n + 1) * c, _pair_slice(p)])
             for n in range(n_chunks)] for p in range(HGRN_PAIRS)]


def _hgrn_scan(ops, contribs, state_ref, reverse):
    n_chunks = len(contribs[0])
    sr = lax.broadcasted_iota(jnp.int32, (HGRN_PAIR, HGRN_PAIR), 0)
    sc = lax.broadcasted_iota(jnp.int32, (HGRN_PAIR, HGRN_PAIR), 1)
    same_head = (sr // HGRN_DIM) == (sc // HGRN_DIM)
    order = range(n_chunks - 1, -1, -1) if reverse else range(n_chunks)
    states = [[None] * n_chunks for _ in range(HGRN_PAIRS)]
    for p in range(HGRN_PAIRS):
        st = state_ref[p]
        for n in order:
            states[p][n] = st.astype(BF16)
            st = ops["decay"][n, :, _pair_slice(p)] * st + jnp.where(same_head, contribs[p][n], 0.0)
        state_ref[p] = st
    return states


def _hgrn_inter(ops, states, p):
    c = HGRN_CHUNK
    parts = [_dot_nt(ops["q_in"][n * c:(n + 1) * c, _pair_slice(p)], st) for n, st in enumerate(states[p])]
    return jnp.concatenate(parts, axis=0)


def _hgrn_kernel(lbl_ref, qf_ref, ff_ref, vf_ref, qb_ref, fb_ref, vb_ref, of_ref, ob_ref,
                 st_f_ref, st_b_ref):
    @pl.when(pl.program_id(1) == 0)
    def _():
        st_f_ref[...] = jnp.zeros_like(st_f_ref)
        st_b_ref[...] = jnp.zeros_like(st_b_ref)

    logits = lbl_ref[...]
    e = jnp.exp(logits - jnp.max(logits, axis=1, keepdims=True))
    lb = e[:, 0, :] / jnp.sum(e, axis=1)
    v_f, v_b = vf_ref[0], vb_ref[0]
    ops_f = _hgrn_prepare(qf_ref[0], ff_ref[0], lb[0:1], False)
    ops_b = _hgrn_prepare(qb_ref[0], fb_ref[0], lb[1:2], True)
    con_f = _hgrn_contribs(ops_f, v_f)
    con_b = _hgrn_contribs(ops_b, v_b)
    states_f = _hgrn_scan(ops_f, con_f, st_f_ref, False)
    states_b = _hgrn_scan(ops_b, con_b, st_b_ref, True)
    pairs = range(HGRN_PAIRS)
    inter_f = [_hgrn_inter(ops_f, states_f, p) for p in pairs]
    inter_b = [_hgrn_inter(ops_b, states_b, p) for p in pairs]
    intra_f = [_hgrn_intra(ops_f, v_f, p) for p in pairs]
    intra_b = [_hgrn_intra(ops_b, v_b, p) for p in pairs]
    of_ref[0] = jnp.concatenate([a + b for a, b in zip(intra_f, inter_f)], axis=1)
    ob_ref[0] = jnp.concatenate([a + b for a, b in zip(intra_b, inter_b)], axis=1)


def _hgrn(lb_logits, hq, hff, hfb, hi):
    b, s, w = hq.shape
    r = HGRN_ROWS
    assert s % r == 0 and r % HGRN_INTRA_ROWS == 0 and HGRN_INTRA_ROWS % HGRN_CHUNK == 0
    nb = s // r
    fwd = pl.BlockSpec((1, r, w), lambda bi, j: (bi, j, 0))
    bwd = pl.BlockSpec((1, r, w), lambda bi, j: (bi, nb - 1 - j, 0))
    state = pltpu.VMEM((HGRN_PAIRS, HGRN_PAIR, HGRN_PAIR), F32)
    return pl.pallas_call(
        _hgrn_kernel,
        grid=(b, nb),
        in_specs=[_resident(lb_logits.shape), fwd, fwd, fwd, bwd, bwd, bwd],
        out_specs=(fwd, bwd),
        out_shape=(jax.ShapeDtypeStruct((b, s, w), F32), jax.ShapeDtypeStruct((b, s, w), F32)),
        scratch_shapes=[state, state],
        compiler_params=_compiler_params(("parallel", "arbitrary")),
        name="hgrn",
    )(lb_logits, hq, hff, hi, hq, hfb, hi)


def _merge_kernel(x_ref, ya_ref, of_ref, ob_ref, hg_ref, mg_ref, wg_ref, ng_ref, gm_ref,
                  wa_ref, wb_ref, wo_ref, o_ref):
    x = x_ref[...]
    d = x.shape[-1]
    h = _rmsnorm(x, mg_ref[...]).astype(BF16)
    gate_a = jax.nn.sigmoid(_dot(h, wg_ref[:, :d]))
    gate_b = jax.nn.sigmoid(_dot(h, wg_ref[:, d:]))
    o = of_ref[...] + ob_ref[...]
    on = o * lax.rsqrt(_group_mean(o * o, gm_ref[...]) + NORM_EPS) * ng_ref[...]
    yb = (on * hg_ref[...]).astype(BF16)
    merged = gate_a * _dot(ya_ref[...], wa_ref[...]) + gate_b * _dot(yb, wb_ref[...])
    o_ref[...] = x + _dot(merged.astype(BF16), wo_ref[...])


def _merge(x1, y_attn, o_f, o_b, hg, mix_g, w_gates, norm_g, w_a, w_b, w_out):
    t, d = x1.shape
    tm = MERGE_ROWS
    assert t % tm == 0
    row = lambda width: pl.BlockSpec((tm, width), lambda i: (i, 0))
    return pl.pallas_call(
        _merge_kernel,
        grid=(t // tm,),
        in_specs=[row(d), row(ATTN_WIDTH), row(HGRN_WIDTH), row(HGRN_WIDTH), row(HGRN_WIDTH),
                  _resident((1, d)), _resident((d, 2 * d)),
                  _resident((1, HGRN_WIDTH)), _resident((HGRN_WIDTH, HGRN_WIDTH)),
                  _resident((ATTN_WIDTH, d)), _resident((HGRN_WIDTH, d)), _resident((d, d))],
        out_specs=row(d),
        out_shape=jax.ShapeDtypeStruct((t, d), F32),
        compiler_params=_compiler_params(("parallel",)),
        name="merge",
    )(x1, y_attn, o_f, o_b, hg, mix_g, w_gates, norm_g, _group_matrix(HGRN_WIDTH, HGRN_DIM), w_a, w_b, w_out)


def kernel(x, ffn1_norm_g, ffn1_w_gate, ffn1_w_up, ffn1_w_down, mix_norm_g, w_in, q_norm_g, k_norm_g,
           hgrn_lb_logits, hgrn_out_norm_g, w_branch_attn, w_branch_hgrn, w_out, ffn2_norm_g,
           ffn2_w_gate, ffn2_w_up, ffn2_w_down, final_norm_g):
    b, s, d = x.shape
    t = b * s
    depth = w_in.shape[0]
    assert depth == 1 and hgrn_lb_logits.shape[1] == depth + 1
    bf = lambda w: w.astype(BF16)
    final_g = final_norm_g.reshape(1, d)
    n_mix = w_in.shape[2] - 2 * d

    x1 = _ffn(x.reshape(t, d), ffn1_norm_g, ffn1_w_gate[0], ffn1_w_up[0], ffn1_w_down[0], final_g, False)
    q, k, v, hq, hff, hfb, hi, hg = _in_proj(
        x1.reshape(b, s, d), mix_norm_g, bf(w_in[0, :, :n_mix]), q_norm_g, k_norm_g)
    y_attn = _attention(_scores_bounded(q_norm_g, k_norm_g), q, k, v)
    o_f, o_b = _hgrn(hgrn_lb_logits, hq, hff, hfb, hi)
    flat = lambda a: a.reshape(t, a.shape[-1])
    x2 = _merge(x1, flat(y_attn), flat(o_f), flat(o_b), flat(hg), mix_norm_g, bf(w_in[0, :, n_mix:]),
                hgrn_out_norm_g, bf(w_branch_attn[0]), bf(w_branch_hgrn[0]), bf(w_out[0]))
    out = _ffn(x2, ffn2_norm_g, ffn2_w_gate[0], ffn2_w_up[0], ffn2_w_down[0], final_g, True)
    return out.reshape(b, s, d)
```

```python
import functools

import jax
import jax.numpy as jnp
import numpy as np
from jax import lax
from jax.experimental import pallas as pl
from jax.experimental.pallas import tpu as pltpu

F32 = jnp.float32
BF16 = jnp.bfloat16

NORM_EPS = 1e-6
ROPE_THETA = 10000.0
GRID_W = 64
HEAD_DIM = 64
ATTN_HEADS = 8
ATTN_KV_HEADS = 2
ATTN_GROUP = ATTN_HEADS // ATTN_KV_HEADS
HGRN_HEADS = 8
HGRN_DIM = 64
HGRN_CHUNK = 32
ATTN_WIDTH = ATTN_HEADS * HEAD_DIM
KV_WIDTH = ATTN_KV_HEADS * HEAD_DIM
HGRN_WIDTH = HGRN_HEADS * HGRN_DIM
HGRN_PAIR = 2 * HGRN_DIM
HGRN_PAIRS = HGRN_HEADS // 2

V7X_LANES = 128
V7X_MXU_DIM = 256
V7X_VMEM_LIMIT_BYTES = 56 * 1024 * 1024

FFN_ROWS = 512
FFN_COLS = V7X_MXU_DIM
PROJ_ROWS = 1024
ATTN_Q_ROWS = 512
ATTN_KEY_SPLIT = 4
ATTN_SCORE_BOUND = 60.0
HGRN_ROWS = 256
HGRN_INTRA_ROWS = V7X_MXU_DIM // 2
MERGE_ROWS = 1024
WEIGHT_STAGE_ROWS = 128


def _compiler_params(semantics):
    return pltpu.CompilerParams(dimension_semantics=semantics, vmem_limit_bytes=V7X_VMEM_LIMIT_BYTES)


def _resident(shape):
    zeros = (0,) * len(shape)
    return pl.BlockSpec(shape, lambda *_: zeros, pipeline_mode=pl.Buffered(1))


def _rmsnorm(x, g):
    return x * lax.rsqrt(jnp.mean(x * x, axis=-1, keepdims=True) + NORM_EPS) * g


def _dot(a, b):
    return jnp.dot(a, b, preferred_element_type=F32)


def _dot_nt(a, b):
    return lax.dot_general(a, b, (((1,), (1,)), ((), ())), preferred_element_type=F32)


def _dot_tn(a, b):
    return lax.dot_general(a, b, (((0,), (0,)), ((), ())), preferred_element_type=F32)


def _split2(x):
    hi = x.astype(BF16)
    lo = (x - hi.astype(F32)).astype(BF16)
    return hi, lo


def _split3(x):
    hi = x.astype(BF16)
    r = x - hi.astype(F32)
    mid = r.astype(BF16)
    lo = (r - mid.astype(F32)).astype(BF16)
    return hi, mid, lo


def _group_mean(x, gmat):
    hi, lo = _split2(x)
    return _dot(hi, gmat) + _dot(lo, gmat)


def _load_as_bf16(src_rows, n_rows, dst, stage, sem):
    chunk = stage.shape[1]
    cols = dst.shape[1]
    assert n_rows % chunk == 0 and cols <= stage.shape[2]

    def copy(k):
        return pltpu.make_async_copy(src_rows(k * chunk, chunk), stage.at[k % 2, :, pl.ds(0, cols)], sem.at[k % 2])

    copy(0).start()
    for k in range(n_rows // chunk):
        if k + 1 < n_rows // chunk:
            copy(k + 1).start()
        copy(k).wait()
        dst[pl.ds(k * chunk, chunk), :] = stage[k % 2, :, :cols].astype(BF16)


def _ffn_kernel(x_ref, g_ref, wg_hbm, wu_hbm, wd_hbm, fg_ref, o_ref,
                wg_ref, wu_ref, wd_ref, stage_g, stage_u, stage_d, sem, *, final_norm):
    d_ff = wg_ref.shape[1]
    n_chunks = d_ff // FFN_COLS

    def chunk_copies(c):
        cols = pl.ds(c * FFN_COLS, FFN_COLS)
        slot = c % 2
        return (pltpu.make_async_copy(wg_hbm.at[:, cols], stage_g.at[slot], sem.at[0, slot]),
                pltpu.make_async_copy(wu_hbm.at[:, cols], stage_u.at[slot], sem.at[1, slot]),
                pltpu.make_async_copy(wd_hbm.at[cols, :], stage_d.at[slot], sem.at[2, slot]))

    def tile(stream_weights):
        if stream_weights:
            for cp in chunk_copies(0):
                cp.start()
        x = x_ref[...]
        xn = _rmsnorm(x, g_ref[...]).astype(BF16)
        acc = None
        for c in range(n_chunks):
            sl = slice(c * FFN_COLS, (c + 1) * FFN_COLS)
            if stream_weights:
                if c + 1 < n_chunks:
                    for cp in chunk_copies(c + 1):
                        cp.start()
                for cp in chunk_copies(c):
                    cp.wait()
                wg_ref[:, sl] = stage_g[c % 2].astype(BF16)
                wu_ref[:, sl] = stage_u[c % 2].astype(BF16)
                wd_ref[sl, :] = stage_d[c % 2].astype(BF16)
            gate = _dot(xn, wg_ref[:, sl])
            up = _dot(xn, wu_ref[:, sl])
            act = (jax.nn.silu(gate) * up).astype(BF16)
            part = _dot(act, wd_ref[sl, :])
            acc = part if acc is None else acc + part
        y = x + 0.5 * acc
        if final_norm:
            y = _rmsnorm(y, fg_ref[...])
        o_ref[...] = y

    pl.when(pl.program_id(0) == 0)(functools.partial(tile, True))
    pl.when(pl.program_id(0) != 0)(functools.partial(tile, False))


def _ffn(x, norm_g, w_gate, w_up, w_down, final_g, final_norm):
    t, d = x.shape
    d_ff = w_gate.shape[1]
    assert t % FFN_ROWS == 0 and d_ff % FFN_COLS == 0
    row = pl.BlockSpec((FFN_ROWS, d), lambda i: (i, 0))
    hbm = pl.BlockSpec(memory_space=pl.ANY)
    return pl.pallas_call(
        functools.partial(_ffn_kernel, final_norm=final_norm),
        grid=(t // FFN_ROWS,),
        in_specs=[row, _resident((1, d)), hbm, hbm, hbm, _resident((1, d))],
        out_specs=row,
        out_shape=jax.ShapeDtypeStruct((t, d), F32),
        scratch_shapes=[pltpu.VMEM((d, d_ff), BF16), pltpu.VMEM((d, d_ff), BF16), pltpu.VMEM((d_ff, d), BF16),
                        pltpu.VMEM((2, d, FFN_COLS), F32), pltpu.VMEM((2, d, FFN_COLS), F32),
                        pltpu.VMEM((2, FFN_COLS, d), F32), pltpu.SemaphoreType.DMA((3, 2))],
        compiler_params=_compiler_params(("arbitrary",)),
        name="ffn_final" if final_norm else "ffn",
    )(x, norm_g, w_gate, w_up, w_down, final_g)


def _rope(xn, cos_t, sin_t, width):
    half = HEAD_DIM // 2
    lane = lax.broadcasted_iota(jnp.int32, xn.shape, 1)
    first_half = (lane % HEAD_DIM) < half
    partner = jnp.where(first_half, pltpu.roll(xn, width - half, 1), pltpu.roll(xn, half, 1))
    return xn * cos_t + partner * sin_t


def _proj_kernel(x_ref, g_ref, w_hbm, qg_ref, kg_ref, gq_ref, gk_ref, cos_ref, sin_ref,
                 q_ref, k_ref, v_ref, hq_ref, ff_ref, fb_ref, hi_ref, hg_ref, w_ref, stage, sem):
    @pl.when((pl.program_id(0) == 0) & (pl.program_id(1) == 0))
    def _():
        n_in = w_ref.shape[1]
        _load_as_bf16(lambda r, n: w_hbm.at[0, pl.ds(r, n), pl.ds(0, n_in)], w_ref.shape[0], w_ref, stage, sem)

    h = _rmsnorm(x_ref[0], g_ref[...]).astype(BF16)

    def cols(start, width):
        return _dot(h, w_ref[:, start:start + width])

    cos2 = cos_ref[...]
    sin2 = sin_ref[...]
    off = 0
    q = cols(off, ATTN_WIDTH)
    off += ATTN_WIDTH
    reps = ATTN_WIDTH // V7X_LANES
    qn = q * lax.rsqrt(_group_mean(q * q, gq_ref[...]) + NORM_EPS) * qg_ref[...]
    qr = _rope(qn, jnp.concatenate([cos2] * reps, axis=1), jnp.concatenate([sin2] * reps, axis=1),
               ATTN_WIDTH) * (HEAD_DIM ** -0.5)
    for hd in range(ATTN_HEADS):
        q_ref[0, hd] = qr[:, hd * HEAD_DIM:(hd + 1) * HEAD_DIM].astype(BF16)
    k = cols(off, KV_WIDTH)
    off += KV_WIDTH
    kn = k * lax.rsqrt(_group_mean(k * k, gk_ref[...]) + NORM_EPS) * kg_ref[...]
    kr = _rope(kn, cos2, sin2, KV_WIDTH)
    v = cols(off, KV_WIDTH)
    off += KV_WIDTH
    lane = lax.broadcasted_iota(jnp.int32, v.shape, 1)
    for hd in range(ATTN_KV_HEADS):
        k_ref[0, hd] = kr[:, hd * HEAD_DIM:(hd + 1) * HEAD_DIM].astype(BF16)
        v_hd = v if hd == 0 else pltpu.roll(v, KV_WIDTH - hd * HEAD_DIM, 1)
        v_ref[0, hd] = jnp.where(lane < HEAD_DIM, v_hd, 1.0).T.astype(BF16)
    hq_ref[0] = jax.nn.silu(cols(off, HGRN_WIDTH))
    off += HGRN_WIDTH
    ff_ref[0] = cols(off, HGRN_WIDTH)
    off += HGRN_WIDTH
    fb_ref[0] = cols(off, HGRN_WIDTH)
    off += HGRN_WIDTH
    hi_ref[0] = cols(off, HGRN_WIDTH).astype(BF16)
    off += HGRN_WIDTH
    hg_ref[0] = jax.nn.silu(cols(off, HGRN_WIDTH))


def _group_matrix(width, group):
    idx = np.arange(width) // group
    return jnp.asarray((idx[:, None] == idx[None, :]).astype(np.float32) / group, dtype=BF16)


def _rope_tables(seq_len):
    f32 = np.float32
    rows = seq_len // GRID_W
    row = np.repeat(np.arange(rows, dtype=f32), GRID_W)
    col = np.tile(np.arange(GRID_W, dtype=f32), rows)
    n_freq = HEAD_DIM // 4
    inv = (f32(ROPE_THETA) ** (-np.arange(n_freq, dtype=f32) / f32(n_freq))).astype(f32)
    ang = np.concatenate([row[:, None] * inv, col[:, None] * inv], axis=-1)
    cos, sin = np.cos(ang).astype(f32), np.sin(ang).astype(f32)
    cos2 = np.concatenate([cos, cos] * 2, axis=-1)
    sin2 = np.concatenate([-sin, sin] * 2, axis=-1)
    return jnp.asarray(cos2), jnp.asarray(sin2)


def _in_proj(x1, norm_g, w_in, q_g, k_g):
    b, s, d = x1.shape
    n_in = ATTN_WIDTH + 2 * KV_WIDTH + 5 * HGRN_WIDTH
    assert s % PROJ_ROWS == 0 and w_in.shape[2] == n_in + 2 * d
    cos2, sin2 = _rope_tables(s)
    tm = PROJ_ROWS
    tok = lambda width: pl.BlockSpec((1, tm, width), lambda bi, i: (bi, i, 0))
    heads = lambda n, width=HEAD_DIM: pl.BlockSpec((1, n, tm, width), lambda bi, i: (bi, 0, i, 0))
    pos = pl.BlockSpec((tm, V7X_LANES), lambda bi, i: (i, 0))
    f32_tok = lambda width: jax.ShapeDtypeStruct((b, s, width), F32)
    out_shape = (
        jax.ShapeDtypeStruct((b, ATTN_HEADS, s, HEAD_DIM), BF16),
        jax.ShapeDtypeStruct((b, ATTN_KV_HEADS, s, HEAD_DIM), BF16),
        jax.ShapeDtypeStruct((b, ATTN_KV_HEADS, V7X_LANES, s), BF16),
        f32_tok(HGRN_WIDTH), f32_tok(HGRN_WIDTH), f32_tok(HGRN_WIDTH),
        jax.ShapeDtypeStruct((b, s, HGRN_WIDTH), BF16),
        f32_tok(HGRN_WIDTH),
    )
    vt_spec = pl.BlockSpec((1, ATTN_KV_HEADS, V7X_LANES, tm), lambda bi, i: (bi, 0, 0, i))
    out_specs = (heads(ATTN_HEADS), heads(ATTN_KV_HEADS), vt_spec,
                 tok(HGRN_WIDTH), tok(HGRN_WIDTH), tok(HGRN_WIDTH), tok(HGRN_WIDTH), tok(HGRN_WIDTH))
    return pl.pallas_call(
        _proj_kernel,
        grid=(b, s // tm),
        in_specs=[tok(d), _resident((1, d)), pl.BlockSpec(memory_space=pl.ANY),
                  _resident((1, ATTN_WIDTH)), _resident((1, KV_WIDTH)),
                  _resident((ATTN_WIDTH, ATTN_WIDTH)), _resident((KV_WIDTH, KV_WIDTH)), pos, pos],
        out_specs=out_specs,
        out_shape=out_shape,
        scratch_shapes=[pltpu.VMEM((d, n_in), BF16), pltpu.VMEM((2, WEIGHT_STAGE_ROWS, n_in), F32),
                        pltpu.SemaphoreType.DMA((2,))],
        compiler_params=_compiler_params(("arbitrary", "arbitrary")),
        name="in_proj",
    )(x1, norm_g, w_in, jnp.tile(q_g, (1, ATTN_HEADS)), jnp.tile(k_g, (1, ATTN_KV_HEADS)),
      _group_matrix(ATTN_WIDTH, HEAD_DIM), _group_matrix(KV_WIDTH, HEAD_DIM), cos2, sin2)


def _attn_kernel(bounded_ref, q_ref, k_ref, vt_ref, o_ref):
    tq = q_ref.shape[2]
    s_len = k_ref.shape[2]
    q = q_ref[0].reshape(ATTN_GROUP * tq, HEAD_DIM)

    def finish(accs):
        lane = lax.broadcasted_iota(jnp.int32, (tq, V7X_LANES), 1)
        pairs = []
        for r in range(0, ATTN_GROUP, 2):
            lo, hi = accs[r].T, accs[r + 1].T
            lo_sw, hi_sw = pltpu.roll(lo, HEAD_DIM, 1), pltpu.roll(hi, HEAD_DIM, 1)
            pairs.append(jnp.where(lane < HEAD_DIM, lo / lo_sw, hi_sw / hi))
        o_ref[0] = jnp.concatenate(pairs, axis=1).astype(BF16)

    def attend(shifted):
        kc = s_len // ATTN_KEY_SPLIT
        chunks = [slice(c * kc, (c + 1) * kc) for c in range(ATTN_KEY_SPLIT)]
        if shifted:
            s_t = _dot_nt(k_ref[0, 0], q)
            p_full = jnp.exp(s_t - jnp.max(s_t, axis=0, keepdims=True)).astype(BF16)
            p_chunks = [p_full[ks] for ks in chunks]
        else:
            p_chunks = [jnp.exp(_dot_nt(k_ref[0, 0, ks, :], q)).astype(BF16) for ks in chunks]
        accs = [None] * ATTN_GROUP
        for ks, p_t in zip(chunks, p_chunks):
            vt_c = vt_ref[0, 0, :, ks]
            for r in range(ATTN_GROUP):
                part = _dot(vt_c, p_t[:, r * tq:(r + 1) * tq])
                accs[r] = part if accs[r] is None else accs[r] + part
        finish(accs)

    pl.when(bounded_ref[0] == 1)(functools.partial(attend, False))
    pl.when(bounded_ref[0] != 1)(functools.partial(attend, True))


def _attention(bounded, q, k, v):
    b, _, s, _ = q.shape
    tq = ATTN_Q_ROWS
    assert s % tq == 0 and s % (ATTN_KEY_SPLIT * V7X_LANES) == 0
    k_spec = pl.BlockSpec((1, 1, s, HEAD_DIM), lambda bi, g, i, *_: (bi, g, 0, 0))
    vt_spec = pl.BlockSpec((1, 1, V7X_LANES, s), lambda bi, g, i, *_: (bi, g, 0, 0))
    grid_spec = pltpu.PrefetchScalarGridSpec(
        num_scalar_prefetch=1,
        grid=(b, ATTN_KV_HEADS, s // tq),
        in_specs=[pl.BlockSpec((1, ATTN_GROUP, tq, HEAD_DIM), lambda bi, g, i, *_: (bi, g, i, 0)),
                  k_spec, vt_spec],
        out_specs=pl.BlockSpec((1, tq, ATTN_GROUP * HEAD_DIM), lambda bi, g, i, *_: (bi, i, g)),
    )
    return pl.pallas_call(
        _attn_kernel,
        grid_spec=grid_spec,
        out_shape=jax.ShapeDtypeStruct((b, s, ATTN_WIDTH), BF16),
        compiler_params=_compiler_params(("parallel", "parallel", "parallel")),
        name="attention",
    )(bounded, q, k, v)


def _scores_bounded(q_g, k_g):
    bound = (HEAD_DIM ** 0.5) * jnp.max(jnp.abs(q_g)) * jnp.max(jnp.abs(k_g))
    return (bound <= ATTN_SCORE_BOUND).astype(jnp.int32).reshape(1)


def _hgrn_prepare(qt, f_logit, lb, reverse):
    rows = qt.shape[0]
    c = HGRN_CHUNK
    n_chunks = rows // c
    f = lb + (1.0 - lb) * jax.nn.sigmoid(f_logit)
    kk = 1.0 - f
    log_f = jnp.log(f)
    r_i = lax.broadcasted_iota(jnp.int32, (rows, rows), 0)
    c_i = lax.broadcasted_iota(jnp.int32, (rows, rows), 1)
    same_chunk = (r_i // c) == (c_i // c)
    causal = same_chunk & ((c_i >= r_i) if reverse else (c_i <= r_i))
    tri = jnp.where(causal, 1.0, 0.0).astype(BF16)
    hi, mid, lo = _split3(log_f)
    bcum = _dot(tri, hi) + _dot(tri, mid) + _dot(tri, lo)
    b3 = bcum.reshape(n_chunks, c, bcum.shape[-1])
    mid_row = c // 2 if reverse else c // 2 - 1
    last_row = 0 if reverse else c - 1
    b_mid = b3[:, mid_row:mid_row + 1, :]
    b_last = b3[:, last_row:last_row + 1, :]
    q3 = qt.reshape(b3.shape)
    k3 = kk.reshape(b3.shape)
    d_mid = b3 - b_mid
    q_mid = q3 * jnp.exp(d_mid)
    k_mid = k3 * jnp.exp(-d_mid)
    k_end = k_mid * jnp.exp(b_last - b_mid)
    q_in = q_mid * jnp.exp(b_mid)
    flat = lambda a: a.reshape(bcum.shape).astype(BF16)
    sub = HGRN_INTRA_ROWS
    causal_sub = jnp.concatenate([causal[:sub, :sub]] * 2, axis=1)
    return dict(causal_sub=causal_sub, q_mid=flat(q_mid), k_mid=flat(k_mid), k_end=flat(k_end), q_in=flat(q_in),
                decay=jnp.exp(b_last))


def _pair_slice(p):
    return slice(p * HGRN_PAIR, (p + 1) * HGRN_PAIR)


def _hgrn_intra(ops, v, p):
    rows = v.shape[0]
    sub = HGRN_INTRA_ROWS
    lower_head = lax.broadcasted_iota(jnp.int32, (sub, HGRN_PAIR), 1) < HGRN_DIM
    zero_bf = jnp.zeros((), BF16)
    stack = lambda x: jnp.concatenate([jnp.where(lower_head, x, zero_bf), jnp.where(lower_head, zero_bf, x)], axis=0)
    causal = ops["causal_sub"]
    sl = _pair_slice(p)
    outs = []
    for h in range(rows // sub):
        rs = slice(h * sub, (h + 1) * sub)
        scores = _dot_nt(ops["q_mid"][rs, sl], stack(ops["k_mid"][rs, sl]))
        scores = jnp.where(causal, scores, 0.0).astype(BF16)
        outs.append(_dot(scores, stack(v[rs, sl])))
    return jnp.concatenate(outs, axis=0)


def _hgrn_contribs(ops, v):
    c = HGRN_CHUNK
    n_chunks = v.shape[0] // c
    return [[_dot_tn(v[n * c:(n + 1) * c, _pair_slice(p)], ops["k_end"][n * c:(n + 1) * c, _pair_slice(p)])
             for n in range(n_chunks)] for p in range(HGRN_PAIRS)]


def _hgrn_scan(ops, contribs, state_ref, reverse):
    n_chunks = len(contribs[0])
    sr = lax.broadcasted_iota(jnp.int32, (HGRN_PAIR, HGRN_PAIR), 0)
    sc = lax.broadcasted_iota(jnp.int32, (HGRN_PAIR, HGRN_PAIR), 1)
    same_head = (sr // HGRN_DIM) == (sc // HGRN_DIM)
    order = range(n_chunks - 1, -1, -1) if reverse else range(n_chunks)
    states = [[None] * n_chunks for _ in range(HGRN_PAIRS)]
    for p in range(HGRN_PAIRS):
        st = state_ref[p]
        for n in order:
            states[p][n] = st.astype(BF16)
            st = ops["decay"][n, :, _pair_slice(p)] * st + jnp.where(same_head, contribs[p][n], 0.0)
        state_ref[p] = st
    return states


def _hgrn_inter(ops, states, p):
    c = HGRN_CHUNK
    parts = [_dot_nt(ops["q_in"][n * c:(n + 1) * c, _pair_slice(p)], st) for n, st in enumerate(states[p])]
    return jnp.concatenate(parts, axis=0)


def _hgrn_kernel(lbl_ref, qf_ref, ff_ref, vf_ref, qb_ref, fb_ref, vb_ref, of_ref, ob_ref,
                 st_f_ref, st_b_ref):
    @pl.when(pl.program_id(1) == 0)
    def _():
        st_f_ref[...] = jnp.zeros_like(st_f_ref)
        st_b_ref[...] = jnp.zeros_like(st_b_ref)

    logits = lbl_ref[...]
    e = jnp.exp(logits - jnp.max(logits, axis=1, keepdims=True))
    lb = e[:, 0, :] / jnp.sum(e, axis=1)
    v_f, v_b = vf_ref[0], vb_ref[0]
    ops_f = _hgrn_prepare(qf_ref[0], ff_ref[0], lb[0:1], False)
    ops_b = _hgrn_prepare(qb_ref[0], fb_ref[0], lb[1:2], True)
    con_f = _hgrn_contribs(ops_f, v_f)
    con_b = _hgrn_contribs(ops_b, v_b)
    states_f = _hgrn_scan(ops_f, con_f, st_f_ref, False)
    states_b = _hgrn_scan(ops_b, con_b, st_b_ref, True)
    pairs = range(HGRN_PAIRS)
    inter_f = [_hgrn_inter(ops_f, states_f, p) for p in pairs]
    inter_b = [_hgrn_inter(ops_b, states_b, p) for p in pairs]
    intra_f = [_hgrn_intra(ops_f, v_f, p) for p in pairs]
    intra_b = [_hgrn_intra(ops_b, v_b, p) for p in pairs]
    of_ref[0] = jnp.concatenate([a + b for a, b in zip(intra_f, inter_f)], axis=1)
    ob_ref[0] = jnp.concatenate([a + b for a, b in zip(intra_b, inter_b)], axis=1)


def _hgrn(lb_logits, hq, hff, hfb, hi):
    b, s, w = hq.shape
    r = HGRN_ROWS
    assert s % r == 0 and r % HGRN_INTRA_ROWS == 0 and HGRN_INTRA_ROWS % HGRN_CHUNK == 0
    nb = s // r
    fwd = pl.BlockSpec((1, r, w), lambda bi, j: (bi, j, 0))
    bwd = pl.BlockSpec((1, r, w), lambda bi, j: (bi, nb - 1 - j, 0))
    state = pltpu.VMEM((HGRN_PAIRS, HGRN_PAIR, HGRN_PAIR), F32)
    return pl.pallas_call(
        _hgrn_kernel,
        grid=(b, nb),
        in_specs=[_resident(lb_logits.shape), fwd, fwd, fwd, bwd, bwd, bwd],
        out_specs=(fwd, bwd),
        out_shape=(jax.ShapeDtypeStruct((b, s, w), F32), jax.ShapeDtypeStruct((b, s, w), F32)),
        scratch_shapes=[state, state],
        compiler_params=_compiler_params(("parallel", "arbitrary")),
        name="hgrn",
    )(lb_logits, hq, hff, hi, hq, hfb, hi)


def _merge_kernel(x_ref, ya_ref, of_ref, ob_ref, hg_ref, mg_ref, win_hbm, ng_ref, gm_ref,
                  wa_hbm, wb_hbm, wo_hbm, o_ref, wg_ref, wa_ref, wb_ref, wo_ref, stage, sem):
    @pl.when(pl.program_id(0) == 0)
    def _():
        n_gate = wg_ref.shape[1]
        n_mix = win_hbm.shape[2] - n_gate
        _load_as_bf16(lambda r, n: win_hbm.at[0, pl.ds(r, n), pl.ds(n_mix, n_gate)], wg_ref.shape[0], wg_ref, stage, sem)
        for hbm, ref in ((wa_hbm, wa_ref), (wb_hbm, wb_ref), (wo_hbm, wo_ref)):
            _load_as_bf16(lambda r, n, hbm=hbm: hbm.at[0, pl.ds(r, n), :], ref.shape[0], ref, stage, sem)

    x = x_ref[...]
    d = x.shape[-1]
    h = _rmsnorm(x, mg_ref[...]).astype(BF16)
    gate_a = jax.nn.sigmoid(_dot(h, wg_ref[:, :d]))
    gate_b = jax.nn.sigmoid(_dot(h, wg_ref[:, d:]))
    o = of_ref[...] + ob_ref[...]
    on = o * lax.rsqrt(_group_mean(o * o, gm_ref[...]) + NORM_EPS) * ng_ref[...]
    yb = (on * hg_ref[...]).astype(BF16)
    merged = gate_a * _dot(ya_ref[...], wa_ref[...]) + gate_b * _dot(yb, wb_ref[...])
    o_ref[...] = x + _dot(merged.astype(BF16), wo_ref[...])


def _merge(x1, y_attn, o_f, o_b, hg, mix_g, w_in, norm_g, w_a, w_b, w_out):
    t, d = x1.shape
    tm = MERGE_ROWS
    assert t % tm == 0
    row = lambda width: pl.BlockSpec((tm, width), lambda i: (i, 0))
    hbm = pl.BlockSpec(memory_space=pl.ANY)
    return pl.pallas_call(
        _merge_kernel,
        grid=(t // tm,),
        in_specs=[row(d), row(ATTN_WIDTH), row(HGRN_WIDTH), row(HGRN_WIDTH), row(HGRN_WIDTH),
                  _resident((1, d)), hbm,
                  _resident((1, HGRN_WIDTH)), _resident((HGRN_WIDTH, HGRN_WIDTH)), hbm, hbm, hbm],
        out_specs=row(d),
        out_shape=jax.ShapeDtypeStruct((t, d), F32),
        scratch_shapes=[pltpu.VMEM((d, 2 * d), BF16), pltpu.VMEM((ATTN_WIDTH, d), BF16),
                        pltpu.VMEM((HGRN_WIDTH, d), BF16), pltpu.VMEM((d, d), BF16),
                        pltpu.VMEM((2, WEIGHT_STAGE_ROWS, 2 * d), F32), pltpu.SemaphoreType.DMA((2,))],
        compiler_params=_compiler_params(("arbitrary",)),
        name="merge",
    )(x1, y_attn, o_f, o_b, hg, mix_g, w_in, norm_g, _group_matrix(HGRN_WIDTH, HGRN_DIM), w_a, w_b, w_out)


def kernel(x, ffn1_norm_g, ffn1_w_gate, ffn1_w_up, ffn1_w_down, mix_norm_g, w_in, q_norm_g, k_norm_g,
           hgrn_lb_logits, hgrn_out_norm_g, w_branch_attn, w_branch_hgrn, w_out, ffn2_norm_g,
           ffn2_w_gate, ffn2_w_up, ffn2_w_down, final_norm_g):
    b, s, d = x.shape
    t = b * s
    depth = w_in.shape[0]
    assert depth == 1 and hgrn_lb_logits.shape[1] == depth + 1
    final_g = final_norm_g.reshape(1, d)
    x1 = _ffn(x.reshape(t, d), ffn1_norm_g, ffn1_w_gate[0], ffn1_w_up[0], ffn1_w_down[0], final_g, False)
    q, k, v, hq, hff, hfb, hi, hg = _in_proj(x1.reshape(b, s, d), mix_norm_g, w_in, q_norm_g, k_norm_g)
    y_attn = _attention(_scores_bounded(q_norm_g, k_norm_g), q, k, v)
    o_f, o_b = _hgrn(hgrn_lb_logits, hq, hff, hfb, hi)
    flat = lambda a: a.reshape(t, a.shape[-1])
    x2 = _merge(x1, flat(y_attn), flat(o_f), flat(o_b), flat(hg), mix_norm_g, w_in, hgrn_out_norm_g,
                w_branch_attn, w_branch_hgrn, w_out)
    out = _ffn(x2, ffn2_norm_g, ffn2_w_gate[0], ffn2_w_up[0], ffn2_w_down[0], final_g, True)
    return out.reshape(b, s, d)
```

```python
import functools

import jax
import jax.numpy as jnp
import numpy as np
from jax import lax
from jax.experimental import pallas as pl
from jax.experimental.pallas import tpu as pltpu

F32 = jnp.float32
BF16 = jnp.bfloat16

NORM_EPS = 1e-6
ROPE_THETA = 10000.0
GRID_W = 64
HEAD_DIM = 64
ATTN_HEADS = 8
ATTN_KV_HEADS = 2
ATTN_GROUP = ATTN_HEADS // ATTN_KV_HEADS
HGRN_HEADS = 8
HGRN_DIM = 64
HGRN_CHUNK = 32
ATTN_WIDTH = ATTN_HEADS * HEAD_DIM
KV_WIDTH = ATTN_KV_HEADS * HEAD_DIM
HGRN_WIDTH = HGRN_HEADS * HGRN_DIM
HGRN_PAIR = 2 * HGRN_DIM
HGRN_PAIRS = HGRN_HEADS // 2

V7X_LANES = 128
V7X_MXU_DIM = 256
V7X_VMEM_LIMIT_BYTES = 56 * 1024 * 1024

FFN_ROWS = 512
FFN_COLS = V7X_MXU_DIM
PROJ_ROWS = 1024
ATTN_Q_ROWS = 512
ATTN_KEY_SPLIT = 4
ATTN_SCORE_BOUND = 60.0
HGRN_ROWS = 256
HGRN_INTRA_ROWS = V7X_MXU_DIM // 2
MERGE_ROWS = 1024
WEIGHT_STAGE_ROWS = 64
WEIGHT_STAGE_SLOTS = 4


def _compiler_params(semantics):
    return pltpu.CompilerParams(dimension_semantics=semantics, vmem_limit_bytes=V7X_VMEM_LIMIT_BYTES)


def _resident(shape):
    zeros = (0,) * len(shape)
    return pl.BlockSpec(shape, lambda *_: zeros, pipeline_mode=pl.Buffered(1))


def _rmsnorm(x, g):
    return x * lax.rsqrt(jnp.mean(x * x, axis=-1, keepdims=True) + NORM_EPS) * g


def _dot(a, b):
    return jnp.dot(a, b, preferred_element_type=F32)


def _dot_nt(a, b):
    return lax.dot_general(a, b, (((1,), (1,)), ((), ())), preferred_element_type=F32)


def _dot_tn(a, b):
    return lax.dot_general(a, b, (((0,), (0,)), ((), ())), preferred_element_type=F32)


def _split2(x):
    hi = x.astype(BF16)
    lo = (x - hi.astype(F32)).astype(BF16)
    return hi, lo


def _split3(x):
    hi = x.astype(BF16)
    r = x - hi.astype(F32)
    mid = r.astype(BF16)
    lo = (r - mid.astype(F32)).astype(BF16)
    return hi, mid, lo


def _group_mean(x, gmat):
    hi, lo = _split2(x)
    return _dot(hi, gmat) + _dot(lo, gmat)


def _load_as_bf16(src_rows, n_rows, dst, stage, sem):
    depth, chunk, _ = stage.shape
    cols = dst.shape[1]
    n = n_rows // chunk
    assert n_rows % chunk == 0 and cols <= stage.shape[2]

    def copy(k):
        slot = k % depth
        return pltpu.make_async_copy(src_rows(k * chunk, chunk), stage.at[slot, :, pl.ds(0, cols)], sem.at[slot])

    for k in range(min(depth - 1, n)):
        copy(k).start()
    for k in range(n):
        if k + depth - 1 < n:
            copy(k + depth - 1).start()
        copy(k).wait()
        dst[pl.ds(k * chunk, chunk), :] = stage[k % depth, :, :cols].astype(BF16)


def _ffn_kernel(x_ref, g_ref, wg_hbm, wu_hbm, wd_hbm, fg_ref, o_ref,
                wg_ref, wu_ref, wd_ref, stage_g, stage_u, stage_d, sem, *, final_norm):
    d_ff = wg_ref.shape[1]
    n_chunks = d_ff // FFN_COLS

    def chunk_copies(c):
        cols = pl.ds(c * FFN_COLS, FFN_COLS)
        slot = c % 2
        return (pltpu.make_async_copy(wg_hbm.at[:, cols], stage_g.at[slot], sem.at[0, slot]),
                pltpu.make_async_copy(wu_hbm.at[:, cols], stage_u.at[slot], sem.at[1, slot]),
                pltpu.make_async_copy(wd_hbm.at[cols, :], stage_d.at[slot], sem.at[2, slot]))

    def tile(stream_weights):
        if stream_weights:
            for cp in chunk_copies(0):
                cp.start()
        x = x_ref[...]
        xn = _rmsnorm(x, g_ref[...]).astype(BF16)
        acc = None
        for c in range(n_chunks):
            sl = slice(c * FFN_COLS, (c + 1) * FFN_COLS)
            if stream_weights:
                if c + 1 < n_chunks:
                    for cp in chunk_copies(c + 1):
                        cp.start()
                for cp in chunk_copies(c):
                    cp.wait()
                wg_ref[:, sl] = stage_g[c % 2].astype(BF16)
                wu_ref[:, sl] = stage_u[c % 2].astype(BF16)
                wd_ref[sl, :] = stage_d[c % 2].astype(BF16)
            gate = _dot(xn, wg_ref[:, sl])
            up = _dot(xn, wu_ref[:, sl])
            act = (jax.nn.silu(gate) * up).astype(BF16)
            part = _dot(act, wd_ref[sl, :])
            acc = part if acc is None else acc + part
        y = x + 0.5 * acc
        if final_norm:
            y = _rmsnorm(y, fg_ref[...])
        o_ref[...] = y

    pl.when(pl.program_id(0) == 0)(functools.partial(tile, True))
    pl.when(pl.program_id(0) != 0)(functools.partial(tile, False))


def _ffn(x, norm_g, w_gate, w_up, w_down, final_g, final_norm):
    t, d = x.shape
    d_ff = w_gate.shape[1]
    assert t % FFN_ROWS == 0 and d_ff % FFN_COLS == 0
    row = pl.BlockSpec((FFN_ROWS, d), lambda i: (i, 0))
    hbm = pl.BlockSpec(memory_space=pl.ANY)
    return pl.pallas_call(
        functools.partial(_ffn_kernel, final_norm=final_norm),
        grid=(t // FFN_ROWS,),
        in_specs=[row, _resident((1, d)), hbm, hbm, hbm, _resident((1, d))],
        out_specs=row,
        out_shape=jax.ShapeDtypeStruct((t, d), F32),
        scratch_shapes=[pltpu.VMEM((d, d_ff), BF16), pltpu.VMEM((d, d_ff), BF16), pltpu.VMEM((d_ff, d), BF16),
                        pltpu.VMEM((2, d, FFN_COLS), F32), pltpu.VMEM((2, d, FFN_COLS), F32),
                        pltpu.VMEM((2, FFN_COLS, d), F32), pltpu.SemaphoreType.DMA((3, 2))],
        compiler_params=_compiler_params(("arbitrary",)),
        name="ffn_final" if final_norm else "ffn",
    )(x, norm_g, w_gate, w_up, w_down, final_g)


def _rope(xn, cos_t, sin_t, width):
    half = HEAD_DIM // 2
    lane = lax.broadcasted_iota(jnp.int32, xn.shape, 1)
    first_half = (lane % HEAD_DIM) < half
    partner = jnp.where(first_half, pltpu.roll(xn, width - half, 1), pltpu.roll(xn, half, 1))
    return xn * cos_t + partner * sin_t


def _proj_kernel(x_ref, g_ref, w_hbm, qg_ref, kg_ref, gq_ref, gk_ref, cos_ref, sin_ref,
                 q_ref, k_ref, v_ref, hq_ref, ff_ref, fb_ref, hi_ref, hg_ref, w_ref, stage, sem):
    @pl.when((pl.program_id(0) == 0) & (pl.program_id(1) == 0))
    def _():
        n_in = w_ref.shape[1]
        _load_as_bf16(lambda r, n: w_hbm.at[0, pl.ds(r, n), pl.ds(0, n_in)], w_ref.shape[0], w_ref, stage, sem)

    h = _rmsnorm(x_ref[0], g_ref[...]).astype(BF16)

    def cols(start, width):
        return _dot(h, w_ref[:, start:start + width])

    cos2 = cos_ref[...]
    sin2 = sin_ref[...]
    off = 0
    q = cols(off, ATTN_WIDTH)
    off += ATTN_WIDTH
    reps = ATTN_WIDTH // V7X_LANES
    qn = q * lax.rsqrt(_group_mean(q * q, gq_ref[...]) + NORM_EPS) * qg_ref[...]
    qr = _rope(qn, jnp.concatenate([cos2] * reps, axis=1), jnp.concatenate([sin2] * reps, axis=1),
               ATTN_WIDTH) * (HEAD_DIM ** -0.5)
    for hd in range(ATTN_HEADS):
        q_ref[0, hd] = qr[:, hd * HEAD_DIM:(hd + 1) * HEAD_DIM].astype(BF16)
    k = cols(off, KV_WIDTH)
    off += KV_WIDTH
    kn = k * lax.rsqrt(_group_mean(k * k, gk_ref[...]) + NORM_EPS) * kg_ref[...]
    kr = _rope(kn, cos2, sin2, KV_WIDTH)
    v = cols(off, KV_WIDTH)
    off += KV_WIDTH
    lane = lax.broadcasted_iota(jnp.int32, v.shape, 1)
    for hd in range(ATTN_KV_HEADS):
        k_ref[0, hd] = kr[:, hd * HEAD_DIM:(hd + 1) * HEAD_DIM].astype(BF16)
        v_hd = v if hd == 0 else pltpu.roll(v, KV_WIDTH - hd * HEAD_DIM, 1)
        v_ref[0, hd] = jnp.where(lane < HEAD_DIM, v_hd, 1.0).T.astype(BF16)
    hq_ref[0] = jax.nn.silu(cols(off, HGRN_WIDTH))
    off += HGRN_WIDTH
    ff_ref[0] = cols(off, HGRN_WIDTH)
    off += HGRN_WIDTH
    fb_ref[0] = cols(off, HGRN_WIDTH)
    off += HGRN_WIDTH
    hi_ref[0] = cols(off, HGRN_WIDTH).astype(BF16)
    off += HGRN_WIDTH
    hg_ref[0] = jax.nn.silu(cols(off, HGRN_WIDTH))


def _group_matrix(width, group):
    idx = np.arange(width) // group
    return jnp.asarray((idx[:, None] == idx[None, :]).astype(np.float32) / group, dtype=BF16)


def _rope_tables(seq_len):
    f32 = np.float32
    rows = seq_len // GRID_W
    row = np.repeat(np.arange(rows, dtype=f32), GRID_W)
    col = np.tile(np.arange(GRID_W, dtype=f32), rows)
    n_freq = HEAD_DIM // 4
    inv = (f32(ROPE_THETA) ** (-np.arange(n_freq, dtype=f32) / f32(n_freq))).astype(f32)
    ang = np.concatenate([row[:, None] * inv, col[:, None] * inv], axis=-1)
    cos, sin = np.cos(ang).astype(f32), np.sin(ang).astype(f32)
    cos2 = np.concatenate([cos, cos] * 2, axis=-1)
    sin2 = np.concatenate([-sin, sin] * 2, axis=-1)
    return jnp.asarray(cos2), jnp.asarray(sin2)


def _in_proj(x1, norm_g, w_in, q_g, k_g):
    b, s, d = x1.shape
    n_in = ATTN_WIDTH + 2 * KV_WIDTH + 5 * HGRN_WIDTH
    assert s % PROJ_ROWS == 0 and w_in.shape[2] == n_in + 2 * d
    cos2, sin2 = _rope_tables(s)
    tm = PROJ_ROWS
    tok = lambda width: pl.BlockSpec((1, tm, width), lambda bi, i: (bi, i, 0))
    heads = lambda n, width=HEAD_DIM: pl.BlockSpec((1, n, tm, width), lambda bi, i: (bi, 0, i, 0))
    pos = pl.BlockSpec((tm, V7X_LANES), lambda bi, i: (i, 0))
    f32_tok = lambda width: jax.ShapeDtypeStruct((b, s, width), F32)
    out_shape = (
        jax.ShapeDtypeStruct((b, ATTN_HEADS, s, HEAD_DIM), BF16),
        jax.ShapeDtypeStruct((b, ATTN_KV_HEADS, s, HEAD_DIM), BF16),
        jax.ShapeDtypeStruct((b, ATTN_KV_HEADS, V7X_LANES, s), BF16),
        f32_tok(HGRN_WIDTH), f32_tok(HGRN_WIDTH), f32_tok(HGRN_WIDTH),
        jax.ShapeDtypeStruct((b, s, HGRN_WIDTH), BF16),
        f32_tok(HGRN_WIDTH),
    )
    vt_spec = pl.BlockSpec((1, ATTN_KV_HEADS, V7X_LANES, tm), lambda bi, i: (bi, 0, 0, i))
    out_specs = (heads(ATTN_HEADS), heads(ATTN_KV_HEADS), vt_spec,
                 tok(HGRN_WIDTH), tok(HGRN_WIDTH), tok(HGRN_WIDTH), tok(HGRN_WIDTH), tok(HGRN_WIDTH))
    return pl.pallas_call(
        _proj_kernel,
        grid=(b, s // tm),
        in_specs=[tok(d), _resident((1, d)), pl.BlockSpec(memory_space=pl.ANY),
                  _resident((1, ATTN_WIDTH)), _resident((1, KV_WIDTH)),
                  _resident((ATTN_WIDTH, ATTN_WIDTH)), _resident((KV_WIDTH, KV_WIDTH)), pos, pos],
        out_specs=out_specs,
        out_shape=out_shape,
        scratch_shapes=[pltpu.VMEM((d, n_in), BF16), pltpu.VMEM((WEIGHT_STAGE_SLOTS, WEIGHT_STAGE_ROWS, n_in), F32),
                        pltpu.SemaphoreType.DMA((WEIGHT_STAGE_SLOTS,))],
        compiler_params=_compiler_params(("arbitrary", "arbitrary")),
        name="in_proj",
    )(x1, norm_g, w_in, jnp.tile(q_g, (1, ATTN_HEADS)), jnp.tile(k_g, (1, ATTN_KV_HEADS)),
      _group_matrix(ATTN_WIDTH, HEAD_DIM), _group_matrix(KV_WIDTH, HEAD_DIM), cos2, sin2)


def _attn_kernel(bounded_ref, q_ref, k_ref, vt_ref, o_ref):
    tq = q_ref.shape[2]
    s_len = k_ref.shape[2]
    q = q_ref[0].reshape(ATTN_GROUP * tq, HEAD_DIM)

    def finish(accs):
        norm = [a[:HEAD_DIM] * (1.0 / a[HEAD_DIM:HEAD_DIM + 1]) for a in accs]
        pairs = [jnp.concatenate(norm[r:r + 2], axis=0).T for r in range(0, ATTN_GROUP, 2)]
        o_ref[0] = jnp.concatenate(pairs, axis=1).astype(BF16)

    def attend(shifted):
        kc = s_len // ATTN_KEY_SPLIT
        chunks = [slice(c * kc, (c + 1) * kc) for c in range(ATTN_KEY_SPLIT)]
        if shifted:
            s_t = _dot_nt(k_ref[0, 0], q)
            p_full = jnp.exp(s_t - jnp.max(s_t, axis=0, keepdims=True)).astype(BF16)
            p_chunks = [p_full[ks] for ks in chunks]
        else:
            p_chunks = [jnp.exp(_dot_nt(k_ref[0, 0, ks, :], q)).astype(BF16) for ks in chunks]
        accs = [None] * ATTN_GROUP
        for ks, p_t in zip(chunks, p_chunks):
            vt_c = vt_ref[0, 0, :, ks]
            for r in range(ATTN_GROUP):
                part = _dot(vt_c, p_t[:, r * tq:(r + 1) * tq])
                accs[r] = part if accs[r] is None else accs[r] + part
        finish(accs)

    pl.when(bounded_ref[0] == 1)(functools.partial(attend, False))
    pl.when(bounded_ref[0] != 1)(functools.partial(attend, True))


def _attention(bounded, q, k, v):
    b, _, s, _ = q.shape
    tq = ATTN_Q_ROWS
    assert s % tq == 0 and s % (ATTN_KEY_SPLIT * V7X_LANES) == 0
    k_spec = pl.BlockSpec((1, 1, s, HEAD_DIM), lambda bi, g, i, *_: (bi, g, 0, 0))
    vt_spec = pl.BlockSpec((1, 1, V7X_LANES, s), lambda bi, g, i, *_: (bi, g, 0, 0))
    grid_spec = pltpu.PrefetchScalarGridSpec(
        num_scalar_prefetch=1,
        grid=(b, ATTN_KV_HEADS, s // tq),
        in_specs=[pl.BlockSpec((1, ATTN_GROUP, tq, HEAD_DIM), lambda bi, g, i, *_: (bi, g, i, 0)),
                  k_spec, vt_spec],
        out_specs=pl.BlockSpec((1, tq, ATTN_GROUP * HEAD_DIM), lambda bi, g, i, *_: (bi, i, g)),
    )
    return pl.pallas_call(
        _attn_kernel,
        grid_spec=grid_spec,
        out_shape=jax.ShapeDtypeStruct((b, s, ATTN_WIDTH), BF16),
        compiler_params=_compiler_params(("parallel", "parallel", "parallel")),
        name="attention",
    )(bounded, q, k, v)


def _scores_bounded(q_g, k_g):
    bound = (HEAD_DIM ** 0.5) * jnp.max(jnp.abs(q_g)) * jnp.max(jnp.abs(k_g))
    return (bound <= ATTN_SCORE_BOUND).astype(jnp.int32).reshape(1)


def _hgrn_prepare(qt, f_logit, lb, reverse):
    rows = qt.shape[0]
    c = HGRN_CHUNK
    n_chunks = rows // c
    f = lb + (1.0 - lb) * jax.nn.sigmoid(f_logit)
    kk = 1.0 - f
    log_f = jnp.log(f)
    r_i = lax.broadcasted_iota(jnp.int32, (rows, rows), 0)
    c_i = lax.broadcasted_iota(jnp.int32, (rows, rows), 1)
    same_chunk = (r_i // c) == (c_i // c)
    causal = same_chunk & ((c_i >= r_i) if reverse else (c_i <= r_i))
    tri = jnp.where(causal, 1.0, 0.0).astype(BF16)
    hi, mid, lo = _split3(log_f)
    bcum = _dot(tri, hi) + _dot(tri, mid) + _dot(tri, lo)
    b3 = bcum.reshape(n_chunks, c, bcum.shape[-1])
    mid_row = c // 2 if reverse else c // 2 - 1
    last_row = 0 if reverse else c - 1
    b_mid = b3[:, mid_row:mid_row + 1, :]
    b_last = b3[:, last_row:last_row + 1, :]
    q3 = qt.reshape(b3.shape)
    k3 = kk.reshape(b3.shape)
    d_mid = b3 - b_mid
    q_mid = q3 * jnp.exp(d_mid)
    k_mid = k3 * jnp.exp(-d_mid)
    k_end = k_mid * jnp.exp(b_last - b_mid)
    q_in = q_mid * jnp.exp(b_mid)
    flat = lambda a: a.reshape(bcum.shape).astype(BF16)
    sub = HGRN_INTRA_ROWS
    causal_sub = jnp.concatenate([causal[:sub, :sub]] * 2, axis=1)
    return dict(causal_sub=causal_sub, q_mid=flat(q_mid), k_mid=flat(k_mid), k_end=flat(k_end), q_in=flat(q_in),
                decay=jnp.exp(b_last))


def _pair_slice(p):
    return slice(p * HGRN_PAIR, (p + 1) * HGRN_PAIR)


def _hgrn_intra(ops, v, p):
    rows = v.shape[0]
    sub = HGRN_INTRA_ROWS
    lower_head = lax.broadcasted_iota(jnp.int32, (sub, HGRN_PAIR), 1) < HGRN_DIM
    zero_bf = jnp.zeros((), BF16)
    stack = lambda x: jnp.concatenate([jnp.where(lower_head, x, zero_bf), jnp.where(lower_head, zero_bf, x)], axis=0)
    causal = ops["causal_sub"]
    sl = _pair_slice(p)
    outs = []
    for h in range(rows // sub):
        rs = slice(h * sub, (h + 1) * sub)
        scores = _dot_nt(ops["q_mid"][rs, sl], stack(ops["k_mid"][rs, sl]))
        scores = jnp.where(causal, scores, 0.0).astype(BF16)
        outs.append(_dot(scores, stack(v[rs, sl])))
    return jnp.concatenate(outs, axis=0)


def _hgrn_contribs(ops, v):
    c = HGRN_CHUNK
    n_chunks = v.shape[0] // c
    return [[_dot_tn(v[n * c:(n + 1) * c, _pair_slice(p)], ops["k_end"][n * c:(n + 1) * c, _pair_slice(p)])
             for n in range(n_chunks)] for p in range(HGRN_PAIRS)]


def _hgrn_scan(ops, contribs, state_ref, reverse):
    n_chunks = len(contribs[0])
    sr = lax.broadcasted_iota(jnp.int32, (HGRN_PAIR, HGRN_PAIR), 0)
    sc = lax.broadcasted_iota(jnp.int32, (HGRN_PAIR, HGRN_PAIR), 1)
    same_head = (sr // HGRN_DIM) == (sc // HGRN_DIM)
    order = range(n_chunks - 1, -1, -1) if reverse else range(n_chunks)
    states = [[None] * n_chunks for _ in range(HGRN_PAIRS)]
    for p in range(HGRN_PAIRS):
        st = state_ref[p]
        for n in order:
            states[p][n] = st.astype(BF16)
            st = ops["decay"][n, :, _pair_slice(p)] * st + jnp.where(same_head, contribs[p][n], 0.0)
        state_ref[p] = st
    return states


def _hgrn_inter(ops, states, p):
    c = HGRN_CHUNK
    parts = [_dot_nt(ops["q_in"][n * c:(n + 1) * c, _pair_slice(p)], st) for n, st in enumerate(states[p])]
    return jnp.concatenate(parts, axis=0)


def _hgrn_kernel(lbl_ref, qf_ref, ff_ref, vf_ref, qb_ref, fb_ref, vb_ref, of_ref, ob_ref,
                 st_f_ref, st_b_ref):
    @pl.when(pl.program_id(1) == 0)
    def _():
        st_f_ref[...] = jnp.zeros_like(st_f_ref)
        st_b_ref[...] = jnp.zeros_like(st_b_ref)

    logits = lbl_ref[...]
    e = jnp.exp(logits - jnp.max(logits, axis=1, keepdims=True))
    lb = e[:, 0, :] / jnp.sum(e, axis=1)
    v_f, v_b = vf_ref[0], vb_ref[0]
    ops_f = _hgrn_prepare(qf_ref[0], ff_ref[0], lb[0:1], False)
    ops_b = _hgrn_prepare(qb_ref[0], fb_ref[0], lb[1:2], True)
    con_f = _hgrn_contribs(ops_f, v_f)
    con_b = _hgrn_contribs(ops_b, v_b)
    states_f = _hgrn_scan(ops_f, con_f, st_f_ref, False)
    states_b = _hgrn_scan(ops_b, con_b, st_b_ref, True)
    pairs = range(HGRN_PAIRS)
    inter_f = [_hgrn_inter(ops_f, states_f, p) for p in pairs]
    inter_b = [_hgrn_inter(ops_b, states_b, p) for p in pairs]
    intra_f = [_hgrn_intra(ops_f, v_f, p) for p in pairs]
    intra_b = [_hgrn_intra(ops_b, v_b, p) for p in pairs]
    of_ref[0] = jnp.concatenate([a + b for a, b in zip(intra_f, inter_f)], axis=1)
    ob_ref[0] = jnp.concatenate([a + b for a, b in zip(intra_b, inter_b)], axis=1)


def _hgrn(lb_logits, hq, hff, hfb, hi):
    b, s, w = hq.shape
    r = HGRN_ROWS
    assert s % r == 0 and r % HGRN_INTRA_ROWS == 0 and HGRN_INTRA_ROWS % HGRN_CHUNK == 0
    nb = s // r
    fwd = pl.BlockSpec((1, r, w), lambda bi, j: (bi, j, 0))
    bwd = pl.BlockSpec((1, r, w), lambda bi, j: (bi, nb - 1 - j, 0))
    state = pltpu.VMEM((HGRN_PAIRS, HGRN_PAIR, HGRN_PAIR), F32)
    return pl.pallas_call(
        _hgrn_kernel,
        grid=(b, nb),
        in_specs=[_resident(lb_logits.shape), fwd, fwd, fwd, bwd, bwd, bwd],
        out_specs=(fwd, bwd),
        out_shape=(jax.ShapeDtypeStruct((b, s, w), F32), jax.ShapeDtypeStruct((b, s, w), F32)),
        scratch_shapes=[state, state],
        compiler_params=_compiler_params(("parallel", "arbitrary")),
        name="hgrn",
    )(lb_logits, hq, hff, hi, hq, hfb, hi)


def _merge_kernel(x_ref, ya_ref, of_ref, ob_ref, hg_ref, mg_ref, win_hbm, ng_ref, gm_ref,
                  wa_hbm, wb_hbm, wo_hbm, o_ref, wg_ref, wa_ref, wb_ref, wo_ref, stage, sem):
    @pl.when(pl.program_id(0) == 0)
    def _():
        n_gate = wg_ref.shape[1]
        n_mix = win_hbm.shape[2] - n_gate
        _load_as_bf16(lambda r, n: win_hbm.at[0, pl.ds(r, n), pl.ds(n_mix, n_gate)], wg_ref.shape[0], wg_ref, stage, sem)
        for hbm, ref in ((wa_hbm, wa_ref), (wb_hbm, wb_ref), (wo_hbm, wo_ref)):
            _load_as_bf16(lambda r, n, hbm=hbm: hbm.at[0, pl.ds(r, n), :], ref.shape[0], ref, stage, sem)

    x = x_ref[...]
    d = x.shape[-1]
    h = _rmsnorm(x, mg_ref[...]).astype(BF16)
    gate_a = jax.nn.sigmoid(_dot(h, wg_ref[:, :d]))
    gate_b = jax.nn.sigmoid(_dot(h, wg_ref[:, d:]))
    o = of_ref[...] + ob_ref[...]
    on = o * lax.rsqrt(_group_mean(o * o, gm_ref[...]) + NORM_EPS) * ng_ref[...]
    yb = (on * hg_ref[...]).astype(BF16)
    merged = gate_a * _dot(ya_ref[...], wa_ref[...]) + gate_b * _dot(yb, wb_ref[...])
    o_ref[...] = x + _dot(merged.astype(BF16), wo_ref[...])


def _merge(x1, y_attn, o_f, o_b, hg, mix_g, w_in, norm_g, w_a, w_b, w_out):
    t, d = x1.shape
    tm = MERGE_ROWS
    assert t % tm == 0
    row = lambda width: pl.BlockSpec((tm, width), lambda i: (i, 0))
    hbm = pl.BlockSpec(memory_space=pl.ANY)
    return pl.pallas_call(
        _merge_kernel,
        grid=(t // tm,),
        in_specs=[row(d), row(ATTN_WIDTH), row(HGRN_WIDTH), row(HGRN_WIDTH), row(HGRN_WIDTH),
                  _resident((1, d)), hbm,
                  _resident((1, HGRN_WIDTH)), _resident((HGRN_WIDTH, HGRN_WIDTH)), hbm, hbm, hbm],
        out_specs=row(d),
        out_shape=jax.ShapeDtypeStruct((t, d), F32),
        scratch_shapes=[pltpu.VMEM((d, 2 * d), BF16), pltpu.VMEM((ATTN_WIDTH, d), BF16),
                        pltpu.VMEM((HGRN_WIDTH, d), BF16), pltpu.VMEM((d, d), BF16),
                        pltpu.VMEM((WEIGHT_STAGE_SLOTS, WEIGHT_STAGE_ROWS, 2 * d), F32),
                        pltpu.SemaphoreType.DMA((WEIGHT_STAGE_SLOTS,))],
        compiler_params=_compiler_params(("arbitrary",)),
        name="merge",
    )(x1, y_attn, o_f, o_b, hg, mix_g, w_in, norm_g, _group_matrix(HGRN_WIDTH, HGRN_DIM), w_a, w_b, w_out)


def kernel(x, ffn1_norm_g, ffn1_w_gate, ffn1_w_up, ffn1_w_down, mix_norm_g, w_in, q_norm_g, k_norm_g,
           hgrn_lb_logits, hgrn_out_norm_g, w_branch_attn, w_branch_hgrn, w_out, ffn2_norm_g,
           ffn2_w_gate, ffn2_w_up, ffn2_w_down, final_norm_g):
    b, s, d = x.shape
    t = b * s
    depth = w_in.shape[0]
    assert depth == 1 and hgrn_lb_logits.shape[1] == depth + 1
    final_g = final_norm_g.reshape(1, d)
    x1 = _ffn(x.reshape(t, d), ffn1_norm_g, ffn1_w_gate[0], ffn1_w_up[0], ffn1_w_down[0], final_g, False)
    q, k, v, hq, hff, hfb, hi, hg = _in_proj(x1.reshape(b, s, d), mix_norm_g, w_in, q_norm_g, k_norm_g)
    y_attn = _attention(_scores_bounded(q_norm_g, k_norm_g), q, k, v)
    o_f, o_b = _hgrn(hgrn_lb_logits, hq, hff, hfb, hi)
    flat = lambda a: a.reshape(t, a.shape[-1])
    x2 = _merge(x1, flat(y_attn), flat(o_f), flat(o_b), flat(hg), mix_norm_g, w_in, hgrn_out_norm_g,
                w_branch_attn, w_branch_hgrn, w_out)
    out = _ffn(x2, ffn2_norm_g, ffn2_w_gate[0], ffn2_w_up[0], ffn2_w_down[0], final_g, True)
    return out.reshape(b, s, d)
```

```python
import functools

import jax
import jax.numpy as jnp
import numpy as np
from jax import lax
from jax.experimental import pallas as pl
from jax.experimental.pallas import tpu as pltpu

F32 = jnp.float32
BF16 = jnp.bfloat16

NORM_EPS = 1e-6
ROPE_THETA = 10000.0
GRID_W = 64
HEAD_DIM = 64
ATTN_HEADS = 8
ATTN_KV_HEADS = 2
ATTN_GROUP = ATTN_HEADS // ATTN_KV_HEADS
HGRN_HEADS = 8
HGRN_DIM = 64
HGRN_CHUNK = 32
ATTN_WIDTH = ATTN_HEADS * HEAD_DIM
KV_WIDTH = ATTN_KV_HEADS * HEAD_DIM
HGRN_WIDTH = HGRN_HEADS * HGRN_DIM
HGRN_PAIR = 2 * HGRN_DIM
HGRN_PAIRS = HGRN_HEADS // 2

V7X_LANES = 128
V7X_MXU_DIM = 256
V7X_VMEM_LIMIT_BYTES = 56 * 1024 * 1024

FFN_ROWS = 512
FFN_COLS = V7X_MXU_DIM
PROJ_ROWS = 1024
ATTN_Q_ROWS = 512
ATTN_KEY_SPLIT = 4
ATTN_SCORE_BOUND = 60.0
HGRN_ROWS = 256
HGRN_INTRA_ROWS = V7X_MXU_DIM // 2
MERGE_ROWS = 1024
WEIGHT_STAGE_ROWS = 64
WEIGHT_STAGE_SLOTS = 4


def _compiler_params(semantics):
    return pltpu.CompilerParams(dimension_semantics=semantics, vmem_limit_bytes=V7X_VMEM_LIMIT_BYTES)


def _resident(shape):
    zeros = (0,) * len(shape)
    return pl.BlockSpec(shape, lambda *_: zeros, pipeline_mode=pl.Buffered(1))


def _rmsnorm(x, g):
    return x * lax.rsqrt(jnp.mean(x * x, axis=-1, keepdims=True) + NORM_EPS) * g


def _dot(a, b):
    return jnp.dot(a, b, preferred_element_type=F32)


def _dot_nt(a, b):
    return lax.dot_general(a, b, (((1,), (1,)), ((), ())), preferred_element_type=F32)


def _dot_tn(a, b):
    return lax.dot_general(a, b, (((0,), (0,)), ((), ())), preferred_element_type=F32)


def _split2(x):
    hi = x.astype(BF16)
    lo = (x - hi.astype(F32)).astype(BF16)
    return hi, lo


def _split3(x):
    hi = x.astype(BF16)
    r = x - hi.astype(F32)
    mid = r.astype(BF16)
    lo = (r - mid.astype(F32)).astype(BF16)
    return hi, mid, lo


def _group_mean(x, gmat):
    w = gmat.shape[0]
    hi, lo = _split2(x)
    tiles = [_dot(hi[:, i:i + w], gmat) + _dot(lo[:, i:i + w], gmat) for i in range(0, x.shape[1], w)]
    return tiles[0] if len(tiles) == 1 else jnp.concatenate(tiles, axis=1)


def _load_as_bf16(src_rows, n_rows, dst, stage, sem):
    depth, chunk, _ = stage.shape
    cols = dst.shape[1]
    n = n_rows // chunk
    assert n_rows % chunk == 0 and cols <= stage.shape[2]

    def copy(k):
        slot = k % depth
        return pltpu.make_async_copy(src_rows(k * chunk, chunk), stage.at[slot, :, pl.ds(0, cols)], sem.at[slot])

    for k in range(min(depth - 1, n)):
        copy(k).start()
    for k in range(n):
        if k + depth - 1 < n:
            copy(k + depth - 1).start()
        copy(k).wait()
        dst[pl.ds(k * chunk, chunk), :] = stage[k % depth, :, :cols].astype(BF16)


def _ffn_kernel(x_ref, g_ref, wg_hbm, wu_hbm, wd_hbm, fg_ref, o_ref,
                wg_ref, wu_ref, wd_ref, stage_g, stage_u, stage_d, sem, *, final_norm):
    d_ff = wg_ref.shape[1]
    n_chunks = d_ff // FFN_COLS

    def chunk_copies(c):
        cols = pl.ds(c * FFN_COLS, FFN_COLS)
        slot = c % 2
        return (pltpu.make_async_copy(wg_hbm.at[:, cols], stage_g.at[slot], sem.at[0, slot]),
                pltpu.make_async_copy(wu_hbm.at[:, cols], stage_u.at[slot], sem.at[1, slot]),
                pltpu.make_async_copy(wd_hbm.at[cols, :], stage_d.at[slot], sem.at[2, slot]))

    def tile(stream_weights):
        if stream_weights:
            for cp in chunk_copies(0):
                cp.start()
        x = x_ref[...]
        xn = _rmsnorm(x, g_ref[...]).astype(BF16)
        acc = None
        for c in range(n_chunks):
            sl = slice(c * FFN_COLS, (c + 1) * FFN_COLS)
            if stream_weights:
                if c + 1 < n_chunks:
                    for cp in chunk_copies(c + 1):
                        cp.start()
                for cp in chunk_copies(c):
                    cp.wait()
                wg_ref[:, sl] = stage_g[c % 2].astype(BF16)
                wu_ref[:, sl] = stage_u[c % 2].astype(BF16)
                wd_ref[sl, :] = stage_d[c % 2].astype(BF16)
            gate = _dot(xn, wg_ref[:, sl])
            up = _dot(xn, wu_ref[:, sl])
            act = (jax.nn.silu(gate) * up).astype(BF16)
            part = _dot(act, wd_ref[sl, :])
            acc = part if acc is None else acc + part
        y = x + 0.5 * acc
        if final_norm:
            y = _rmsnorm(y, fg_ref[...])
        o_ref[...] = y

    pl.when(pl.program_id(0) == 0)(functools.partial(tile, True))
    pl.when(pl.program_id(0) != 0)(functools.partial(tile, False))


def _ffn(x, norm_g, w_gate, w_up, w_down, final_g, final_norm):
    t, d = x.shape
    d_ff = w_gate.shape[1]
    assert t % FFN_ROWS == 0 and d_ff % FFN_COLS == 0
    row = pl.BlockSpec((FFN_ROWS, d), lambda i: (i, 0))
    hbm = pl.BlockSpec(memory_space=pl.ANY)
    return pl.pallas_call(
        functools.partial(_ffn_kernel, final_norm=final_norm),
        grid=(t // FFN_ROWS,),
        in_specs=[row, _resident((1, d)), hbm, hbm, hbm, _resident((1, d))],
        out_specs=row,
        out_shape=jax.ShapeDtypeStruct((t, d), F32),
        scratch_shapes=[pltpu.VMEM((d, d_ff), BF16), pltpu.VMEM((d, d_ff), BF16), pltpu.VMEM((d_ff, d), BF16),
                        pltpu.VMEM((2, d, FFN_COLS), F32), pltpu.VMEM((2, d, FFN_COLS), F32),
                        pltpu.VMEM((2, FFN_COLS, d), F32), pltpu.SemaphoreType.DMA((3, 2))],
        compiler_params=_compiler_params(("arbitrary",)),
        name="ffn_final" if final_norm else "ffn",
    )(x, norm_g, w_gate, w_up, w_down, final_g)


def _rope(xn, cos_t, sin_t, width):
    half = HEAD_DIM // 2
    lane = lax.broadcasted_iota(jnp.int32, xn.shape, 1)
    first_half = (lane % HEAD_DIM) < half
    partner = jnp.where(first_half, pltpu.roll(xn, width - half, 1), pltpu.roll(xn, half, 1))
    return xn * cos_t + partner * sin_t


def _proj_kernel(x_ref, g_ref, w_hbm, qg_ref, kg_ref, gq_ref, gk_ref, cos_ref, sin_ref,
                 q_ref, k_ref, v_ref, hq_ref, ff_ref, fb_ref, hi_ref, hg_ref, w_ref, stage, sem):
    @pl.when((pl.program_id(0) == 0) & (pl.program_id(1) == 0))
    def _():
        n_in = w_ref.shape[1]
        _load_as_bf16(lambda r, n: w_hbm.at[0, pl.ds(r, n), pl.ds(0, n_in)], w_ref.shape[0], w_ref, stage, sem)

    h = _rmsnorm(x_ref[0], g_ref[...]).astype(BF16)

    def cols(start, width):
        return _dot(h, w_ref[:, start:start + width])

    cos2 = cos_ref[...]
    sin2 = sin_ref[...]
    off = 0
    q = cols(off, ATTN_WIDTH)
    off += ATTN_WIDTH
    reps = ATTN_WIDTH // V7X_LANES
    qn = q * lax.rsqrt(_group_mean(q * q, gq_ref[...]) + NORM_EPS) * qg_ref[...]
    qr = _rope(qn, jnp.concatenate([cos2] * reps, axis=1), jnp.concatenate([sin2] * reps, axis=1),
               ATTN_WIDTH) * (HEAD_DIM ** -0.5)
    for hd in range(ATTN_HEADS):
        q_ref[0, hd] = qr[:, hd * HEAD_DIM:(hd + 1) * HEAD_DIM].astype(BF16)
    k = cols(off, KV_WIDTH)
    off += KV_WIDTH
    kn = k * lax.rsqrt(_group_mean(k * k, gk_ref[...]) + NORM_EPS) * kg_ref[...]
    kr = _rope(kn, cos2, sin2, KV_WIDTH)
    v = cols(off, KV_WIDTH)
    off += KV_WIDTH
    lane = lax.broadcasted_iota(jnp.int32, v.shape, 1)
    for hd in range(ATTN_KV_HEADS):
        k_ref[0, hd] = kr[:, hd * HEAD_DIM:(hd + 1) * HEAD_DIM].astype(BF16)
        v_hd = v if hd == 0 else pltpu.roll(v, KV_WIDTH - hd * HEAD_DIM, 1)
        v_ref[0, hd] = jnp.where(lane < HEAD_DIM, v_hd, 1.0).T.astype(BF16)
    hq_ref[0] = jax.nn.silu(cols(off, HGRN_WIDTH))
    off += HGRN_WIDTH
    ff_ref[0] = cols(off, HGRN_WIDTH)
    off += HGRN_WIDTH
    fb_ref[0] = cols(off, HGRN_WIDTH)
    off += HGRN_WIDTH
    hi_ref[0] = cols(off, HGRN_WIDTH).astype(BF16)
    off += HGRN_WIDTH
    hg_ref[0] = jax.nn.silu(cols(off, HGRN_WIDTH))


def _group_matrix(width, group):
    idx = np.arange(width) // group
    return jnp.asarray((idx[:, None] == idx[None, :]).astype(np.float32) / group, dtype=BF16)


def _rope_tables(seq_len):
    f32 = np.float32
    rows = seq_len // GRID_W
    row = np.repeat(np.arange(rows, dtype=f32), GRID_W)
    col = np.tile(np.arange(GRID_W, dtype=f32), rows)
    n_freq = HEAD_DIM // 4
    inv = (f32(ROPE_THETA) ** (-np.arange(n_freq, dtype=f32) / f32(n_freq))).astype(f32)
    ang = np.concatenate([row[:, None] * inv, col[:, None] * inv], axis=-1)
    cos, sin = np.cos(ang).astype(f32), np.sin(ang).astype(f32)
    cos2 = np.concatenate([cos, cos] * 2, axis=-1)
    sin2 = np.concatenate([-sin, sin] * 2, axis=-1)
    return jnp.asarray(cos2), jnp.asarray(sin2)


def _in_proj(x1, norm_g, w_in, q_g, k_g):
    b, s, d = x1.shape
    n_in = ATTN_WIDTH + 2 * KV_WIDTH + 5 * HGRN_WIDTH
    assert s % PROJ_ROWS == 0 and w_in.shape[2] == n_in + 2 * d
    cos2, sin2 = _rope_tables(s)
    tm = PROJ_ROWS
    tok = lambda width: pl.BlockSpec((1, tm, width), lambda bi, i: (bi, i, 0))
    heads = lambda n, width=HEAD_DIM: pl.BlockSpec((1, n, tm, width), lambda bi, i: (bi, 0, i, 0))
    pos = pl.BlockSpec((tm, V7X_LANES), lambda bi, i: (i, 0))
    f32_tok = lambda width: jax.ShapeDtypeStruct((b, s, width), F32)
    out_shape = (
        jax.ShapeDtypeStruct((b, ATTN_HEADS, s, HEAD_DIM), BF16),
        jax.ShapeDtypeStruct((b, ATTN_KV_HEADS, s, HEAD_DIM), BF16),
        jax.ShapeDtypeStruct((b, ATTN_KV_HEADS, V7X_LANES, s), BF16),
        f32_tok(HGRN_WIDTH), f32_tok(HGRN_WIDTH), f32_tok(HGRN_WIDTH),
        jax.ShapeDtypeStruct((b, s, HGRN_WIDTH), BF16),
        f32_tok(HGRN_WIDTH),
    )
    vt_spec = pl.BlockSpec((1, ATTN_KV_HEADS, V7X_LANES, tm), lambda bi, i: (bi, 0, 0, i))
    out_specs = (heads(ATTN_HEADS), heads(ATTN_KV_HEADS), vt_spec,
                 tok(HGRN_WIDTH), tok(HGRN_WIDTH), tok(HGRN_WIDTH), tok(HGRN_WIDTH), tok(HGRN_WIDTH))
    return pl.pallas_call(
        _proj_kernel,
        grid=(b, s // tm),
        in_specs=[tok(d), _resident((1, d)), pl.BlockSpec(memory_space=pl.ANY),
                  _resident((1, ATTN_WIDTH)), _resident((1, KV_WIDTH)),
                  _resident((V7X_MXU_DIM, V7X_MXU_DIM)), _resident((KV_WIDTH, KV_WIDTH)), pos, pos],
        out_specs=out_specs,
        out_shape=out_shape,
        scratch_shapes=[pltpu.VMEM((d, n_in), BF16), pltpu.VMEM((WEIGHT_STAGE_SLOTS, WEIGHT_STAGE_ROWS, n_in), F32),
                        pltpu.SemaphoreType.DMA((WEIGHT_STAGE_SLOTS,))],
        compiler_params=_compiler_params(("arbitrary", "arbitrary")),
        name="in_proj",
    )(x1, norm_g, w_in, jnp.tile(q_g, (1, ATTN_HEADS)), jnp.tile(k_g, (1, ATTN_KV_HEADS)),
      _group_matrix(V7X_MXU_DIM, HEAD_DIM), _group_matrix(KV_WIDTH, HEAD_DIM), cos2, sin2)


def _attn_kernel(bounded_ref, q_ref, k_ref, vt_ref, o_ref):
    tq = q_ref.shape[2]
    s_len = k_ref.shape[2]
    q = q_ref[0].reshape(ATTN_GROUP * tq, HEAD_DIM)

    def finish(accs):
        norm = [a[:HEAD_DIM] * (1.0 / a[HEAD_DIM:HEAD_DIM + 1]) for a in accs]
        pairs = [jnp.concatenate(norm[r:r + 2], axis=0).T for r in range(0, ATTN_GROUP, 2)]
        o_ref[0] = jnp.concatenate(pairs, axis=1).astype(BF16)

    def attend(shifted):
        kc = s_len // ATTN_KEY_SPLIT
        chunks = [slice(c * kc, (c + 1) * kc) for c in range(ATTN_KEY_SPLIT)]
        if shifted:
            s_t = _dot_nt(k_ref[0, 0], q)
            p_full = jnp.exp(s_t - jnp.max(s_t, axis=0, keepdims=True)).astype(BF16)
            p_chunks = [p_full[ks] for ks in chunks]
        else:
            p_chunks = [jnp.exp(_dot_nt(k_ref[0, 0, ks, :], q)).astype(BF16) for ks in chunks]
        accs = [None] * ATTN_GROUP
        for ks, p_t in zip(chunks, p_chunks):
            vt_c = vt_ref[0, 0, :, ks]
            for r in range(ATTN_GROUP):
                part = _dot(vt_c, p_t[:, r * tq:(r + 1) * tq])
                accs[r] = part if accs[r] is None else accs[r] + part
        finish(accs)

    pl.when(bounded_ref[0] == 1)(functools.partial(attend, False))
    pl.when(bounded_ref[0] != 1)(functools.partial(attend, True))


def _attention(bounded, q, k, v):
    b, _, s, _ = q.shape
    tq = ATTN_Q_ROWS
    assert s % tq == 0 and s % (ATTN_KEY_SPLIT * V7X_LANES) == 0
    k_spec = pl.BlockSpec((1, 1, s, HEAD_DIM), lambda bi, g, i, *_: (bi, g, 0, 0))
    vt_spec = pl.BlockSpec((1, 1, V7X_LANES, s), lambda bi, g, i, *_: (bi, g, 0, 0))
    grid_spec = pltpu.PrefetchScalarGridSpec(
        num_scalar_prefetch=1,
        grid=(b, ATTN_KV_HEADS, s // tq),
        in_specs=[pl.BlockSpec((1, ATTN_GROUP, tq, HEAD_DIM), lambda bi, g, i, *_: (bi, g, i, 0)),
                  k_spec, vt_spec],
        out_specs=pl.BlockSpec((1, tq, ATTN_GROUP * HEAD_DIM), lambda bi, g, i, *_: (bi, i, g)),
    )
    return pl.pallas_call(
        _attn_kernel,
        grid_spec=grid_spec,
        out_shape=jax.ShapeDtypeStruct((b, s, ATTN_WIDTH), BF16),
        compiler_params=_compiler_params(("parallel", "parallel", "parallel")),
        name="attention",
    )(bounded, q, k, v)


def _scores_bounded(q_g, k_g):
    bound = (HEAD_DIM ** 0.5) * jnp.max(jnp.abs(q_g)) * jnp.max(jnp.abs(k_g))
    return (bound <= ATTN_SCORE_BOUND).astype(jnp.int32).reshape(1)


def _hgrn_prepare(qt, f_logit, lb, reverse):
    rows = qt.shape[0]
    c = HGRN_CHUNK
    n_chunks = rows // c
    f = lb + (1.0 - lb) * jax.nn.sigmoid(f_logit)
    kk = 1.0 - f
    log_f = jnp.log(f)
    r_i = lax.broadcasted_iota(jnp.int32, (rows, rows), 0)
    c_i = lax.broadcasted_iota(jnp.int32, (rows, rows), 1)
    same_chunk = (r_i // c) == (c_i // c)
    causal = same_chunk & ((c_i >= r_i) if reverse else (c_i <= r_i))
    tri = jnp.where(causal, 1.0, 0.0).astype(BF16)
    hi, mid, lo = _split3(log_f)
    bcum = _dot(tri, hi) + _dot(tri, mid) + _dot(tri, lo)
    b3 = bcum.reshape(n_chunks, c, bcum.shape[-1])
    mid_row = c // 2 if reverse else c // 2 - 1
    last_row = 0 if reverse else c - 1
    b_mid = b3[:, mid_row:mid_row + 1, :]
    b_last = b3[:, last_row:last_row + 1, :]
    q3 = qt.reshape(b3.shape)
    k3 = kk.reshape(b3.shape)
    d_mid = b3 - b_mid
    q_mid = q3 * jnp.exp(d_mid)
    k_mid = k3 * jnp.exp(-d_mid)
    k_end = k_mid * jnp.exp(b_last - b_mid)
    q_in = q_mid * jnp.exp(b_mid)
    flat = lambda a: a.reshape(bcum.shape).astype(BF16)
    sub = HGRN_INTRA_ROWS
    causal_sub = jnp.concatenate([causal[:sub, :sub]] * 2, axis=1)
    return dict(causal_sub=causal_sub, q_mid=flat(q_mid), k_mid=flat(k_mid), k_end=flat(k_end), q_in=flat(q_in),
                decay=jnp.exp(b_last))


def _pair_slice(p):
    return slice(p * HGRN_PAIR, (p + 1) * HGRN_PAIR)


def _hgrn_intra(ops, v, p):
    rows = v.shape[0]
    sub = HGRN_INTRA_ROWS
    lower_head = lax.broadcasted_iota(jnp.int32, (sub, HGRN_PAIR), 1) < HGRN_DIM
    zero_bf = jnp.zeros((), BF16)
    stack = lambda x: jnp.concatenate([jnp.where(lower_head, x, zero_bf), jnp.where(lower_head, zero_bf, x)], axis=0)
    causal = ops["causal_sub"]
    sl = _pair_slice(p)
    outs = []
    for h in range(rows // sub):
        rs = slice(h * sub, (h + 1) * sub)
        scores = _dot_nt(ops["q_mid"][rs, sl], stack(ops["k_mid"][rs, sl]))
        scores = jnp.where(causal, scores, 0.0).astype(BF16)
        outs.append(_dot(scores, stack(v[rs, sl])))
    return jnp.concatenate(outs, axis=0)


def _hgrn_contribs(ops, v):
    c = HGRN_CHUNK
    n_chunks = v.shape[0] // c
    return [[_dot_tn(v[n * c:(n + 1) * c, _pair_slice(p)], ops["k_end"][n * c:(n + 1) * c, _pair_slice(p)])
             for n in range(n_chunks)] for p in range(HGRN_PAIRS)]


def _hgrn_scan(ops, contribs, state_ref, reverse):
    n_chunks = len(contribs[0])
    sr = lax.broadcasted_iota(jnp.int32, (HGRN_PAIR, HGRN_PAIR), 0)
    sc = lax.broadcasted_iota(jnp.int32, (HGRN_PAIR, HGRN_PAIR), 1)
    same_head = (sr // HGRN_DIM) == (sc // HGRN_DIM)
    order = range(n_chunks - 1, -1, -1) if reverse else range(n_chunks)
    states = [[None] * n_chunks for _ in range(HGRN_PAIRS)]
    for p in range(HGRN_PAIRS):
        st = state_ref[p]
        for n in order:
            states[p][n] = st.astype(BF16)
            st = ops["decay"][n, :, _pair_slice(p)] * st + jnp.where(same_head, contribs[p][n], 0.0)
        state_ref[p] = st
    return states


def _hgrn_inter(ops, states, p):
    c = HGRN_CHUNK
    parts = [_dot_nt(ops["q_in"][n * c:(n + 1) * c, _pair_slice(p)], st) for n, st in enumerate(states[p])]
    return jnp.concatenate(parts, axis=0)


def _hgrn_kernel(lbl_ref, qf_ref, ff_ref, vf_ref, qb_ref, fb_ref, vb_ref, of_ref, ob_ref,
                 st_f_ref, st_b_ref):
    @pl.when(pl.program_id(1) == 0)
    def _():
        st_f_ref[...] = jnp.zeros_like(st_f_ref)
        st_b_ref[...] = jnp.zeros_like(st_b_ref)

    logits = lbl_ref[...]
    e = jnp.exp(logits - jnp.max(logits, axis=1, keepdims=True))
    lb = e[:, 0, :] / jnp.sum(e, axis=1)
    v_f, v_b = vf_ref[0], vb_ref[0]
    ops_f = _hgrn_prepare(qf_ref[0], ff_ref[0], lb[0:1], False)
    ops_b = _hgrn_prepare(qb_ref[0], fb_ref[0], lb[1:2], True)
    con_f = _hgrn_contribs(ops_f, v_f)
    con_b = _hgrn_contribs(ops_b, v_b)
    states_f = _hgrn_scan(ops_f, con_f, st_f_ref, False)
    states_b = _hgrn_scan(ops_b, con_b, st_b_ref, True)
    pairs = range(HGRN_PAIRS)
    inter_f = [_hgrn_inter(ops_f, states_f, p) for p in pairs]
    inter_b = [_hgrn_inter(ops_b, states_b, p) for p in pairs]
    intra_f = [_hgrn_intra(ops_f, v_f, p) for p in pairs]
    intra_b = [_hgrn_intra(ops_b, v_b, p) for p in pairs]
    of_ref[0] = jnp.concatenate([a + b for a, b in zip(intra_f, inter_f)], axis=1)
    ob_ref[0] = jnp.concatenate([a + b for a, b in zip(intra_b, inter_b)], axis=1)


def _hgrn(lb_logits, hq, hff, hfb, hi):
    b, s, w = hq.shape
    r = HGRN_ROWS
    assert s % r == 0 and r % HGRN_INTRA_ROWS == 0 and HGRN_INTRA_ROWS % HGRN_CHUNK == 0
    nb = s // r
    fwd = pl.BlockSpec((1, r, w), lambda bi, j: (bi, j, 0))
    bwd = pl.BlockSpec((1, r, w), lambda bi, j: (bi, nb - 1 - j, 0))
    state = pltpu.VMEM((HGRN_PAIRS, HGRN_PAIR, HGRN_PAIR), F32)
    return pl.pallas_call(
        _hgrn_kernel,
        grid=(b, nb),
        in_specs=[_resident(lb_logits.shape), fwd, fwd, fwd, bwd, bwd, bwd],
        out_specs=(fwd, bwd),
        out_shape=(jax.ShapeDtypeStruct((b, s, w), F32), jax.ShapeDtypeStruct((b, s, w), F32)),
        scratch_shapes=[state, state],
        compiler_params=_compiler_params(("parallel", "arbitrary")),
        name="hgrn",
    )(lb_logits, hq, hff, hi, hq, hfb, hi)


def _merge_kernel(x_ref, ya_ref, of_ref, ob_ref, hg_ref, mg_ref, win_hbm, ng_ref, gm_ref,
                  wa_hbm, wb_hbm, wo_hbm, o_ref, wg_ref, wa_ref, wb_ref, wo_ref, stage, sem):
    @pl.when(pl.program_id(0) == 0)
    def _():
        n_gate = wg_ref.shape[1]
        n_mix = win_hbm.shape[2] - n_gate
        _load_as_bf16(lambda r, n: win_hbm.at[0, pl.ds(r, n), pl.ds(n_mix, n_gate)], wg_ref.shape[0], wg_ref, stage, sem)
        for hbm, ref in ((wa_hbm, wa_ref), (wb_hbm, wb_ref), (wo_hbm, wo_ref)):
            _load_as_bf16(lambda r, n, hbm=hbm: hbm.at[0, pl.ds(r, n), :], ref.shape[0], ref, stage, sem)

    x = x_ref[...]
    d = x.shape[-1]
    h = _rmsnorm(x, mg_ref[...]).astype(BF16)
    gate_a = jax.nn.sigmoid(_dot(h, wg_ref[:, :d]))
    gate_b = jax.nn.sigmoid(_dot(h, wg_ref[:, d:]))
    o = of_ref[...] + ob_ref[...]
    on = o * lax.rsqrt(_group_mean(o * o, gm_ref[...]) + NORM_EPS) * ng_ref[...]
    yb = (on * hg_ref[...]).astype(BF16)
    merged = gate_a * _dot(ya_ref[...], wa_ref[...]) + gate_b * _dot(yb, wb_ref[...])
    o_ref[...] = x + _dot(merged.astype(BF16), wo_ref[...])


def _merge(x1, y_attn, o_f, o_b, hg, mix_g, w_in, norm_g, w_a, w_b, w_out):
    t, d = x1.shape
    tm = MERGE_ROWS
    assert t % tm == 0
    row = lambda width: pl.BlockSpec((tm, width), lambda i: (i, 0))
    hbm = pl.BlockSpec(memory_space=pl.ANY)
    return pl.pallas_call(
        _merge_kernel,
        grid=(t // tm,),
        in_specs=[row(d), row(ATTN_WIDTH), row(HGRN_WIDTH), row(HGRN_WIDTH), row(HGRN_WIDTH),
                  _resident((1, d)), hbm,
                  _resident((1, HGRN_WIDTH)), _resident((V7X_MXU_DIM, V7X_MXU_DIM)), hbm, hbm, hbm],
        out_specs=row(d),
        out_shape=jax.ShapeDtypeStruct((t, d), F32),
        scratch_shapes=[pltpu.VMEM((d, 2 * d), BF16), pltpu.VMEM((ATTN_WIDTH, d), BF16),
                        pltpu.VMEM((HGRN_WIDTH, d), BF16), pltpu.VMEM((d, d), BF16),
                        pltpu.VMEM((WEIGHT_STAGE_SLOTS, WEIGHT_STAGE_ROWS, 2 * d), F32),
                        pltpu.SemaphoreType.DMA((WEIGHT_STAGE_SLOTS,))],
        compiler_params=_compiler_params(("arbitrary",)),
        name="merge",
    )(x1, y_attn, o_f, o_b, hg, mix_g, w_in, norm_g, _group_matrix(V7X_MXU_DIM, HGRN_DIM), w_a, w_b, w_out)


def kernel(x, ffn1_norm_g, ffn1_w_gate, ffn1_w_up, ffn1_w_down, mix_norm_g, w_in, q_norm_g, k_norm_g,
           hgrn_lb_logits, hgrn_out_norm_g, w_branch_attn, w_branch_hgrn, w_out, ffn2_norm_g,
           ffn2_w_gate, ffn2_w_up, ffn2_w_down, final_norm_g):
    b, s, d = x.shape
    t = b * s
    depth = w_in.shape[0]
    assert depth == 1 and hgrn_lb_logits.shape[1] == depth + 1
    final_g = final_norm_g.reshape(1, d)
    x1 = _ffn(x.reshape(t, d), ffn1_norm_g, ffn1_w_gate[0], ffn1_w_up[0], ffn1_w_down[0], final_g, False)
    q, k, v, hq, hff, hfb, hi, hg = _in_proj(x1.reshape(b, s, d), mix_norm_g, w_in, q_norm_g, k_norm_g)
    y_attn = _attention(_scores_bounded(q_norm_g, k_norm_g), q, k, v)
    o_f, o_b = _hgrn(hgrn_lb_logits, hq, hff, hfb, hi)
    flat = lambda a: a.reshape(t, a.shape[-1])
    x2 = _merge(x1, flat(y_attn), flat(o_f), flat(o_b), flat(hg), mix_norm_g, w_in, hgrn_out_norm_g,
                w_branch_attn, w_branch_hgrn, w_out)
    out = _ffn(x2, ffn2_norm_g, ffn2_w_gate[0], ffn2_w_up[0], ffn2_w_down[0], final_g, True)
    return out.reshape(b, s, d)
```

```python
import functools

import jax
import jax.numpy as jnp
import numpy as np
from jax import lax
from jax.experimental import pallas as pl
from jax.experimental.pallas import tpu as pltpu

F32 = jnp.float32
BF16 = jnp.bfloat16

NORM_EPS = 1e-6
ROPE_THETA = 10000.0
GRID_W = 64
HEAD_DIM = 64
ATTN_HEADS = 8
ATTN_KV_HEADS = 2
ATTN_GROUP = ATTN_HEADS // ATTN_KV_HEADS
HGRN_HEADS = 8
HGRN_DIM = 64
HGRN_CHUNK = 32
ATTN_WIDTH = ATTN_HEADS * HEAD_DIM
KV_WIDTH = ATTN_KV_HEADS * HEAD_DIM
HGRN_WIDTH = HGRN_HEADS * HGRN_DIM
HGRN_PAIR = 2 * HGRN_DIM
HGRN_PAIRS = HGRN_HEADS // 2

V7X_LANES = 128
V7X_MXU_DIM = 256
V7X_VMEM_LIMIT_BYTES = 56 * 1024 * 1024

FFN_ROWS = 512
FFN_COLS = V7X_MXU_DIM
PROJ_ROWS = 1024
ATTN_Q_ROWS = 512
ATTN_KEY_SPLIT = 4
ATTN_SCORE_BOUND = 60.0
HGRN_ROWS = 256
HGRN_INTRA_ROWS = V7X_MXU_DIM // 2
MERGE_ROWS = 1024
WEIGHT_STAGE_ROWS = 64
WEIGHT_STAGE_SLOTS = 4


def _compiler_params(semantics):
    return pltpu.CompilerParams(dimension_semantics=semantics, vmem_limit_bytes=V7X_VMEM_LIMIT_BYTES)


def _resident(shape):
    zeros = (0,) * len(shape)
    return pl.BlockSpec(shape, lambda *_: zeros, pipeline_mode=pl.Buffered(1))


def _rmsnorm(x, g):
    return x * lax.rsqrt(jnp.mean(x * x, axis=-1, keepdims=True) + NORM_EPS) * g


def _dot(a, b):
    return jnp.dot(a, b, preferred_element_type=F32)


def _dot_nt(a, b):
    return lax.dot_general(a, b, (((1,), (1,)), ((), ())), preferred_element_type=F32)


def _dot_tn(a, b):
    return lax.dot_general(a, b, (((0,), (0,)), ((), ())), preferred_element_type=F32)


def _split2(x):
    hi = x.astype(BF16)
    lo = (x - hi.astype(F32)).astype(BF16)
    return hi, lo


def _split3(x):
    hi = x.astype(BF16)
    r = x - hi.astype(F32)
    mid = r.astype(BF16)
    lo = (r - mid.astype(F32)).astype(BF16)
    return hi, mid, lo


def _group_mean(x, gmat):
    w = gmat.shape[0]
    hi, lo = _split2(x)
    tiles = [_dot(hi[:, i:i + w], gmat) + _dot(lo[:, i:i + w], gmat) for i in range(0, x.shape[1], w)]
    return tiles[0] if len(tiles) == 1 else jnp.concatenate(tiles, axis=1)


def _load_as_bf16(src_rows, n_rows, dst, stage, sem):
    depth, chunk, _ = stage.shape
    cols = dst.shape[1]
    n = n_rows // chunk
    assert n_rows % chunk == 0 and cols <= stage.shape[2]

    def copy(k):
        slot = k % depth
        return pltpu.make_async_copy(src_rows(k * chunk, chunk), stage.at[slot, :, pl.ds(0, cols)], sem.at[slot])

    for k in range(min(depth - 1, n)):
        copy(k).start()
    for k in range(n):
        if k + depth - 1 < n:
            copy(k + depth - 1).start()
        copy(k).wait()
        dst[pl.ds(k * chunk, chunk), :] = stage[k % depth, :, :cols].astype(BF16)


def _ffn_kernel(x_ref, g_ref, wg_hbm, wu_hbm, wd_hbm, fg_ref, o_ref,
                wg_ref, wu_ref, wd_ref, stage_g, stage_u, stage_d, sem, *, final_norm):
    d_ff = wg_ref.shape[1]
    n_chunks = d_ff // FFN_COLS

    def chunk_copies(c):
        cols = pl.ds(c * FFN_COLS, FFN_COLS)
        slot = c % 2
        return (pltpu.make_async_copy(wg_hbm.at[:, cols], stage_g.at[slot], sem.at[0, slot]),
                pltpu.make_async_copy(wu_hbm.at[:, cols], stage_u.at[slot], sem.at[1, slot]),
                pltpu.make_async_copy(wd_hbm.at[cols, :], stage_d.at[slot], sem.at[2, slot]))

    def tile(stream_weights):
        if stream_weights:
            for cp in chunk_copies(0):
                cp.start()
        x = x_ref[...]
        xn = _rmsnorm(x, g_ref[...]).astype(BF16)
        acc = None
        for c in range(n_chunks):
            sl = slice(c * FFN_COLS, (c + 1) * FFN_COLS)
            if stream_weights:
                if c + 1 < n_chunks:
                    for cp in chunk_copies(c + 1):
                        cp.start()
                for cp in chunk_copies(c):
                    cp.wait()
                wg_ref[:, sl] = stage_g[c % 2].astype(BF16)
                wu_ref[:, sl] = stage_u[c % 2].astype(BF16)
                wd_ref[sl, :] = stage_d[c % 2].astype(BF16)
            gate = _dot(xn, wg_ref[:, sl])
            up = _dot(xn, wu_ref[:, sl])
            act = (jax.nn.silu(gate) * up).astype(BF16)
            part = _dot(act, wd_ref[sl, :])
            acc = part if acc is None else acc + part
        y = x + 0.5 * acc
        if final_norm:
            y = _rmsnorm(y, fg_ref[...])
        o_ref[...] = y

    pl.when(pl.program_id(0) == 0)(functools.partial(tile, True))
    pl.when(pl.program_id(0) != 0)(functools.partial(tile, False))


def _ffn(x, norm_g, w_gate, w_up, w_down, final_g, final_norm):
    t, d = x.shape
    d_ff = w_gate.shape[1]
    assert t % FFN_ROWS == 0 and d_ff % FFN_COLS == 0
    row = pl.BlockSpec((FFN_ROWS, d), lambda i: (i, 0))
    hbm = pl.BlockSpec(memory_space=pl.ANY)
    return pl.pallas_call(
        functools.partial(_ffn_kernel, final_norm=final_norm),
        grid=(t // FFN_ROWS,),
        in_specs=[row, _resident((1, d)), hbm, hbm, hbm, _resident((1, d))],
        out_specs=row,
        out_shape=jax.ShapeDtypeStruct((t, d), F32),
        scratch_shapes=[pltpu.VMEM((d, d_ff), BF16), pltpu.VMEM((d, d_ff), BF16), pltpu.VMEM((d_ff, d), BF16),
                        pltpu.VMEM((2, d, FFN_COLS), F32), pltpu.VMEM((2, d, FFN_COLS), F32),
                        pltpu.VMEM((2, FFN_COLS, d), F32), pltpu.SemaphoreType.DMA((3, 2))],
        compiler_params=_compiler_params(("arbitrary",)),
        name="ffn_final" if final_norm else "ffn",
    )(x, norm_g, w_gate, w_up, w_down, final_g)


def _rope(xn, cos_t, sin_t, width):
    half = HEAD_DIM // 2
    lane = lax.broadcasted_iota(jnp.int32, xn.shape, 1)
    first_half = (lane % HEAD_DIM) < half
    partner = jnp.where(first_half, pltpu.roll(xn, width - half, 1), pltpu.roll(xn, half, 1))
    return xn * cos_t + partner * sin_t


def _proj_kernel(x_ref, g_ref, w_hbm, qg_ref, kg_ref, gq_ref, gk_ref, cos_ref, sin_ref,
                 q_ref, k_ref, v_ref, hq_ref, ff_ref, fb_ref, hi_ref, hg_ref, w_ref, stage, sem, wvt_ref):
    @pl.when((pl.program_id(0) == 0) & (pl.program_id(1) == 0))
    def _():
        n_in = w_ref.shape[1]
        _load_as_bf16(lambda r, n: w_hbm.at[0, pl.ds(r, n), pl.ds(0, n_in)], w_ref.shape[0], w_ref, stage, sem)
        v0 = ATTN_WIDTH + KV_WIDTH
        wvt_ref[...] = w_ref[:, v0:v0 + KV_WIDTH].astype(F32).T.astype(BF16)

    h = _rmsnorm(x_ref[0], g_ref[...]).astype(BF16)

    def cols(start, width):
        return _dot(h, w_ref[:, start:start + width])

    cos2 = cos_ref[...]
    sin2 = sin_ref[...]
    off = 0
    q = cols(off, ATTN_WIDTH)
    off += ATTN_WIDTH
    reps = ATTN_WIDTH // V7X_LANES
    qn = q * lax.rsqrt(_group_mean(q * q, gq_ref[...]) + NORM_EPS) * qg_ref[...]
    qr = _rope(qn, jnp.concatenate([cos2] * reps, axis=1), jnp.concatenate([sin2] * reps, axis=1),
               ATTN_WIDTH) * (HEAD_DIM ** -0.5)
    for hd in range(ATTN_HEADS):
        q_ref[0, hd] = qr[:, hd * HEAD_DIM:(hd + 1) * HEAD_DIM].astype(BF16)
    k = cols(off, KV_WIDTH)
    off += KV_WIDTH
    kn = k * lax.rsqrt(_group_mean(k * k, gk_ref[...]) + NORM_EPS) * kg_ref[...]
    kr = _rope(kn, cos2, sin2, KV_WIDTH)
    v_t = _dot_nt(wvt_ref[...], h)
    off += KV_WIDTH
    ones = jnp.ones((HEAD_DIM, v_t.shape[1]), F32)
    for hd in range(ATTN_KV_HEADS):
        k_ref[0, hd] = kr[:, hd * HEAD_DIM:(hd + 1) * HEAD_DIM].astype(BF16)
        v_ref[0, hd] = jnp.concatenate([v_t[hd * HEAD_DIM:(hd + 1) * HEAD_DIM], ones], axis=0).astype(BF16)
    hq_ref[0] = jax.nn.silu(cols(off, HGRN_WIDTH))
    off += HGRN_WIDTH
    ff_ref[0] = cols(off, HGRN_WIDTH)
    off += HGRN_WIDTH
    fb_ref[0] = cols(off, HGRN_WIDTH)
    off += HGRN_WIDTH
    hi_ref[0] = cols(off, HGRN_WIDTH).astype(BF16)
    off += HGRN_WIDTH
    hg_ref[0] = jax.nn.silu(cols(off, HGRN_WIDTH))


def _group_matrix(width, group):
    idx = np.arange(width) // group
    return jnp.asarray((idx[:, None] == idx[None, :]).astype(np.float32) / group, dtype=BF16)


def _rope_tables(seq_len):
    f32 = np.float32
    rows = seq_len // GRID_W
    row = np.repeat(np.arange(rows, dtype=f32), GRID_W)
    col = np.tile(np.arange(GRID_W, dtype=f32), rows)
    n_freq = HEAD_DIM // 4
    inv = (f32(ROPE_THETA) ** (-np.arange(n_freq, dtype=f32) / f32(n_freq))).astype(f32)
    ang = np.concatenate([row[:, None] * inv, col[:, None] * inv], axis=-1)
    cos, sin = np.cos(ang).astype(f32), np.sin(ang).astype(f32)
    cos2 = np.concatenate([cos, cos] * 2, axis=-1)
    sin2 = np.concatenate([-sin, sin] * 2, axis=-1)
    return jnp.asarray(cos2), jnp.asarray(sin2)


def _in_proj(x1, norm_g, w_in, q_g, k_g):
    b, s, d = x1.shape
    n_in = ATTN_WIDTH + 2 * KV_WIDTH + 5 * HGRN_WIDTH
    assert s % PROJ_ROWS == 0 and w_in.shape[2] == n_in + 2 * d
    cos2, sin2 = _rope_tables(s)
    tm = PROJ_ROWS
    tok = lambda width: pl.BlockSpec((1, tm, width), lambda bi, i: (bi, i, 0))
    heads = lambda n, width=HEAD_DIM: pl.BlockSpec((1, n, tm, width), lambda bi, i: (bi, 0, i, 0))
    pos = pl.BlockSpec((tm, V7X_LANES), lambda bi, i: (i, 0))
    f32_tok = lambda width: jax.ShapeDtypeStruct((b, s, width), F32)
    out_shape = (
        jax.ShapeDtypeStruct((b, ATTN_HEADS, s, HEAD_DIM), BF16),
        jax.ShapeDtypeStruct((b, ATTN_KV_HEADS, s, HEAD_DIM), BF16),
        jax.ShapeDtypeStruct((b, ATTN_KV_HEADS, V7X_LANES, s), BF16),
        f32_tok(HGRN_WIDTH), f32_tok(HGRN_WIDTH), f32_tok(HGRN_WIDTH),
        jax.ShapeDtypeStruct((b, s, HGRN_WIDTH), BF16),
        f32_tok(HGRN_WIDTH),
    )
    vt_spec = pl.BlockSpec((1, ATTN_KV_HEADS, V7X_LANES, tm), lambda bi, i: (bi, 0, 0, i))
    out_specs = (heads(ATTN_HEADS), heads(ATTN_KV_HEADS), vt_spec,
                 tok(HGRN_WIDTH), tok(HGRN_WIDTH), tok(HGRN_WIDTH), tok(HGRN_WIDTH), tok(HGRN_WIDTH))
    return pl.pallas_call(
        _proj_kernel,
        grid=(b, s // tm),
        in_specs=[tok(d), _resident((1, d)), pl.BlockSpec(memory_space=pl.ANY),
                  _resident((1, ATTN_WIDTH)), _resident((1, KV_WIDTH)),
                  _resident((V7X_MXU_DIM, V7X_MXU_DIM)), _resident((KV_WIDTH, KV_WIDTH)), pos, pos],
        out_specs=out_specs,
        out_shape=out_shape,
        scratch_shapes=[pltpu.VMEM((d, n_in), BF16), pltpu.VMEM((WEIGHT_STAGE_SLOTS, WEIGHT_STAGE_ROWS, n_in), F32),
                        pltpu.SemaphoreType.DMA((WEIGHT_STAGE_SLOTS,)), pltpu.VMEM((KV_WIDTH, d), BF16)],
        compiler_params=_compiler_params(("arbitrary", "arbitrary")),
        name="in_proj",
    )(x1, norm_g, w_in, jnp.tile(q_g, (1, ATTN_HEADS)), jnp.tile(k_g, (1, ATTN_KV_HEADS)),
      _group_matrix(V7X_MXU_DIM, HEAD_DIM), _group_matrix(KV_WIDTH, HEAD_DIM), cos2, sin2)


def _attn_kernel(bounded_ref, q_ref, k_ref, vt_ref, o_ref):
    tq = q_ref.shape[2]
    s_len = k_ref.shape[2]
    q = q_ref[0].reshape(ATTN_GROUP * tq, HEAD_DIM)

    def finish(accs):
        norm = [a[:HEAD_DIM] * (1.0 / a[HEAD_DIM:HEAD_DIM + 1]) for a in accs]
        pairs = [jnp.concatenate(norm[r:r + 2], axis=0).T for r in range(0, ATTN_GROUP, 2)]
        o_ref[0] = jnp.concatenate(pairs, axis=1).astype(BF16)

    def attend(shifted):
        kc = s_len // ATTN_KEY_SPLIT
        chunks = [slice(c * kc, (c + 1) * kc) for c in range(ATTN_KEY_SPLIT)]
        if shifted:
            s_t = _dot_nt(k_ref[0, 0], q)
            p_full = jnp.exp(s_t - jnp.max(s_t, axis=0, keepdims=True)).astype(BF16)
            p_chunks = [p_full[ks] for ks in chunks]
        else:
            p_chunks = [jnp.exp(_dot_nt(k_ref[0, 0, ks, :], q)).astype(BF16) for ks in chunks]
        accs = [None] * ATTN_GROUP
        for ks, p_t in zip(chunks, p_chunks):
            vt_c = vt_ref[0, 0, :, ks]
            for r in range(ATTN_GROUP):
                part = _dot(vt_c, p_t[:, r * tq:(r + 1) * tq])
                accs[r] = part if accs[r] is None else accs[r] + part
        finish(accs)

    pl.when(bounded_ref[0] == 1)(functools.partial(attend, False))
    pl.when(bounded_ref[0] != 1)(functools.partial(attend, True))


def _attention(bounded, q, k, v):
    b, _, s, _ = q.shape
    tq = ATTN_Q_ROWS
    assert s % tq == 0 and s % (ATTN_KEY_SPLIT * V7X_LANES) == 0
    k_spec = pl.BlockSpec((1, 1, s, HEAD_DIM), lambda bi, g, i, *_: (bi, g, 0, 0))
    vt_spec = pl.BlockSpec((1, 1, V7X_LANES, s), lambda bi, g, i, *_: (bi, g, 0, 0))
    grid_spec = pltpu.PrefetchScalarGridSpec(
        num_scalar_prefetch=1,
        grid=(b, ATTN_KV_HEADS, s // tq),
        in_specs=[pl.BlockSpec((1, ATTN_GROUP, tq, HEAD_DIM), lambda bi, g, i, *_: (bi, g, i, 0)),
                  k_spec, vt_spec],
        out_specs=pl.BlockSpec((1, tq, ATTN_GROUP * HEAD_DIM), lambda bi, g, i, *_: (bi, i, g)),
    )
    return pl.pallas_call(
        _attn_kernel,
        grid_spec=grid_spec,
        out_shape=jax.ShapeDtypeStruct((b, s, ATTN_WIDTH), BF16),
        compiler_params=_compiler_params(("parallel", "parallel", "parallel")),
        name="attention",
    )(bounded, q, k, v)


def _scores_bounded(q_g, k_g):
    bound = (HEAD_DIM ** 0.5) * jnp.max(jnp.abs(q_g)) * jnp.max(jnp.abs(k_g))
    return (bound <= ATTN_SCORE_BOUND).astype(jnp.int32).reshape(1)


def _hgrn_prepare(qt, f_logit, lb, reverse):
    rows = qt.shape[0]
    c = HGRN_CHUNK
    n_chunks = rows // c
    f = lb + (1.0 - lb) * jax.nn.sigmoid(f_logit)
    kk = 1.0 - f
    log_f = jnp.log(f)
    r_i = lax.broadcasted_iota(jnp.int32, (rows, rows), 0)
    c_i = lax.broadcasted_iota(jnp.int32, (rows, rows), 1)
    same_chunk = (r_i // c) == (c_i // c)
    causal = same_chunk & ((c_i >= r_i) if reverse else (c_i <= r_i))
    tri = jnp.where(causal, 1.0, 0.0).astype(BF16)
    hi, mid, lo = _split3(log_f)
    bcum = _dot(tri, hi) + _dot(tri, mid) + _dot(tri, lo)
    b3 = bcum.reshape(n_chunks, c, bcum.shape[-1])
    mid_row = c // 2 if reverse else c // 2 - 1
    last_row = 0 if reverse else c - 1
    b_mid = b3[:, mid_row:mid_row + 1, :]
    b_last = b3[:, last_row:last_row + 1, :]
    q3 = qt.reshape(b3.shape)
    k3 = kk.reshape(b3.shape)
    d_mid = b3 - b_mid
    q_mid = q3 * jnp.exp(d_mid)
    k_mid = k3 * jnp.exp(-d_mid)
    k_end = k_mid * jnp.exp(b_last - b_mid)
    q_in = q_mid * jnp.exp(b_mid)
    flat = lambda a: a.reshape(bcum.shape).astype(BF16)
    sub = HGRN_INTRA_ROWS
    causal_sub = jnp.concatenate([causal[:sub, :sub]] * 2, axis=1)
    return dict(causal_sub=causal_sub, q_mid=flat(q_mid), k_mid=flat(k_mid), k_end=flat(k_end), q_in=flat(q_in),
                decay=jnp.exp(b_last))


def _pair_slice(p):
    return slice(p * HGRN_PAIR, (p + 1) * HGRN_PAIR)


def _hgrn_intra(ops, v, p):
    rows = v.shape[0]
    sub = HGRN_INTRA_ROWS
    lower_head = lax.broadcasted_iota(jnp.int32, (sub, HGRN_PAIR), 1) < HGRN_DIM
    zero_bf = jnp.zeros((), BF16)
    stack = lambda x: jnp.concatenate([jnp.where(lower_head, x, zero_bf), jnp.where(lower_head, zero_bf, x)], axis=0)
    causal = ops["causal_sub"]
    sl = _pair_slice(p)
    outs = []
    for h in range(rows // sub):
        rs = slice(h * sub, (h + 1) * sub)
        scores = _dot_nt(ops["q_mid"][rs, sl], stack(ops["k_mid"][rs, sl]))
        scores = jnp.where(causal, scores, 0.0).astype(BF16)
        outs.append(_dot(scores, stack(v[rs, sl])))
    return jnp.concatenate(outs, axis=0)


def _hgrn_contribs(ops, v):
    c = HGRN_CHUNK
    n_chunks = v.shape[0] // c
    return [[_dot_tn(v[n * c:(n + 1) * c, _pair_slice(p)], ops["k_end"][n * c:(n + 1) * c, _pair_slice(p)])
             for n in range(n_chunks)] for p in range(HGRN_PAIRS)]


def _hgrn_scan(ops, contribs, state_ref, reverse):
    n_chunks = len(contribs[0])
    sr = lax.broadcasted_iota(jnp.int32, (HGRN_PAIR, HGRN_PAIR), 0)
    sc = lax.broadcasted_iota(jnp.int32, (HGRN_PAIR, HGRN_PAIR), 1)
    same_head = (sr // HGRN_DIM) == (sc // HGRN_DIM)
    order = range(n_chunks - 1, -1, -1) if reverse else range(n_chunks)
    states = [[None] * n_chunks for _ in range(HGRN_PAIRS)]
    for p in range(HGRN_PAIRS):
        st = state_ref[p]
        for n in order:
            states[p][n] = st.astype(BF16)
            st = ops["decay"][n, :, _pair_slice(p)] * st + jnp.where(same_head, contribs[p][n], 0.0)
        state_ref[p] = st
    return states


def _hgrn_inter(ops, states, p):
    c = HGRN_CHUNK
    parts = [_dot_nt(ops["q_in"][n * c:(n + 1) * c, _pair_slice(p)], st) for n, st in enumerate(states[p])]
    return jnp.concatenate(parts, axis=0)


def _hgrn_kernel(lbl_ref, qf_ref, ff_ref, vf_ref, qb_ref, fb_ref, vb_ref, of_ref, ob_ref,
                 st_f_ref, st_b_ref):
    @pl.when(pl.program_id(1) == 0)
    def _():
        st_f_ref[...] = jnp.zeros_like(st_f_ref)
        st_b_ref[...] = jnp.zeros_like(st_b_ref)

    logits = lbl_ref[...]
    e = jnp.exp(logits - jnp.max(logits, axis=1, keepdims=True))
    lb = e[:, 0, :] / jnp.sum(e, axis=1)
    v_f, v_b = vf_ref[0], vb_ref[0]
    ops_f = _hgrn_prepare(qf_ref[0], ff_ref[0], lb[0:1], False)
    ops_b = _hgrn_prepare(qb_ref[0], fb_ref[0], lb[1:2], True)
    con_f = _hgrn_contribs(ops_f, v_f)
    con_b = _hgrn_contribs(ops_b, v_b)
    states_f = _hgrn_scan(ops_f, con_f, st_f_ref, False)
    states_b = _hgrn_scan(ops_b, con_b, st_b_ref, True)
    pairs = range(HGRN_PAIRS)
    inter_f = [_hgrn_inter(ops_f, states_f, p) for p in pairs]
    inter_b = [_hgrn_inter(ops_b, states_b, p) for p in pairs]
    intra_f = [_hgrn_intra(ops_f, v_f, p) for p in pairs]
    intra_b = [_hgrn_intra(ops_b, v_b, p) for p in pairs]
    of_ref[0] = jnp.concatenate([a + b for a, b in zip(intra_f, inter_f)], axis=1)
    ob_ref[0] = jnp.concatenate([a + b for a, b in zip(intra_b, inter_b)], axis=1)


def _hgrn(lb_logits, hq, hff, hfb, hi):
    b, s, w = hq.shape
    r = HGRN_ROWS
    assert s % r == 0 and r % HGRN_INTRA_ROWS == 0 and HGRN_INTRA_ROWS % HGRN_CHUNK == 0
    nb = s // r
    fwd = pl.BlockSpec((1, r, w), lambda bi, j: (bi, j, 0))
    bwd = pl.BlockSpec((1, r, w), lambda bi, j: (bi, nb - 1 - j, 0))
    state = pltpu.VMEM((HGRN_PAIRS, HGRN_PAIR, HGRN_PAIR), F32)
    return pl.pallas_call(
        _hgrn_kernel,
        grid=(b, nb),
        in_specs=[_resident(lb_logits.shape), fwd, fwd, fwd, bwd, bwd, bwd],
        out_specs=(fwd, bwd),
        out_shape=(jax.ShapeDtypeStruct((b, s, w), F32), jax.ShapeDtypeStruct((b, s, w), F32)),
        scratch_shapes=[state, state],
        compiler_params=_compiler_params(("parallel", "arbitrary")),
        name="hgrn",
    )(lb_logits, hq, hff, hi, hq, hfb, hi)


def _merge_kernel(x_ref, ya_ref, of_ref, ob_ref, hg_ref, mg_ref, win_hbm, ng_ref, gm_ref,
                  wa_hbm, wb_hbm, wo_hbm, o_ref, wg_ref, wa_ref, wb_ref, wo_ref, stage, sem):
    @pl.when(pl.program_id(0) == 0)
    def _():
        n_gate = wg_ref.shape[1]
        n_mix = win_hbm.shape[2] - n_gate
        _load_as_bf16(lambda r, n: win_hbm.at[0, pl.ds(r, n), pl.ds(n_mix, n_gate)], wg_ref.shape[0], wg_ref, stage, sem)
        for hbm, ref in ((wa_hbm, wa_ref), (wb_hbm, wb_ref), (wo_hbm, wo_ref)):
            _load_as_bf16(lambda r, n, hbm=hbm: hbm.at[0, pl.ds(r, n), :], ref.shape[0], ref, stage, sem)

    x = x_ref[...]
    d = x.shape[-1]
    h = _rmsnorm(x, mg_ref[...]).astype(BF16)
    gate_a = jax.nn.sigmoid(_dot(h, wg_ref[:, :d]))
    gate_b = jax.nn.sigmoid(_dot(h, wg_ref[:, d:]))
    o = of_ref[...] + ob_ref[...]
    on = o * lax.rsqrt(_group_mean(o * o, gm_ref[...]) + NORM_EPS) * ng_ref[...]
    yb = (on * hg_ref[...]).astype(BF16)
    merged = gate_a * _dot(ya_ref[...], wa_ref[...]) + gate_b * _dot(yb, wb_ref[...])
    o_ref[...] = x + _dot(merged.astype(BF16), wo_ref[...])


def _merge(x1, y_attn, o_f, o_b, hg, mix_g, w_in, norm_g, w_a, w_b, w_out):
    t, d = x1.shape
    tm = MERGE_ROWS
    assert t % tm == 0
    row = lambda width: pl.BlockSpec((tm, width), lambda i: (i, 0))
    hbm = pl.BlockSpec(memory_space=pl.ANY)
    return pl.pallas_call(
        _merge_kernel,
        grid=(t // tm,),
        in_specs=[row(d), row(ATTN_WIDTH), row(HGRN_WIDTH), row(HGRN_WIDTH), row(HGRN_WIDTH),
                  _resident((1, d)), hbm,
                  _resident((1, HGRN_WIDTH)), _resident((V7X_MXU_DIM, V7X_MXU_DIM)), hbm, hbm, hbm],
        out_specs=row(d),
        out_shape=jax.ShapeDtypeStruct((t, d), F32),
        scratch_shapes=[pltpu.VMEM((d, 2 * d), BF16), pltpu.VMEM((ATTN_WIDTH, d), BF16),
                        pltpu.VMEM((HGRN_WIDTH, d), BF16), pltpu.VMEM((d, d), BF16),
                        pltpu.VMEM((WEIGHT_STAGE_SLOTS, WEIGHT_STAGE_ROWS, 2 * d), F32),
                        pltpu.SemaphoreType.DMA((WEIGHT_STAGE_SLOTS,))],
        compiler_params=_compiler_params(("arbitrary",)),
        name="merge",
    )(x1, y_attn, o_f, o_b, hg, mix_g, w_in, norm_g, _group_matrix(V7X_MXU_DIM, HGRN_DIM), w_a, w_b, w_out)


def kernel(x, ffn1_norm_g, ffn1_w_gate, ffn1_w_up, ffn1_w_down, mix_norm_g, w_in, q_norm_g, k_norm_g,
           hgrn_lb_logits, hgrn_out_norm_g, w_branch_attn, w_branch_hgrn, w_out, ffn2_norm_g,
           ffn2_w_gate, ffn2_w_up, ffn2_w_down, final_norm_g):
    b, s, d = x.shape
    t = b * s
    depth = w_in.shape[0]
    assert depth == 1 and hgrn_lb_logits.shape[1] == depth + 1
    final_g = final_norm_g.reshape(1, d)
    x1 = _ffn(x.reshape(t, d), ffn1_norm_g, ffn1_w_gate[0], ffn1_w_up[0], ffn1_w_down[0], final_g, False)
    q, k, v, hq, hff, hfb, hi, hg = _in_proj(x1.reshape(b, s, d), mix_norm_g, w_in, q_norm_g, k_norm_g)
    y_attn = _attention(_scores_bounded(q_norm_g, k_norm_g), q, k, v)
    o_f, o_b = _hgrn(hgrn_lb_logits, hq, hff, hfb, hi)
    flat = lambda a: a.reshape(t, a.shape[-1])
    x2 = _merge(x1, flat(y_attn), flat(o_f), flat(o_b), flat(hg), mix_norm_g, w_in, hgrn_out_norm_g,
                w_branch_attn, w_branch_hgrn, w_out)
    out = _ffn(x2, ffn2_norm_g, ffn2_w_gate[0], ffn2_w_up[0], ffn2_w_down[0], final_g, True)
    return out.reshape(b, s, d)
```

```python
import functools

import jax
import jax.numpy as jnp
import numpy as np
from jax import lax
from jax.experimental import pallas as pl
from jax.experimental.pallas import tpu as pltpu

F32 = jnp.float32
BF16 = jnp.bfloat16

NORM_EPS = 1e-6
ROPE_THETA = 10000.0
GRID_W = 64
HEAD_DIM = 64
ATTN_HEADS = 8
ATTN_KV_HEADS = 2
ATTN_GROUP = ATTN_HEADS // ATTN_KV_HEADS
HGRN_HEADS = 8
HGRN_DIM = 64
HGRN_CHUNK = 32
ATTN_WIDTH = ATTN_HEADS * HEAD_DIM
KV_WIDTH = ATTN_KV_HEADS * HEAD_DIM
HGRN_WIDTH = HGRN_HEADS * HGRN_DIM
HGRN_PAIR = 2 * HGRN_DIM
HGRN_PAIRS = HGRN_HEADS // 2

V7X_LANES = 128
V7X_MXU_DIM = 256
V7X_VMEM_LIMIT_BYTES = 56 * 1024 * 1024

FFN_ROWS = 512
FFN_COLS = V7X_MXU_DIM
PROJ_ROWS = 1024
ATTN_Q_ROWS = 512
ATTN_KEY_SPLIT = 4
ATTN_SCORE_BOUND = 60.0
HGRN_ROWS = 256
HGRN_INTRA_ROWS = V7X_MXU_DIM // 2
MERGE_ROWS = 1024
WEIGHT_STAGE_ROWS = 64
WEIGHT_STAGE_SLOTS = 4


def _compiler_params(semantics):
    return pltpu.CompilerParams(dimension_semantics=semantics, vmem_limit_bytes=V7X_VMEM_LIMIT_BYTES)


def _resident(shape):
    zeros = (0,) * len(shape)
    return pl.BlockSpec(shape, lambda *_: zeros, pipeline_mode=pl.Buffered(1))


def _rmsnorm(x, g):
    return x * lax.rsqrt(jnp.mean(x * x, axis=-1, keepdims=True) + NORM_EPS) * g


def _dot(a, b):
    return jnp.dot(a, b, preferred_element_type=F32)


def _dot_nt(a, b):
    return lax.dot_general(a, b, (((1,), (1,)), ((), ())), preferred_element_type=F32)


def _dot_tn(a, b):
    return lax.dot_general(a, b, (((0,), (0,)), ((), ())), preferred_element_type=F32)


def _split2(x):
    hi = x.astype(BF16)
    lo = (x - hi.astype(F32)).astype(BF16)
    return hi, lo


def _split3(x):
    hi = x.astype(BF16)
    r = x - hi.astype(F32)
    mid = r.astype(BF16)
    lo = (r - mid.astype(F32)).astype(BF16)
    return hi, mid, lo


def _group_mean(x, gmat):
    w = gmat.shape[0]
    hi, lo = _split2(x)
    tiles = [_dot(hi[:, i:i + w], gmat) + _dot(lo[:, i:i + w], gmat) for i in range(0, x.shape[1], w)]
    return tiles[0] if len(tiles) == 1 else jnp.concatenate(tiles, axis=1)


def _load_as_bf16(src_rows, n_rows, dst, stage, sem):
    depth, chunk, _ = stage.shape
    cols = dst.shape[1]
    n = n_rows // chunk
    assert n_rows % chunk == 0 and cols <= stage.shape[2]

    def copy(k):
        slot = k % depth
        return pltpu.make_async_copy(src_rows(k * chunk, chunk), stage.at[slot, :, pl.ds(0, cols)], sem.at[slot])

    for k in range(min(depth - 1, n)):
        copy(k).start()
    for k in range(n):
        if k + depth - 1 < n:
            copy(k + depth - 1).start()
        copy(k).wait()
        dst[pl.ds(k * chunk, chunk), :] = stage[k % depth, :, :cols].astype(BF16)


def _ffn_kernel(x_ref, g_ref, wg_hbm, wu_hbm, wd_hbm, fg_ref, o_ref,
                wg_ref, wu_ref, wd_ref, stage_g, stage_u, stage_d, sem, *, final_norm):
    d_ff = wg_ref.shape[1]
    n_chunks = d_ff // FFN_COLS

    def chunk_copies(c):
        cols = pl.ds(c * FFN_COLS, FFN_COLS)
        slot = c % 2
        return (pltpu.make_async_copy(wg_hbm.at[:, cols], stage_g.at[slot], sem.at[0, slot]),
                pltpu.make_async_copy(wu_hbm.at[:, cols], stage_u.at[slot], sem.at[1, slot]),
                pltpu.make_async_copy(wd_hbm.at[cols, :], stage_d.at[slot], sem.at[2, slot]))

    def tile(stream_weights):
        if stream_weights:
            for cp in chunk_copies(0):
                cp.start()
        x = x_ref[...]
        xn = _rmsnorm(x, g_ref[...]).astype(BF16)
        acc = None
        for c in range(n_chunks):
            sl = slice(c * FFN_COLS, (c + 1) * FFN_COLS)
            if stream_weights:
                if c + 1 < n_chunks:
                    for cp in chunk_copies(c + 1):
                        cp.start()
                for cp in chunk_copies(c):
                    cp.wait()
                wg_ref[:, sl] = stage_g[c % 2].astype(BF16)
                wu_ref[:, sl] = stage_u[c % 2].astype(BF16)
                wd_ref[sl, :] = stage_d[c % 2].astype(BF16)
            gate = _dot(xn, wg_ref[:, sl])
            up = _dot(xn, wu_ref[:, sl])
            act = (jax.nn.silu(gate) * up).astype(BF16)
            part = _dot(act, wd_ref[sl, :])
            acc = part if acc is None else acc + part
        y = x + 0.5 * acc
        if final_norm:
            y = _rmsnorm(y, fg_ref[...])
        o_ref[...] = y

    pl.when(pl.program_id(0) == 0)(functools.partial(tile, True))
    pl.when(pl.program_id(0) != 0)(functools.partial(tile, False))


def _ffn(x, norm_g, w_gate, w_up, w_down, final_g, final_norm):
    t, d = x.shape
    d_ff = w_gate.shape[1]
    assert t % FFN_ROWS == 0 and d_ff % FFN_COLS == 0
    row = pl.BlockSpec((FFN_ROWS, d), lambda i: (i, 0))
    hbm = pl.BlockSpec(memory_space=pl.ANY)
    return pl.pallas_call(
        functools.partial(_ffn_kernel, final_norm=final_norm),
        grid=(t // FFN_ROWS,),
        in_specs=[row, _resident((1, d)), hbm, hbm, hbm, _resident((1, d))],
        out_specs=row,
        out_shape=jax.ShapeDtypeStruct((t, d), F32),
        scratch_shapes=[pltpu.VMEM((d, d_ff), BF16), pltpu.VMEM((d, d_ff), BF16), pltpu.VMEM((d_ff, d), BF16),
                        pltpu.VMEM((2, d, FFN_COLS), F32), pltpu.VMEM((2, d, FFN_COLS), F32),
                        pltpu.VMEM((2, FFN_COLS, d), F32), pltpu.SemaphoreType.DMA((3, 2))],
        compiler_params=_compiler_params(("arbitrary",)),
        name="ffn_final" if final_norm else "ffn",
    )(x, norm_g, w_gate, w_up, w_down, final_g)


def _rope(xn, cos_t, sin_t, width):
    half = HEAD_DIM // 2
    lane = lax.broadcasted_iota(jnp.int32, xn.shape, 1)
    first_half = (lane % HEAD_DIM) < half
    partner = jnp.where(first_half, pltpu.roll(xn, width - half, 1), pltpu.roll(xn, half, 1))
    return xn * cos_t + partner * sin_t


def _proj_kernel(x_ref, g_ref, w_hbm, qgt_ref, kg_ref, gk_ref, cos_ref, sin_ref, cost_ref, sint_ref,
                 q_ref, k_ref, v_ref, hq_ref, ff_ref, fb_ref, hi_ref, hg_ref, w_ref, stage, sem, wqvt_ref):
    @pl.when((pl.program_id(0) == 0) & (pl.program_id(1) == 0))
    def _():
        n_in = w_ref.shape[1]
        _load_as_bf16(lambda r, n: w_hbm.at[0, pl.ds(r, n), pl.ds(0, n_in)], w_ref.shape[0], w_ref, stage, sem)
        v0 = ATTN_WIDTH + KV_WIDTH
        wqvt_ref[:ATTN_WIDTH, :] = w_ref[:, :ATTN_WIDTH].astype(F32).T.astype(BF16)
        wqvt_ref[ATTN_WIDTH:, :] = w_ref[:, v0:v0 + KV_WIDTH].astype(F32).T.astype(BF16)

    h = _rmsnorm(x_ref[0], g_ref[...]).astype(BF16)
    rows = h.shape[0]

    def cols(start, width):
        return _dot(h, w_ref[:, start:start + width])

    qv_t = _dot_nt(wqvt_ref[...], h)
    q3 = qv_t[:ATTN_WIDTH].reshape(ATTN_HEADS, HEAD_DIM, rows)
    qn = q3 * lax.rsqrt(jnp.mean(q3 * q3, axis=1, keepdims=True) + NORM_EPS) * qgt_ref[...]
    half = HEAD_DIM // 2
    x1, x2 = qn[:, :half], qn[:, half:]
    cos_t, sin_t = cost_ref[...], sint_ref[...]
    qr = jnp.concatenate([x1 * cos_t - x2 * sin_t, x2 * cos_t + x1 * sin_t], axis=1) * (HEAD_DIM ** -0.5)
    q_ref[0] = qr.astype(BF16)
    off = ATTN_WIDTH
    k = cols(off, KV_WIDTH)
    off += KV_WIDTH
    kn = k * lax.rsqrt(_group_mean(k * k, gk_ref[...]) + NORM_EPS) * kg_ref[...]
    kr = _rope(kn, cos_ref[...], sin_ref[...], KV_WIDTH)
    off += KV_WIDTH
    ones = jnp.ones((HEAD_DIM, rows), F32)
    for hd in range(ATTN_KV_HEADS):
        k_ref[0, hd] = kr[:, hd * HEAD_DIM:(hd + 1) * HEAD_DIM].astype(BF16)
        v_hd = qv_t[ATTN_WIDTH + hd * HEAD_DIM:ATTN_WIDTH + (hd + 1) * HEAD_DIM]
        v_ref[0, hd] = jnp.concatenate([v_hd, ones], axis=0).astype(BF16)
    hq_ref[0] = jax.nn.silu(cols(off, HGRN_WIDTH))
    off += HGRN_WIDTH
    ff_ref[0] = cols(off, HGRN_WIDTH)
    off += HGRN_WIDTH
    fb_ref[0] = cols(off, HGRN_WIDTH)
    off += HGRN_WIDTH
    hi_ref[0] = cols(off, HGRN_WIDTH).astype(BF16)
    off += HGRN_WIDTH
    hg_ref[0] = jax.nn.silu(cols(off, HGRN_WIDTH))


def _group_matrix(width, group):
    idx = np.arange(width) // group
    return jnp.asarray((idx[:, None] == idx[None, :]).astype(np.float32) / group, dtype=BF16)


def _rope_tables(seq_len):
    f32 = np.float32
    rows = seq_len // GRID_W
    row = np.repeat(np.arange(rows, dtype=f32), GRID_W)
    col = np.tile(np.arange(GRID_W, dtype=f32), rows)
    n_freq = HEAD_DIM // 4
    inv = (f32(ROPE_THETA) ** (-np.arange(n_freq, dtype=f32) / f32(n_freq))).astype(f32)
    ang = np.concatenate([row[:, None] * inv, col[:, None] * inv], axis=-1)
    cos, sin = np.cos(ang).astype(f32), np.sin(ang).astype(f32)
    cos2 = np.concatenate([cos, cos] * 2, axis=-1)
    sin2 = np.concatenate([-sin, sin] * 2, axis=-1)
    return (jnp.asarray(cos2), jnp.asarray(sin2),
            jnp.asarray(np.ascontiguousarray(cos.T)), jnp.asarray(np.ascontiguousarray(sin.T)))


def _in_proj(x1, norm_g, w_in, q_g, k_g):
    b, s, d = x1.shape
    n_in = ATTN_WIDTH + 2 * KV_WIDTH + 5 * HGRN_WIDTH
    assert s % PROJ_ROWS == 0 and w_in.shape[2] == n_in + 2 * d
    cos2, sin2, cos_t, sin_t = _rope_tables(s)
    tm = PROJ_ROWS
    tok = lambda width: pl.BlockSpec((1, tm, width), lambda bi, i: (bi, i, 0))
    heads = lambda n, width=HEAD_DIM: pl.BlockSpec((1, n, tm, width), lambda bi, i: (bi, 0, i, 0))
    pos = pl.BlockSpec((tm, V7X_LANES), lambda bi, i: (i, 0))
    pos_t = pl.BlockSpec((HEAD_DIM // 2, tm), lambda bi, i: (0, i))
    f32_tok = lambda width: jax.ShapeDtypeStruct((b, s, width), F32)
    out_shape = (
        jax.ShapeDtypeStruct((b, ATTN_HEADS, HEAD_DIM, s), BF16),
        jax.ShapeDtypeStruct((b, ATTN_KV_HEADS, s, HEAD_DIM), BF16),
        jax.ShapeDtypeStruct((b, ATTN_KV_HEADS, V7X_LANES, s), BF16),
        f32_tok(HGRN_WIDTH), f32_tok(HGRN_WIDTH), f32_tok(HGRN_WIDTH),
        jax.ShapeDtypeStruct((b, s, HGRN_WIDTH), BF16),
        f32_tok(HGRN_WIDTH),
    )
    vt_spec = pl.BlockSpec((1, ATTN_KV_HEADS, V7X_LANES, tm), lambda bi, i: (bi, 0, 0, i))
    qt_spec = pl.BlockSpec((1, ATTN_HEADS, HEAD_DIM, tm), lambda bi, i: (bi, 0, 0, i))
    out_specs = (qt_spec, heads(ATTN_KV_HEADS), vt_spec,
                 tok(HGRN_WIDTH), tok(HGRN_WIDTH), tok(HGRN_WIDTH), tok(HGRN_WIDTH), tok(HGRN_WIDTH))
    return pl.pallas_call(
        _proj_kernel,
        grid=(b, s // tm),
        in_specs=[tok(d), _resident((1, d)), pl.BlockSpec(memory_space=pl.ANY),
                  _resident((HEAD_DIM, tm)), _resident((1, KV_WIDTH)),
                  _resident((KV_WIDTH, KV_WIDTH)), pos, pos, pos_t, pos_t],
        out_specs=out_specs,
        out_shape=out_shape,
        scratch_shapes=[pltpu.VMEM((d, n_in), BF16), pltpu.VMEM((WEIGHT_STAGE_SLOTS, WEIGHT_STAGE_ROWS, n_in), F32),
                        pltpu.SemaphoreType.DMA((WEIGHT_STAGE_SLOTS,)),
                        pltpu.VMEM((ATTN_WIDTH + KV_WIDTH, d), BF16)],
        compiler_params=_compiler_params(("arbitrary", "arbitrary")),
        name="in_proj",
    )(x1, norm_g, w_in, jnp.broadcast_to(q_g.reshape(HEAD_DIM, 1), (HEAD_DIM, tm)), jnp.tile(k_g, (1, ATTN_KV_HEADS)),
      _group_matrix(KV_WIDTH, HEAD_DIM), cos2, sin2, cos_t, sin_t)


def _attn_kernel(bounded_ref, q_ref, k_ref, vt_ref, o_ref):
    tq = q_ref.shape[3]
    s_len = k_ref.shape[2]
    q_t = jnp.concatenate([q_ref[0, r] for r in range(ATTN_GROUP)], axis=1)

    def finish(accs):
        norm = [a[:HEAD_DIM] * (1.0 / a[HEAD_DIM:HEAD_DIM + 1]) for a in accs]
        pairs = [jnp.concatenate(norm[r:r + 2], axis=0).T for r in range(0, ATTN_GROUP, 2)]
        o_ref[0] = jnp.concatenate(pairs, axis=1).astype(BF16)

    def attend(shifted):
        kc = s_len // ATTN_KEY_SPLIT
        chunks = [slice(c * kc, (c + 1) * kc) for c in range(ATTN_KEY_SPLIT)]
        if shifted:
            s_t = _dot(k_ref[0, 0], q_t)
            p_full = jnp.exp(s_t - jnp.max(s_t, axis=0, keepdims=True)).astype(BF16)
            p_chunks = [p_full[ks] for ks in chunks]
        else:
            p_chunks = [jnp.exp(_dot(k_ref[0, 0, ks, :], q_t)).astype(BF16) for ks in chunks]
        accs = [None] * ATTN_GROUP
        for ks, p_t in zip(chunks, p_chunks):
            vt_c = vt_ref[0, 0, :, ks]
            for r in range(ATTN_GROUP):
                part = _dot(vt_c, p_t[:, r * tq:(r + 1) * tq])
                accs[r] = part if accs[r] is None else accs[r] + part
        finish(accs)

    pl.when(bounded_ref[0] == 1)(functools.partial(attend, False))
    pl.when(bounded_ref[0] != 1)(functools.partial(attend, True))


def _attention(bounded, q, k, v):
    b, _, _, s = q.shape
    tq = ATTN_Q_ROWS
    assert s % tq == 0 and s % (ATTN_KEY_SPLIT * V7X_LANES) == 0
    k_spec = pl.BlockSpec((1, 1, s, HEAD_DIM), lambda bi, g, i, *_: (bi, g, 0, 0))
    vt_spec = pl.BlockSpec((1, 1, V7X_LANES, s), lambda bi, g, i, *_: (bi, g, 0, 0))
    grid_spec = pltpu.PrefetchScalarGridSpec(
        num_scalar_prefetch=1,
        grid=(b, ATTN_KV_HEADS, s // tq),
        in_specs=[pl.BlockSpec((1, ATTN_GROUP, HEAD_DIM, tq), lambda bi, g, i, *_: (bi, g, 0, i)),
                  k_spec, vt_spec],
        out_specs=pl.BlockSpec((1, tq, ATTN_GROUP * HEAD_DIM), lambda bi, g, i, *_: (bi, i, g)),
    )
    return pl.pallas_call(
        _attn_kernel,
        grid_spec=grid_spec,
        out_shape=jax.ShapeDtypeStruct((b, s, ATTN_WIDTH), BF16),
        compiler_params=_compiler_params(("parallel", "parallel", "parallel")),
        name="attention",
    )(bounded, q, k, v)


def _scores_bounded(q_g, k_g):
    bound = (HEAD_DIM ** 0.5) * jnp.max(jnp.abs(q_g)) * jnp.max(jnp.abs(k_g))
    return (bound <= ATTN_SCORE_BOUND).astype(jnp.int32).reshape(1)


def _hgrn_prepare(qt, f_logit, lb, reverse):
    rows = qt.shape[0]
    c = HGRN_CHUNK
    n_chunks = rows // c
    f = lb + (1.0 - lb) * jax.nn.sigmoid(f_logit)
    kk = 1.0 - f
    log_f = jnp.log(f)
    r_i = lax.broadcasted_iota(jnp.int32, (rows, rows), 0)
    c_i = lax.broadcasted_iota(jnp.int32, (rows, rows), 1)
    same_chunk = (r_i // c) == (c_i // c)
    causal = same_chunk & ((c_i >= r_i) if reverse else (c_i <= r_i))
    tri = jnp.where(causal, 1.0, 0.0).astype(BF16)
    hi, mid, lo = _split3(log_f)
    bcum = _dot(tri, hi) + _dot(tri, mid) + _dot(tri, lo)
    b3 = bcum.reshape(n_chunks, c, bcum.shape[-1])
    mid_row = c // 2 if reverse else c // 2 - 1
    last_row = 0 if reverse else c - 1
    b_mid = b3[:, mid_row:mid_row + 1, :]
    b_last = b3[:, last_row:last_row + 1, :]
    q3 = qt.reshape(b3.shape)
    k3 = kk.reshape(b3.shape)
    d_mid = b3 - b_mid
    q_mid = q3 * jnp.exp(d_mid)
    k_mid = k3 * jnp.exp(-d_mid)
    k_end = k_mid * jnp.exp(b_last - b_mid)
    q_in = q_mid * jnp.exp(b_mid)
    flat = lambda a: a.reshape(bcum.shape).astype(BF16)
    sub = HGRN_INTRA_ROWS
    causal_sub = jnp.concatenate([causal[:sub, :sub]] * 2, axis=1)
    return dict(causal_sub=causal_sub, q_mid=flat(q_mid), k_mid=flat(k_mid), k_end=flat(k_end), q_in=flat(q_in),
                decay=jnp.exp(b_last))


def _pair_slice(p):
    return slice(p * HGRN_PAIR, (p + 1) * HGRN_PAIR)


def _hgrn_intra(ops, v, p):
    rows = v.shape[0]
    sub = HGRN_INTRA_ROWS
    lower_head = lax.broadcasted_iota(jnp.int32, (sub, HGRN_PAIR), 1) < HGRN_DIM
    zero_bf = jnp.zeros((), BF16)
    stack = lambda x: jnp.concatenate([jnp.where(lower_head, x, zero_bf), jnp.where(lower_head, zero_bf, x)], axis=0)
    causal = ops["causal_sub"]
    sl = _pair_slice(p)
    outs = []
    for h in range(rows // sub):
        rs = slice(h * sub, (h + 1) * sub)
        scores = _dot_nt(ops["q_mid"][rs, sl], stack(ops["k_mid"][rs, sl]))
        scores = jnp.where(causal, scores, 0.0).astype(BF16)
        outs.append(_dot(scores, stack(v[rs, sl])))
    return jnp.concatenate(outs, axis=0)


def _hgrn_contribs(ops, v):
    c = HGRN_CHUNK
    n_chunks = v.shape[0] // c
    return [[_dot_tn(v[n * c:(n + 1) * c, _pair_slice(p)], ops["k_end"][n * c:(n + 1) * c, _pair_slice(p)])
             for n in range(n_chunks)] for p in range(HGRN_PAIRS)]


def _hgrn_scan(ops, contribs, state_ref, reverse):
    n_chunks = len(contribs[0])
    sr = lax.broadcasted_iota(jnp.int32, (HGRN_PAIR, HGRN_PAIR), 0)
    sc = lax.broadcasted_iota(jnp.int32, (HGRN_PAIR, HGRN_PAIR), 1)
    same_head = (sr // HGRN_DIM) == (sc // HGRN_DIM)
    order = range(n_chunks - 1, -1, -1) if reverse else range(n_chunks)
    states = [[None] * n_chunks for _ in range(HGRN_PAIRS)]
    for p in range(HGRN_PAIRS):
        st = state_ref[p]
        for n in order:
            states[p][n] = st.astype(BF16)
            st = ops["decay"][n, :, _pair_slice(p)] * st + jnp.where(same_head, contribs[p][n], 0.0)
        state_ref[p] = st
    return states


def _hgrn_inter(ops, states, p):
    c = HGRN_CHUNK
    parts = [_dot_nt(ops["q_in"][n * c:(n + 1) * c, _pair_slice(p)], st) for n, st in enumerate(states[p])]
    return jnp.concatenate(parts, axis=0)


def _hgrn_kernel(lbl_ref, qf_ref, ff_ref, vf_ref, qb_ref, fb_ref, vb_ref, of_ref, ob_ref,
                 st_f_ref, st_b_ref):
    @pl.when(pl.program_id(1) == 0)
    def _():
        st_f_ref[...] = jnp.zeros_like(st_f_ref)
        st_b_ref[...] = jnp.zeros_like(st_b_ref)

    logits = lbl_ref[...]
    e = jnp.exp(logits - jnp.max(logits, axis=1, keepdims=True))
    lb = e[:, 0, :] / jnp.sum(e, axis=1)
    v_f, v_b = vf_ref[0], vb_ref[0]
    ops_f = _hgrn_prepare(qf_ref[0], ff_ref[0], lb[0:1], False)
    ops_b = _hgrn_prepare(qb_ref[0], fb_ref[0], lb[1:2], True)
    con_f = _hgrn_contribs(ops_f, v_f)
    con_b = _hgrn_contribs(ops_b, v_b)
    states_f = _hgrn_scan(ops_f, con_f, st_f_ref, False)
    states_b = _hgrn_scan(ops_b, con_b, st_b_ref, True)
    pairs = range(HGRN_PAIRS)
    inter_f = [_hgrn_inter(ops_f, states_f, p) for p in pairs]
    inter_b = [_hgrn_inter(ops_b, states_b, p) for p in pairs]
    intra_f = [_hgrn_intra(ops_f, v_f, p) for p in pairs]
    intra_b = [_hgrn_intra(ops_b, v_b, p) for p in pairs]
    of_ref[0] = jnp.concatenate([a + b for a, b in zip(intra_f, inter_f)], axis=1)
    ob_ref[0] = jnp.concatenate([a + b for a, b in zip(intra_b, inter_b)], axis=1)


def _hgrn(lb_logits, hq, hff, hfb, hi):
    b, s, w = hq.shape
    r = HGRN_ROWS
    assert s % r == 0 and r % HGRN_INTRA_ROWS == 0 and HGRN_INTRA_ROWS % HGRN_CHUNK == 0
    nb = s // r
    fwd = pl.BlockSpec((1, r, w), lambda bi, j: (bi, j, 0))
    bwd = pl.BlockSpec((1, r, w), lambda bi, j: (bi, nb - 1 - j, 0))
    state = pltpu.VMEM((HGRN_PAIRS, HGRN_PAIR, HGRN_PAIR), F32)
    return pl.pallas_call(
        _hgrn_kernel,
        grid=(b, nb),
        in_specs=[_resident(lb_logits.shape), fwd, fwd, fwd, bwd, bwd, bwd],
        out_specs=(fwd, bwd),
        out_shape=(jax.ShapeDtypeStruct((b, s, w), F32), jax.ShapeDtypeStruct((b, s, w), F32)),
        scratch_shapes=[state, state],
        compiler_params=_compiler_params(("parallel", "arbitrary")),
        name="hgrn",
    )(lb_logits, hq, hff, hi, hq, hfb, hi)


def _merge_kernel(x_ref, ya_ref, of_ref, ob_ref, hg_ref, mg_ref, win_hbm, ng_ref, gm_ref,
                  wa_hbm, wb_hbm, wo_hbm, o_ref, wg_ref, wa_ref, wb_ref, wo_ref, stage, sem):
    @pl.when(pl.program_id(0) == 0)
    def _():
        n_gate = wg_ref.shape[1]
        n_mix = win_hbm.shape[2] - n_gate
        _load_as_bf16(lambda r, n: win_hbm.at[0, pl.ds(r, n), pl.ds(n_mix, n_gate)], wg_ref.shape[0], wg_ref, stage, sem)
        for hbm, ref in ((wa_hbm, wa_ref), (wb_hbm, wb_ref), (wo_hbm, wo_ref)):
            _load_as_bf16(lambda r, n, hbm=hbm: hbm.at[0, pl.ds(r, n), :], ref.shape[0], ref, stage, sem)

    x = x_ref[...]
    d = x.shape[-1]
    h = _rmsnorm(x, mg_ref[...]).astype(BF16)
    gate_a = jax.nn.sigmoid(_dot(h, wg_ref[:, :d]))
    gate_b = jax.nn.sigmoid(_dot(h, wg_ref[:, d:]))
    o = of_ref[...] + ob_ref[...]
    on = o * lax.rsqrt(_group_mean(o * o, gm_ref[...]) + NORM_EPS) * ng_ref[...]
    yb = (on * hg_ref[...]).astype(BF16)
    merged = gate_a * _dot(ya_ref[...], wa_ref[...]) + gate_b * _dot(yb, wb_ref[...])
    o_ref[...] = x + _dot(merged.astype(BF16), wo_ref[...])


def _merge(x1, y_attn, o_f, o_b, hg, mix_g, w_in, norm_g, w_a, w_b, w_out):
    t, d = x1.shape
    tm = MERGE_ROWS
    assert t % tm == 0
    row = lambda width: pl.BlockSpec((tm, width), lambda i: (i, 0))
    hbm = pl.BlockSpec(memory_space=pl.ANY)
    return pl.pallas_call(
        _merge_kernel,
        grid=(t // tm,),
        in_specs=[row(d), row(ATTN_WIDTH), row(HGRN_WIDTH), row(HGRN_WIDTH), row(HGRN_WIDTH),
                  _resident((1, d)), hbm,
                  _resident((1, HGRN_WIDTH)), _resident((V7X_MXU_DIM, V7X_MXU_DIM)), hbm, hbm, hbm],
        out_specs=row(d),
        out_shape=jax.ShapeDtypeStruct((t, d), F32),
        scratch_shapes=[pltpu.VMEM((d, 2 * d), BF16), pltpu.VMEM((ATTN_WIDTH, d), BF16),
                        pltpu.VMEM((HGRN_WIDTH, d), BF16), pltpu.VMEM((d, d), BF16),
                        pltpu.VMEM((WEIGHT_STAGE_SLOTS, WEIGHT_STAGE_ROWS, 2 * d), F32),
                        pltpu.SemaphoreType.DMA((WEIGHT_STAGE_SLOTS,))],
        compiler_params=_compiler_params(("arbitrary",)),
        name="merge",
    )(x1, y_attn, o_f, o_b, hg, mix_g, w_in, norm_g, _group_matrix(V7X_MXU_DIM, HGRN_DIM), w_a, w_b, w_out)


def kernel(x, ffn1_norm_g, ffn1_w_gate, ffn1_w_up, ffn1_w_down, mix_norm_g, w_in, q_norm_g, k_norm_g,
           hgrn_lb_logits, hgrn_out_norm_g, w_branch_attn, w_branch_hgrn, w_out, ffn2_norm_g,
           ffn2_w_gate, ffn2_w_up, ffn2_w_down, final_norm_g):
    b, s, d = x.shape
    t = b * s
    depth = w_in.shape[0]
    assert depth == 1 and hgrn_lb_logits.shape[1] == depth + 1
    final_g = final_norm_g.reshape(1, d)
    x1 = _ffn(x.reshape(t, d), ffn1_norm_g, ffn1_w_gate[0], ffn1_w_up[0], ffn1_w_down[0], final_g, False)
    q, k, v, hq, hff, hfb, hi, hg = _in_proj(x1.reshape(b, s, d), mix_norm_g, w_in, q_norm_g, k_norm_g)
    y_attn = _attention(_scores_bounded(q_norm_g, k_norm_g), q, k, v)
    o_f, o_b = _hgrn(hgrn_lb_logits, hq, hff, hfb, hi)
    flat = lambda a: a.reshape(t, a.shape[-1])
    x2 = _merge(x1, flat(y_attn), flat(o_f), flat(o_b), flat(hg), mix_norm_g, w_in, hgrn_out_norm_g,
                w_branch_attn, w_branch_hgrn, w_out)
    out = _ffn(x2, ffn2_norm_g, ffn2_w_gate[0], ffn2_w_up[0], ffn2_w_down[0], final_g, True)
    return out.reshape(b, s, d)
```

```python
import functools

import jax
import jax.numpy as jnp
import numpy as np
from jax import lax
from jax.experimental import pallas as pl
from jax.experimental.pallas import tpu as pltpu

F32 = jnp.float32
BF16 = jnp.bfloat16

NORM_EPS = 1e-6
ROPE_THETA = 10000.0
GRID_W = 64
HEAD_DIM = 64
ATTN_HEADS = 8
ATTN_KV_HEADS = 2
ATTN_GROUP = ATTN_HEADS // ATTN_KV_HEADS
HGRN_HEADS = 8
HGRN_DIM = 64
HGRN_CHUNK = 32
ATTN_WIDTH = ATTN_HEADS * HEAD_DIM
KV_WIDTH = ATTN_KV_HEADS * HEAD_DIM
HGRN_WIDTH = HGRN_HEADS * HGRN_DIM
HGRN_PAIR = 2 * HGRN_DIM
HGRN_PAIRS = HGRN_HEADS // 2

V7X_LANES = 128
V7X_MXU_DIM = 256
V7X_VMEM_LIMIT_BYTES = 56 * 1024 * 1024

FFN_ROWS = 512
FFN_COLS = V7X_MXU_DIM
PROJ_ROWS = 1024
ATTN_Q_ROWS = 512
ATTN_KEY_SPLIT = 4
ATTN_SCORE_BOUND = 60.0
HGRN_ROWS = 256
HGRN_INTRA_ROWS = V7X_MXU_DIM // 2
MERGE_ROWS = 1024
WEIGHT_STAGE_ROWS = 64
WEIGHT_STAGE_SLOTS = 4


def _compiler_params(semantics):
    return pltpu.CompilerParams(dimension_semantics=semantics, vmem_limit_bytes=V7X_VMEM_LIMIT_BYTES)


def _resident(shape):
    zeros = (0,) * len(shape)
    return pl.BlockSpec(shape, lambda *_: zeros, pipeline_mode=pl.Buffered(1))


def _rmsnorm(x, g):
    return x * lax.rsqrt(jnp.mean(x * x, axis=-1, keepdims=True) + NORM_EPS) * g


def _dot(a, b):
    return jnp.dot(a, b, preferred_element_type=F32)


def _dot_nt(a, b):
    return lax.dot_general(a, b, (((1,), (1,)), ((), ())), preferred_element_type=F32)


def _dot_tn(a, b):
    return lax.dot_general(a, b, (((0,), (0,)), ((), ())), preferred_element_type=F32)


def _split2(x):
    hi = x.astype(BF16)
    lo = (x - hi.astype(F32)).astype(BF16)
    return hi, lo


def _split3(x):
    hi = x.astype(BF16)
    r = x - hi.astype(F32)
    mid = r.astype(BF16)
    lo = (r - mid.astype(F32)).astype(BF16)
    return hi, mid, lo


def _group_mean(x, gmat):
    w = gmat.shape[0]
    hi, lo = _split2(x)
    tiles = [_dot(hi[:, i:i + w], gmat) + _dot(lo[:, i:i + w], gmat) for i in range(0, x.shape[1], w)]
    return tiles[0] if len(tiles) == 1 else jnp.concatenate(tiles, axis=1)


def _load_as_bf16(src_rows, n_rows, dst, stage, sem):
    depth, chunk, _ = stage.shape
    cols = dst.shape[1]
    n = n_rows // chunk
    assert n_rows % chunk == 0 and cols <= stage.shape[2]

    def copy(k):
        slot = k % depth
        return pltpu.make_async_copy(src_rows(k * chunk, chunk), stage.at[slot, :, pl.ds(0, cols)], sem.at[slot])

    for k in range(min(depth - 1, n)):
        copy(k).start()
    for k in range(n):
        if k + depth - 1 < n:
            copy(k + depth - 1).start()
        copy(k).wait()
        dst[pl.ds(k * chunk, chunk), :] = stage[k % depth, :, :cols].astype(BF16)


def _ffn_kernel(x_ref, g_ref, wg_hbm, wu_hbm, wd_hbm, fg_ref, o_ref,
                wg_ref, wu_ref, wd_ref, stage_g, stage_u, stage_d, sem, *, final_norm):
    d_ff = wg_ref.shape[1]
    n_chunks = d_ff // FFN_COLS

    def chunk_copies(c):
        cols = pl.ds(c * FFN_COLS, FFN_COLS)
        slot = c % 2
        return (pltpu.make_async_copy(wg_hbm.at[:, cols], stage_g.at[slot], sem.at[0, slot]),
                pltpu.make_async_copy(wu_hbm.at[:, cols], stage_u.at[slot], sem.at[1, slot]),
                pltpu.make_async_copy(wd_hbm.at[cols, :], stage_d.at[slot], sem.at[2, slot]))

    def tile(stream_weights):
        if stream_weights:
            for cp in chunk_copies(0):
                cp.start()
        x = x_ref[...]
        xn = _rmsnorm(x, g_ref[...]).astype(BF16)
        acc = None
        for c in range(n_chunks):
            sl = slice(c * FFN_COLS, (c + 1) * FFN_COLS)
            if stream_weights:
                if c + 1 < n_chunks:
                    for cp in chunk_copies(c + 1):
                        cp.start()
                for cp in chunk_copies(c):
                    cp.wait()
                wg_ref[:, sl] = stage_g[c % 2].astype(BF16)
                wu_ref[:, sl] = stage_u[c % 2].astype(BF16)
                wd_ref[sl, :] = stage_d[c % 2].astype(BF16)
            gate = _dot(xn, wg_ref[:, sl])
            up = _dot(xn, wu_ref[:, sl])
            act = (jax.nn.silu(gate) * up).astype(BF16)
            part = _dot(act, wd_ref[sl, :])
            acc = part if acc is None else acc + part
        y = x + 0.5 * acc
        if final_norm:
            y = _rmsnorm(y, fg_ref[...])
        o_ref[...] = y

    pl.when(pl.program_id(0) == 0)(functools.partial(tile, True))
    pl.when(pl.program_id(0) != 0)(functools.partial(tile, False))


def _ffn(x, norm_g, w_gate, w_up, w_down, final_g, final_norm):
    t, d = x.shape
    d_ff = w_gate.shape[1]
    assert t % FFN_ROWS == 0 and d_ff % FFN_COLS == 0
    row = pl.BlockSpec((FFN_ROWS, d), lambda i: (i, 0))
    hbm = pl.BlockSpec(memory_space=pl.ANY)
    return pl.pallas_call(
        functools.partial(_ffn_kernel, final_norm=final_norm),
        grid=(t // FFN_ROWS,),
        in_specs=[row, _resident((1, d)), hbm, hbm, hbm, _resident((1, d))],
        out_specs=row,
        out_shape=jax.ShapeDtypeStruct((t, d), F32),
        scratch_shapes=[pltpu.VMEM((d, d_ff), BF16), pltpu.VMEM((d, d_ff), BF16), pltpu.VMEM((d_ff, d), BF16),
                        pltpu.VMEM((2, d, FFN_COLS), F32), pltpu.VMEM((2, d, FFN_COLS), F32),
                        pltpu.VMEM((2, FFN_COLS, d), F32), pltpu.SemaphoreType.DMA((3, 2))],
        compiler_params=_compiler_params(("arbitrary",)),
        name="ffn_final" if final_norm else "ffn",
    )(x, norm_g, w_gate, w_up, w_down, final_g)


def _proj_kernel(x_ref, g_ref, w_hbm, qgt_ref, kgt_ref, cost_ref, sint_ref,
                 q_ref, k_ref, v_ref, hq_ref, ff_ref, fb_ref, hi_ref, hg_ref, w_ref, stage, sem, wqvt_ref):
    @pl.when((pl.program_id(0) == 0) & (pl.program_id(1) == 0))
    def _():
        n_in = w_ref.shape[1]
        _load_as_bf16(lambda r, n: w_hbm.at[0, pl.ds(r, n), pl.ds(0, n_in)], w_ref.shape[0], w_ref, stage, sem)
        v0 = ATTN_WIDTH + KV_WIDTH
        wqvt_ref[:v0, :] = w_ref[:, :v0].astype(F32).T.astype(BF16)
        wqvt_ref[v0:, :] = w_ref[:, v0:v0 + KV_WIDTH].astype(F32).T.astype(BF16)

    h = _rmsnorm(x_ref[0], g_ref[...]).astype(BF16)
    rows = h.shape[0]

    def cols(start, width):
        return _dot(h, w_ref[:, start:start + width])

    qv_t = _dot_nt(wqvt_ref[...], h)
    q3 = qv_t[:ATTN_WIDTH].reshape(ATTN_HEADS, HEAD_DIM, rows)
    qn = q3 * lax.rsqrt(jnp.mean(q3 * q3, axis=1, keepdims=True) + NORM_EPS) * qgt_ref[...]
    half = HEAD_DIM // 2
    cos_t, sin_t = cost_ref[...], sint_ref[...]

    def rope_t(x):
        a, b = x[:, :half], x[:, half:]
        return jnp.concatenate([a * cos_t - b * sin_t, b * cos_t + a * sin_t], axis=1)

    q_ref[0] = (rope_t(qn) * (HEAD_DIM ** -0.5)).astype(BF16)
    k3 = qv_t[ATTN_WIDTH:ATTN_WIDTH + KV_WIDTH].reshape(ATTN_KV_HEADS, HEAD_DIM, rows)
    kn = k3 * lax.rsqrt(jnp.mean(k3 * k3, axis=1, keepdims=True) + NORM_EPS) * kgt_ref[...]
    kr = rope_t(kn).reshape(KV_WIDTH, rows).T
    off = ATTN_WIDTH + 2 * KV_WIDTH
    ones = jnp.ones((HEAD_DIM, rows), F32)
    for hd in range(ATTN_KV_HEADS):
        k_ref[0, hd] = kr[:, hd * HEAD_DIM:(hd + 1) * HEAD_DIM].astype(BF16)
        v_hd = qv_t[ATTN_WIDTH + KV_WIDTH + hd * HEAD_DIM:ATTN_WIDTH + KV_WIDTH + (hd + 1) * HEAD_DIM]
        v_ref[0, hd] = jnp.concatenate([v_hd, ones], axis=0).astype(BF16)
    hq_ref[0] = jax.nn.silu(cols(off, HGRN_WIDTH))
    off += HGRN_WIDTH
    ff_ref[0] = cols(off, HGRN_WIDTH)
    off += HGRN_WIDTH
    fb_ref[0] = cols(off, HGRN_WIDTH)
    off += HGRN_WIDTH
    hi_ref[0] = cols(off, HGRN_WIDTH).astype(BF16)
    off += HGRN_WIDTH
    hg_ref[0] = jax.nn.silu(cols(off, HGRN_WIDTH))


def _group_matrix(width, group):
    idx = np.arange(width) // group
    return jnp.asarray((idx[:, None] == idx[None, :]).astype(np.float32) / group, dtype=BF16)


def _rope_tables(seq_len):
    f32 = np.float32
    rows = seq_len // GRID_W
    row = np.repeat(np.arange(rows, dtype=f32), GRID_W)
    col = np.tile(np.arange(GRID_W, dtype=f32), rows)
    n_freq = HEAD_DIM // 4
    inv = (f32(ROPE_THETA) ** (-np.arange(n_freq, dtype=f32) / f32(n_freq))).astype(f32)
    ang = np.concatenate([row[:, None] * inv, col[:, None] * inv], axis=-1)
    cos, sin = np.cos(ang).astype(f32), np.sin(ang).astype(f32)
    return jnp.asarray(np.ascontiguousarray(cos.T)), jnp.asarray(np.ascontiguousarray(sin.T))


def _in_proj(x1, norm_g, w_in, q_g, k_g):
    b, s, d = x1.shape
    n_in = ATTN_WIDTH + 2 * KV_WIDTH + 5 * HGRN_WIDTH
    assert s % PROJ_ROWS == 0 and w_in.shape[2] == n_in + 2 * d
    cos_t, sin_t = _rope_tables(s)
    tm = PROJ_ROWS
    tok = lambda width: pl.BlockSpec((1, tm, width), lambda bi, i: (bi, i, 0))
    heads = lambda n, width=HEAD_DIM: pl.BlockSpec((1, n, tm, width), lambda bi, i: (bi, 0, i, 0))
    pos_t = pl.BlockSpec((HEAD_DIM // 2, tm), lambda bi, i: (0, i))
    f32_tok = lambda width: jax.ShapeDtypeStruct((b, s, width), F32)
    out_shape = (
        jax.ShapeDtypeStruct((b, ATTN_HEADS, HEAD_DIM, s), BF16),
        jax.ShapeDtypeStruct((b, ATTN_KV_HEADS, s, HEAD_DIM), BF16),
        jax.ShapeDtypeStruct((b, ATTN_KV_HEADS, V7X_LANES, s), BF16),
        f32_tok(HGRN_WIDTH), f32_tok(HGRN_WIDTH), f32_tok(HGRN_WIDTH),
        jax.ShapeDtypeStruct((b, s, HGRN_WIDTH), BF16),
        f32_tok(HGRN_WIDTH),
    )
    vt_spec = pl.BlockSpec((1, ATTN_KV_HEADS, V7X_LANES, tm), lambda bi, i: (bi, 0, 0, i))
    qt_spec = pl.BlockSpec((1, ATTN_HEADS, HEAD_DIM, tm), lambda bi, i: (bi, 0, 0, i))
    out_specs = (qt_spec, heads(ATTN_KV_HEADS), vt_spec,
                 tok(HGRN_WIDTH), tok(HGRN_WIDTH), tok(HGRN_WIDTH), tok(HGRN_WIDTH), tok(HGRN_WIDTH))
    return pl.pallas_call(
        _proj_kernel,
        grid=(b, s // tm),
        in_specs=[tok(d), _resident((1, d)), pl.BlockSpec(memory_space=pl.ANY),
                  _resident((HEAD_DIM, tm)), _resident((HEAD_DIM, tm)), pos_t, pos_t],
        out_specs=out_specs,
        out_shape=out_shape,
        scratch_shapes=[pltpu.VMEM((d, n_in), BF16), pltpu.VMEM((WEIGHT_STAGE_SLOTS, WEIGHT_STAGE_ROWS, n_in), F32),
                        pltpu.SemaphoreType.DMA((WEIGHT_STAGE_SLOTS,)),
                        pltpu.VMEM((ATTN_WIDTH + 2 * KV_WIDTH, d), BF16)],
        compiler_params=_compiler_params(("arbitrary", "arbitrary")),
        name="in_proj",
    )(x1, norm_g, w_in, jnp.broadcast_to(q_g.reshape(HEAD_DIM, 1), (HEAD_DIM, tm)),
      jnp.broadcast_to(k_g.reshape(HEAD_DIM, 1), (HEAD_DIM, tm)), cos_t, sin_t)


def _attn_kernel(bounded_ref, q_ref, k_ref, vt_ref, o_ref):
    tq = q_ref.shape[3]
    s_len = k_ref.shape[2]
    q_t = jnp.concatenate([q_ref[0, r] for r in range(ATTN_GROUP)], axis=1)

    def finish(accs):
        norm = [a[:HEAD_DIM] * (1.0 / a[HEAD_DIM:HEAD_DIM + 1]) for a in accs]
        pairs = [jnp.concatenate(norm[r:r + 2], axis=0).T for r in range(0, ATTN_GROUP, 2)]
        o_ref[0] = jnp.concatenate(pairs, axis=1).astype(BF16)

    def attend(shifted):
        kc = s_len // ATTN_KEY_SPLIT
        chunks = [slice(c * kc, (c + 1) * kc) for c in range(ATTN_KEY_SPLIT)]
        if shifted:
            s_t = _dot(k_ref[0, 0], q_t)
            p_full = jnp.exp(s_t - jnp.max(s_t, axis=0, keepdims=True)).astype(BF16)
            p_chunks = [p_full[ks] for ks in chunks]
        else:
            p_chunks = [jnp.exp(_dot(k_ref[0, 0, ks, :], q_t)).astype(BF16) for ks in chunks]
        accs = [None] * ATTN_GROUP
        for ks, p_t in zip(chunks, p_chunks):
            vt_c = vt_ref[0, 0, :, ks]
            for r in range(ATTN_GROUP):
                part = _dot(vt_c, p_t[:, r * tq:(r + 1) * tq])
                accs[r] = part if accs[r] is None else accs[r] + part
        finish(accs)

    pl.when(bounded_ref[0] == 1)(functools.partial(attend, False))
    pl.when(bounded_ref[0] != 1)(functools.partial(attend, True))


def _attention(bounded, q, k, v):
    b, _, _, s = q.shape
    tq = ATTN_Q_ROWS
    assert s % tq == 0 and s % (ATTN_KEY_SPLIT * V7X_LANES) == 0
    k_spec = pl.BlockSpec((1, 1, s, HEAD_DIM), lambda bi, g, i, *_: (bi, g, 0, 0))
    vt_spec = pl.BlockSpec((1, 1, V7X_LANES, s), lambda bi, g, i, *_: (bi, g, 0, 0))
    grid_spec = pltpu.PrefetchScalarGridSpec(
        num_scalar_prefetch=1,
        grid=(b, ATTN_KV_HEADS, s // tq),
        in_specs=[pl.BlockSpec((1, ATTN_GROUP, HEAD_DIM, tq), lambda bi, g, i, *_: (bi, g, 0, i)),
                  k_spec, vt_spec],
        out_specs=pl.BlockSpec((1, tq, ATTN_GROUP * HEAD_DIM), lambda bi, g, i, *_: (bi, i, g)),
    )
    return pl.pallas_call(
        _attn_kernel,
        grid_spec=grid_spec,
        out_shape=jax.ShapeDtypeStruct((b, s, ATTN_WIDTH), BF16),
        compiler_params=_compiler_params(("parallel", "parallel", "parallel")),
        name="attention",
    )(bounded, q, k, v)


def _scores_bounded(q_g, k_g):
    bound = (HEAD_DIM ** 0.5) * jnp.max(jnp.abs(q_g)) * jnp.max(jnp.abs(k_g))
    return (bound <= ATTN_SCORE_BOUND).astype(jnp.int32).reshape(1)


def _hgrn_prepare(qt, f_logit, lb, reverse):
    rows = qt.shape[0]
    c = HGRN_CHUNK
    n_chunks = rows // c
    f = lb + (1.0 - lb) * jax.nn.sigmoid(f_logit)
    kk = 1.0 - f
    log_f = jnp.log(f)
    r_i = lax.broadcasted_iota(jnp.int32, (rows, rows), 0)
    c_i = lax.broadcasted_iota(jnp.int32, (rows, rows), 1)
    same_chunk = (r_i // c) == (c_i // c)
    causal = same_chunk & ((c_i >= r_i) if reverse else (c_i <= r_i))
    tri = jnp.where(causal, 1.0, 0.0).astype(BF16)
    hi, mid, lo = _split3(log_f)
    bcum = _dot(tri, hi) + _dot(tri, mid) + _dot(tri, lo)
    b3 = bcum.reshape(n_chunks, c, bcum.shape[-1])
    mid_row = c // 2 if reverse else c // 2 - 1
    last_row = 0 if reverse else c - 1
    b_mid = b3[:, mid_row:mid_row + 1, :]
    b_last = b3[:, last_row:last_row + 1, :]
    q3 = qt.reshape(b3.shape)
    k3 = kk.reshape(b3.shape)
    d_mid = b3 - b_mid
    q_mid = q3 * jnp.exp(d_mid)
    k_mid = k3 * jnp.exp(-d_mid)
    k_end = k_mid * jnp.exp(b_last - b_mid)
    q_in = q_mid * jnp.exp(b_mid)
    flat = lambda a: a.reshape(bcum.shape).astype(BF16)
    sub = HGRN_INTRA_ROWS
    causal_sub = jnp.concatenate([causal[:sub, :sub]] * 2, axis=1)
    return dict(causal_sub=causal_sub, q_mid=flat(q_mid), k_mid=flat(k_mid), k_end=flat(k_end), q_in=flat(q_in),
                decay=jnp.exp(b_last))


def _pair_slice(p):
    return slice(p * HGRN_PAIR, (p + 1) * HGRN_PAIR)


def _hgrn_intra(ops, v, p):
    rows = v.shape[0]
    sub = HGRN_INTRA_ROWS
    lower_head = lax.broadcasted_iota(jnp.int32, (sub, HGRN_PAIR), 1) < HGRN_DIM
    zero_bf = jnp.zeros((), BF16)
    stack = lambda x: jnp.concatenate([jnp.where(lower_head, x, zero_bf), jnp.where(lower_head, zero_bf, x)], axis=0)
    causal = ops["causal_sub"]
    sl = _pair_slice(p)
    outs = []
    for h in range(rows // sub):
        rs = slice(h * sub, (h + 1) * sub)
        scores = _dot_nt(ops["q_mid"][rs, sl], stack(ops["k_mid"][rs, sl]))
        scores = jnp.where(causal, scores, 0.0).astype(BF16)
        outs.append(_dot(scores, stack(v[rs, sl])))
    return jnp.concatenate(outs, axis=0)


def _hgrn_contribs(ops, v):
    c = HGRN_CHUNK
    n_chunks = v.shape[0] // c
    return [[_dot_tn(v[n * c:(n + 1) * c, _pair_slice(p)], ops["k_end"][n * c:(n + 1) * c, _pair_slice(p)])
             for n in range(n_chunks)] for p in range(HGRN_PAIRS)]


def _hgrn_scan(ops, contribs, state_ref, reverse):
    n_chunks = len(contribs[0])
    sr = lax.broadcasted_iota(jnp.int32, (HGRN_PAIR, HGRN_PAIR), 0)
    sc = lax.broadcasted_iota(jnp.int32, (HGRN_PAIR, HGRN_PAIR), 1)
    same_head = (sr // HGRN_DIM) == (sc // HGRN_DIM)
    order = range(n_chunks - 1, -1, -1) if reverse else range(n_chunks)
    states = [[None] * n_chunks for _ in range(HGRN_PAIRS)]
    for p in range(HGRN_PAIRS):
        st = state_ref[p]
        for n in order:
            states[p][n] = st.astype(BF16)
            st = ops["decay"][n, :, _pair_slice(p)] * st + jnp.where(same_head, contribs[p][n], 0.0)
        state_ref[p] = st
    return states


def _hgrn_inter(ops, states, p):
    c = HGRN_CHUNK
    parts = [_dot_nt(ops["q_in"][n * c:(n + 1) * c, _pair_slice(p)], st) for n, st in enumerate(states[p])]
    return jnp.concatenate(parts, axis=0)


def _hgrn_kernel(lbl_ref, qf_ref, ff_ref, vf_ref, qb_ref, fb_ref, vb_ref, of_ref, ob_ref,
                 st_f_ref, st_b_ref):
    @pl.when(pl.program_id(1) == 0)
    def _():
        st_f_ref[...] = jnp.zeros_like(st_f_ref)
        st_b_ref[...] = jnp.zeros_like(st_b_ref)

    logits = lbl_ref[...]
    e = jnp.exp(logits - jnp.max(logits, axis=1, keepdims=True))
    lb = e[:, 0, :] / jnp.sum(e, axis=1)
    v_f, v_b = vf_ref[0], vb_ref[0]
    ops_f = _hgrn_prepare(qf_ref[0], ff_ref[0], lb[0:1], False)
    ops_b = _hgrn_prepare(qb_ref[0], fb_ref[0], lb[1:2], True)
    con_f = _hgrn_contribs(ops_f, v_f)
    con_b = _hgrn_contribs(ops_b, v_b)
    states_f = _hgrn_scan(ops_f, con_f, st_f_ref, False)
    states_b = _hgrn_scan(ops_b, con_b, st_b_ref, True)
    pairs = range(HGRN_PAIRS)
    inter_f = [_hgrn_inter(ops_f, states_f, p) for p in pairs]
    inter_b = [_hgrn_inter(ops_b, states_b, p) for p in pairs]
    intra_f = [_hgrn_intra(ops_f, v_f, p) for p in pairs]
    intra_b = [_hgrn_intra(ops_b, v_b, p) for p in pairs]
    of_ref[0] = jnp.concatenate([a + b for a, b in zip(intra_f, inter_f)], axis=1)
    ob_ref[0] = jnp.concatenate([a + b for a, b in zip(intra_b, inter_b)], axis=1)


def _hgrn(lb_logits, hq, hff, hfb, hi):
    b, s, w = hq.shape
    r = HGRN_ROWS
    assert s % r == 0 and r % HGRN_INTRA_ROWS == 0 and HGRN_INTRA_ROWS % HGRN_CHUNK == 0
    nb = s // r
    fwd = pl.BlockSpec((1, r, w), lambda bi, j: (bi, j, 0))
    bwd = pl.BlockSpec((1, r, w), lambda bi, j: (bi, nb - 1 - j, 0))
    state = pltpu.VMEM((HGRN_PAIRS, HGRN_PAIR, HGRN_PAIR), F32)
    return pl.pallas_call(
        _hgrn_kernel,
        grid=(b, nb),
        in_specs=[_resident(lb_logits.shape), fwd, fwd, fwd, bwd, bwd, bwd],
        out_specs=(fwd, bwd),
        out_shape=(jax.ShapeDtypeStruct((b, s, w), F32), jax.ShapeDtypeStruct((b, s, w), F32)),
        scratch_shapes=[state, state],
        compiler_params=_compiler_params(("parallel", "arbitrary")),
        name="hgrn",
    )(lb_logits, hq, hff, hi, hq, hfb, hi)


def _merge_kernel(x_ref, ya_ref, of_ref, ob_ref, hg_ref, mg_ref, win_hbm, ng_ref, gm_ref,
                  wa_hbm, wb_hbm, wo_hbm, o_ref, wg_ref, wa_ref, wb_ref, wo_ref, stage, sem):
    @pl.when(pl.program_id(0) == 0)
    def _():
        n_gate = wg_ref.shape[1]
        n_mix = win_hbm.shape[2] - n_gate
        _load_as_bf16(lambda r, n: win_hbm.at[0, pl.ds(r, n), pl.ds(n_mix, n_gate)], wg_ref.shape[0], wg_ref, stage, sem)
        for hbm, ref in ((wa_hbm, wa_ref), (wb_hbm, wb_ref), (wo_hbm, wo_ref)):
            _load_as_bf16(lambda r, n, hbm=hbm: hbm.at[0, pl.ds(r, n), :], ref.shape[0], ref, stage, sem)

    x = x_ref[...]
    d = x.shape[-1]
    h = _rmsnorm(x, mg_ref[...]).astype(BF16)
    gate_a = jax.nn.sigmoid(_dot(h, wg_ref[:, :d]))
    gate_b = jax.nn.sigmoid(_dot(h, wg_ref[:, d:]))
    o = of_ref[...] + ob_ref[...]
    on = o * lax.rsqrt(_group_mean(o * o, gm_ref[...]) + NORM_EPS) * ng_ref[...]
    yb = (on * hg_ref[...]).astype(BF16)
    merged = gate_a * _dot(ya_ref[...], wa_ref[...]) + gate_b * _dot(yb, wb_ref[...])
    o_ref[...] = x + _dot(merged.astype(BF16), wo_ref[...])


def _merge(x1, y_attn, o_f, o_b, hg, mix_g, w_in, norm_g, w_a, w_b, w_out):
    t, d = x1.shape
    tm = MERGE_ROWS
    assert t % tm == 0
    row = lambda width: pl.BlockSpec((tm, width), lambda i: (i, 0))
    hbm = pl.BlockSpec(memory_space=pl.ANY)
    return pl.pallas_call(
        _merge_kernel,
        grid=(t // tm,),
        in_specs=[row(d), row(ATTN_WIDTH), row(HGRN_WIDTH), row(HGRN_WIDTH), row(HGRN_WIDTH),
                  _resident((1, d)), hbm,
                  _resident((1, HGRN_WIDTH)), _resident((V7X_MXU_DIM, V7X_MXU_DIM)), hbm, hbm, hbm],
        out_specs=row(d),
        out_shape=jax.ShapeDtypeStruct((t, d), F32),
        scratch_shapes=[pltpu.VMEM((d, 2 * d), BF16), pltpu.VMEM((ATTN_WIDTH, d), BF16),
                        pltpu.VMEM((HGRN_WIDTH, d), BF16), pltpu.VMEM((d, d), BF16),
                        pltpu.VMEM((WEIGHT_STAGE_SLOTS, WEIGHT_STAGE_ROWS, 2 * d), F32),
                        pltpu.SemaphoreType.DMA((WEIGHT_STAGE_SLOTS,))],
        compiler_params=_compiler_params(("arbitrary",)),
        name="merge",
    )(x1, y_attn, o_f, o_b, hg, mix_g, w_in, norm_g, _group_matrix(V7X_MXU_DIM, HGRN_DIM), w_a, w_b, w_out)


def kernel(x, ffn1_norm_g, ffn1_w_gate, ffn1_w_up, ffn1_w_down, mix_norm_g, w_in, q_norm_g, k_norm_g,
           hgrn_lb_logits, hgrn_out_norm_g, w_branch_attn, w_branch_hgrn, w_out, ffn2_norm_g,
           ffn2_w_gate, ffn2_w_up, ffn2_w_down, final_norm_g):
    b, s, d = x.shape
    t = b * s
    depth = w_in.shape[0]
    assert depth == 1 and hgrn_lb_logits.shape[1] == depth + 1
    final_g = final_norm_g.reshape(1, d)
    x1 = _ffn(x.reshape(t, d), ffn1_norm_g, ffn1_w_gate[0], ffn1_w_up[0], ffn1_w_down[0], final_g, False)
    q, k, v, hq, hff, hfb, hi, hg = _in_proj(x1.reshape(b, s, d), mix_norm_g, w_in, q_norm_g, k_norm_g)
    y_attn = _attention(_scores_bounded(q_norm_g, k_norm_g), q, k, v)
    o_f, o_b = _hgrn(hgrn_lb_logits, hq, hff, hfb, hi)
    flat = lambda a: a.reshape(t, a.shape[-1])
    x2 = _merge(x1, flat(y_attn), flat(o_f), flat(o_b), flat(hg), mix_norm_g, w_in, hgrn_out_norm_g,
                w_branch_attn, w_branch_hgrn, w_out)
    out = _ffn(x2, ffn2_norm_g, ffn2_w_gate[0], ffn2_w_up[0], ffn2_w_down[0], final_g, True)
    return out.reshape(b, s, d)
```

```python
import functools

import jax
import jax.numpy as jnp
import numpy as np
from jax import lax
from jax.experimental import pallas as pl
from jax.experimental.pallas import tpu as pltpu

F32 = jnp.float32
BF16 = jnp.bfloat16

NORM_EPS = 1e-6
ROPE_THETA = 10000.0
GRID_W = 64
HEAD_DIM = 64
ATTN_HEADS = 8
ATTN_KV_HEADS = 2
ATTN_GROUP = ATTN_HEADS // ATTN_KV_HEADS
HGRN_HEADS = 8
HGRN_DIM = 64
HGRN_CHUNK = 32
ATTN_WIDTH = ATTN_HEADS * HEAD_DIM
KV_WIDTH = ATTN_KV_HEADS * HEAD_DIM
HGRN_WIDTH = HGRN_HEADS * HGRN_DIM
HGRN_PAIR = 2 * HGRN_DIM
HGRN_PAIRS = HGRN_HEADS // 2

V7X_LANES = 128
V7X_MXU_DIM = 256
V7X_VMEM_LIMIT_BYTES = 56 * 1024 * 1024

FFN_ROWS = 512
FFN_COLS = V7X_MXU_DIM
PROJ_ROWS = 1024
ATTN_Q_ROWS = 512
ATTN_KEY_SPLIT = 4
ATTN_SCORE_BOUND = 60.0
HGRN_ROWS = 256
HGRN_INTRA_ROWS = V7X_MXU_DIM // 2
MERGE_ROWS = 1024
WEIGHT_STAGE_ROWS = 128
WEIGHT_STAGE_SLOTS = 4


def _compiler_params(semantics):
    return pltpu.CompilerParams(dimension_semantics=semantics, vmem_limit_bytes=V7X_VMEM_LIMIT_BYTES)


def _resident(shape):
    zeros = (0,) * len(shape)
    return pl.BlockSpec(shape, lambda *_: zeros, pipeline_mode=pl.Buffered(1))


def _rmsnorm(x, g):
    return x * lax.rsqrt(jnp.mean(x * x, axis=-1, keepdims=True) + NORM_EPS) * g


def _dot(a, b):
    return jnp.dot(a, b, preferred_element_type=F32)


def _dot_nt(a, b):
    return lax.dot_general(a, b, (((1,), (1,)), ((), ())), preferred_element_type=F32)


def _dot_tn(a, b):
    return lax.dot_general(a, b, (((0,), (0,)), ((), ())), preferred_element_type=F32)


def _split2(x):
    hi = x.astype(BF16)
    lo = (x - hi.astype(F32)).astype(BF16)
    return hi, lo


def _split3(x):
    hi = x.astype(BF16)
    r = x - hi.astype(F32)
    mid = r.astype(BF16)
    lo = (r - mid.astype(F32)).astype(BF16)
    return hi, mid, lo


def _group_mean(x, gmat):
    w = gmat.shape[0]
    hi, lo = _split2(x)
    tiles = [_dot(hi[:, i:i + w], gmat) + _dot(lo[:, i:i + w], gmat) for i in range(0, x.shape[1], w)]
    return tiles[0] if len(tiles) == 1 else jnp.concatenate(tiles, axis=1)


def _load_as_bf16(src_rows, n_rows, dst, stage, sem):
    depth, chunk, _ = stage.shape
    cols = dst.shape[1]
    n = n_rows // chunk
    assert n_rows % chunk == 0 and cols <= stage.shape[2]

    def copy(k):
        slot = k % depth
        return pltpu.make_async_copy(src_rows(k * chunk, chunk), stage.at[slot, :, pl.ds(0, cols)], sem.at[slot])

    for k in range(min(depth - 1, n)):
        copy(k).start()
    for k in range(n):
        if k + depth - 1 < n:
            copy(k + depth - 1).start()
        copy(k).wait()
        dst[pl.ds(k * chunk, chunk), :] = stage[k % depth, :, :cols].astype(BF16)


def _ffn_kernel(x_ref, g_ref, wg_hbm, wu_hbm, wd_hbm, fg_ref, o_ref,
                wg_ref, wu_ref, wd_ref, stage_g, stage_u, stage_d, sem, *, final_norm):
    d_ff = wg_ref.shape[1]
    n_chunks = d_ff // FFN_COLS

    def chunk_copies(c):
        cols = pl.ds(c * FFN_COLS, FFN_COLS)
        slot = c % 2
        return (pltpu.make_async_copy(wg_hbm.at[:, cols], stage_g.at[slot], sem.at[0, slot]),
                pltpu.make_async_copy(wu_hbm.at[:, cols], stage_u.at[slot], sem.at[1, slot]),
                pltpu.make_async_copy(wd_hbm.at[cols, :], stage_d.at[slot], sem.at[2, slot]))

    def tile(stream_weights):
        if stream_weights:
            for cp in chunk_copies(0):
                cp.start()
        x = x_ref[...]
        xn = _rmsnorm(x, g_ref[...]).astype(BF16)
        acc = None
        for c in range(n_chunks):
            sl = slice(c * FFN_COLS, (c + 1) * FFN_COLS)
            if stream_weights:
                if c + 1 < n_chunks:
                    for cp in chunk_copies(c + 1):
                        cp.start()
                for cp in chunk_copies(c):
                    cp.wait()
                wg_ref[:, sl] = stage_g[c % 2].astype(BF16)
                wu_ref[:, sl] = stage_u[c % 2].astype(BF16)
                wd_ref[sl, :] = stage_d[c % 2].astype(BF16)
            gate = _dot(xn, wg_ref[:, sl])
            up = _dot(xn, wu_ref[:, sl])
            act = (jax.nn.silu(gate) * up).astype(BF16)
            part = _dot(act, wd_ref[sl, :])
            acc = part if acc is None else acc + part
        y = x + 0.5 * acc
        if final_norm:
            y = _rmsnorm(y, fg_ref[...])
        o_ref[...] = y

    pl.when(pl.program_id(0) == 0)(functools.partial(tile, True))
    pl.when(pl.program_id(0) != 0)(functools.partial(tile, False))


def _ffn(x, norm_g, w_gate, w_up, w_down, final_g, final_norm):
    t, d = x.shape
    d_ff = w_gate.shape[1]
    assert t % FFN_ROWS == 0 and d_ff % FFN_COLS == 0
    row = pl.BlockSpec((FFN_ROWS, d), lambda i: (i, 0))
    hbm = pl.BlockSpec(memory_space=pl.ANY)
    return pl.pallas_call(
        functools.partial(_ffn_kernel, final_norm=final_norm),
        grid=(t // FFN_ROWS,),
        in_specs=[row, _resident((1, d)), hbm, hbm, hbm, _resident((1, d))],
        out_specs=row,
        out_shape=jax.ShapeDtypeStruct((t, d), F32),
        scratch_shapes=[pltpu.VMEM((d, d_ff), BF16), pltpu.VMEM((d, d_ff), BF16), pltpu.VMEM((d_ff, d), BF16),
                        pltpu.VMEM((2, d, FFN_COLS), F32), pltpu.VMEM((2, d, FFN_COLS), F32),
                        pltpu.VMEM((2, FFN_COLS, d), F32), pltpu.SemaphoreType.DMA((3, 2))],
        compiler_params=_compiler_params(("arbitrary",)),
        name="ffn_final" if final_norm else "ffn",
    )(x, norm_g, w_gate, w_up, w_down, final_g)


def _proj_kernel(x_ref, g_ref, w_hbm, qgt_ref, kgt_ref, cost_ref, sint_ref,
                 q_ref, k_ref, v_ref, hq_ref, ff_ref, fb_ref, hi_ref, hg_ref, w_ref, stage, sem, wqvt_ref):
    @pl.when((pl.program_id(0) == 0) & (pl.program_id(1) == 0))
    def _():
        n_in = w_ref.shape[1]
        _load_as_bf16(lambda r, n: w_hbm.at[0, pl.ds(r, n), pl.ds(0, n_in)], w_ref.shape[0], w_ref, stage, sem)
        v0 = ATTN_WIDTH + KV_WIDTH
        wqvt_ref[:v0, :] = w_ref[:, :v0].astype(F32).T.astype(BF16)
        wqvt_ref[v0:, :] = w_ref[:, v0:v0 + KV_WIDTH].astype(F32).T.astype(BF16)

    h = _rmsnorm(x_ref[0], g_ref[...]).astype(BF16)
    rows = h.shape[0]

    def cols(start, width):
        return _dot(h, w_ref[:, start:start + width])

    qv_t = _dot_nt(wqvt_ref[...], h)
    q3 = qv_t[:ATTN_WIDTH].reshape(ATTN_HEADS, HEAD_DIM, rows)
    qn = q3 * lax.rsqrt(jnp.mean(q3 * q3, axis=1, keepdims=True) + NORM_EPS) * qgt_ref[...]
    half = HEAD_DIM // 2
    cos_t, sin_t = cost_ref[...], sint_ref[...]

    def rope_t(x):
        a, b = x[:, :half], x[:, half:]
        return jnp.concatenate([a * cos_t - b * sin_t, b * cos_t + a * sin_t], axis=1)

    q_ref[0] = (rope_t(qn) * (HEAD_DIM ** -0.5)).astype(BF16)
    k3 = qv_t[ATTN_WIDTH:ATTN_WIDTH + KV_WIDTH].reshape(ATTN_KV_HEADS, HEAD_DIM, rows)
    kn = k3 * lax.rsqrt(jnp.mean(k3 * k3, axis=1, keepdims=True) + NORM_EPS) * kgt_ref[...]
    kr = rope_t(kn).reshape(KV_WIDTH, rows).T
    off = ATTN_WIDTH + 2 * KV_WIDTH
    ones = jnp.ones((HEAD_DIM, rows), F32)
    for hd in range(ATTN_KV_HEADS):
        k_ref[0, hd] = kr[:, hd * HEAD_DIM:(hd + 1) * HEAD_DIM].astype(BF16)
        v_hd = qv_t[ATTN_WIDTH + KV_WIDTH + hd * HEAD_DIM:ATTN_WIDTH + KV_WIDTH + (hd + 1) * HEAD_DIM]
        v_ref[0, hd] = jnp.concatenate([v_hd, ones], axis=0).astype(BF16)
    hq_ref[0] = jax.nn.silu(cols(off, HGRN_WIDTH))
    off += HGRN_WIDTH
    ff_ref[0] = cols(off, HGRN_WIDTH)
    off += HGRN_WIDTH
    fb_ref[0] = cols(off, HGRN_WIDTH)
    off += HGRN_WIDTH
    hi_ref[0] = cols(off, HGRN_WIDTH).astype(BF16)
    off += HGRN_WIDTH
    hg_ref[0] = jax.nn.silu(cols(off, HGRN_WIDTH))


def _group_matrix(width, group):
    idx = np.arange(width) // group
    return jnp.asarray((idx[:, None] == idx[None, :]).astype(np.float32) / group, dtype=BF16)


def _rope_tables(seq_len):
    f32 = np.float32
    rows = seq_len // GRID_W
    row = np.repeat(np.arange(rows, dtype=f32), GRID_W)
    col = np.tile(np.arange(GRID_W, dtype=f32), rows)
    n_freq = HEAD_DIM // 4
    inv = (f32(ROPE_THETA) ** (-np.arange(n_freq, dtype=f32) / f32(n_freq))).astype(f32)
    ang = np.concatenate([row[:, None] * inv, col[:, None] * inv], axis=-1)
    cos, sin = np.cos(ang).astype(f32), np.sin(ang).astype(f32)
    return jnp.asarray(np.ascontiguousarray(cos.T)), jnp.asarray(np.ascontiguousarray(sin.T))


def _in_proj(x1, norm_g, w_in, q_g, k_g):
    b, s, d = x1.shape
    n_in = ATTN_WIDTH + 2 * KV_WIDTH + 5 * HGRN_WIDTH
    assert s % PROJ_ROWS == 0 and w_in.shape[2] == n_in + 2 * d
    cos_t, sin_t = _rope_tables(s)
    tm = PROJ_ROWS
    tok = lambda width: pl.BlockSpec((1, tm, width), lambda bi, i: (bi, i, 0))
    heads = lambda n, width=HEAD_DIM: pl.BlockSpec((1, n, tm, width), lambda bi, i: (bi, 0, i, 0))
    pos_t = pl.BlockSpec((HEAD_DIM // 2, tm), lambda bi, i: (0, i))
    f32_tok = lambda width: jax.ShapeDtypeStruct((b, s, width), F32)
    out_shape = (
        jax.ShapeDtypeStruct((b, ATTN_HEADS, HEAD_DIM, s), BF16),
        jax.ShapeDtypeStruct((b, ATTN_KV_HEADS, s, HEAD_DIM), BF16),
        jax.ShapeDtypeStruct((b, ATTN_KV_HEADS, V7X_LANES, s), BF16),
        f32_tok(HGRN_WIDTH), f32_tok(HGRN_WIDTH), f32_tok(HGRN_WIDTH),
        jax.ShapeDtypeStruct((b, s, HGRN_WIDTH), BF16),
        f32_tok(HGRN_WIDTH),
    )
    vt_spec = pl.BlockSpec((1, ATTN_KV_HEADS, V7X_LANES, tm), lambda bi, i: (bi, 0, 0, i))
    qt_spec = pl.BlockSpec((1, ATTN_HEADS, HEAD_DIM, tm), lambda bi, i: (bi, 0, 0, i))
    out_specs = (qt_spec, heads(ATTN_KV_HEADS), vt_spec,
                 tok(HGRN_WIDTH), tok(HGRN_WIDTH), tok(HGRN_WIDTH), tok(HGRN_WIDTH), tok(HGRN_WIDTH))
    return pl.pallas_call(
        _proj_kernel,
        grid=(b, s // tm),
        in_specs=[tok(d), _resident((1, d)), pl.BlockSpec(memory_space=pl.ANY),
                  _resident((HEAD_DIM, tm)), _resident((HEAD_DIM, tm)), pos_t, pos_t],
        out_specs=out_specs,
        out_shape=out_shape,
        scratch_shapes=[pltpu.VMEM((d, n_in), BF16), pltpu.VMEM((WEIGHT_STAGE_SLOTS, WEIGHT_STAGE_ROWS, n_in), F32),
                        pltpu.SemaphoreType.DMA((WEIGHT_STAGE_SLOTS,)),
                        pltpu.VMEM((ATTN_WIDTH + 2 * KV_WIDTH, d), BF16)],
        compiler_params=_compiler_params(("arbitrary", "arbitrary")),
        name="in_proj",
    )(x1, norm_g, w_in, jnp.broadcast_to(q_g.reshape(HEAD_DIM, 1), (HEAD_DIM, tm)),
      jnp.broadcast_to(k_g.reshape(HEAD_DIM, 1), (HEAD_DIM, tm)), cos_t, sin_t)


def _attn_kernel(bounded_ref, q_ref, k_ref, vt_ref, o_ref):
    tq = q_ref.shape[3]
    s_len = k_ref.shape[2]
    q_t = jnp.concatenate([q_ref[0, r] for r in range(ATTN_GROUP)], axis=1)

    def finish(accs):
        norm = [a[:HEAD_DIM] * (1.0 / a[HEAD_DIM:HEAD_DIM + 1]) for a in accs]
        pairs = [jnp.concatenate(norm[r:r + 2], axis=0).T for r in range(0, ATTN_GROUP, 2)]
        o_ref[0] = jnp.concatenate(pairs, axis=1).astype(BF16)

    def attend(shifted):
        kc = s_len // ATTN_KEY_SPLIT
        chunks = [slice(c * kc, (c + 1) * kc) for c in range(ATTN_KEY_SPLIT)]
        if shifted:
            s_t = _dot(k_ref[0, 0], q_t)
            p_full = jnp.exp(s_t - jnp.max(s_t, axis=0, keepdims=True)).astype(BF16)
            p_chunks = [p_full[ks] for ks in chunks]
        else:
            p_chunks = [jnp.exp(_dot(k_ref[0, 0, ks, :], q_t)).astype(BF16) for ks in chunks]
        accs = [None] * ATTN_GROUP
        for ks, p_t in zip(chunks, p_chunks):
            vt_c = vt_ref[0, 0, :, ks]
            for r in range(ATTN_GROUP):
                part = _dot(vt_c, p_t[:, r * tq:(r + 1) * tq])
                accs[r] = part if accs[r] is None else accs[r] + part
        finish(accs)

    pl.when(bounded_ref[0] == 1)(functools.partial(attend, False))
    pl.when(bounded_ref[0] != 1)(functools.partial(attend, True))


def _attention(bounded, q, k, v):
    b, _, _, s = q.shape
    tq = ATTN_Q_ROWS
    assert s % tq == 0 and s % (ATTN_KEY_SPLIT * V7X_LANES) == 0
    k_spec = pl.BlockSpec((1, 1, s, HEAD_DIM), lambda bi, g, i, *_: (bi, g, 0, 0))
    vt_spec = pl.BlockSpec((1, 1, V7X_LANES, s), lambda bi, g, i, *_: (bi, g, 0, 0))
    grid_spec = pltpu.PrefetchScalarGridSpec(
        num_scalar_prefetch=1,
        grid=(b, ATTN_KV_HEADS, s // tq),
        in_specs=[pl.BlockSpec((1, ATTN_GROUP, HEAD_DIM, tq), lambda bi, g, i, *_: (bi, g, 0, i)),
                  k_spec, vt_spec],
        out_specs=pl.BlockSpec((1, tq, ATTN_GROUP * HEAD_DIM), lambda bi, g, i, *_: (bi, i, g)),
    )
    return pl.pallas_call(
        _attn_kernel,
        grid_spec=grid_spec,
        out_shape=jax.ShapeDtypeStruct((b, s, ATTN_WIDTH), BF16),
        compiler_params=_compiler_params(("parallel", "parallel", "parallel")),
        name="attention",
    )(bounded, q, k, v)


def _scores_bounded(q_g, k_g):
    bound = (HEAD_DIM ** 0.5) * jnp.max(jnp.abs(q_g)) * jnp.max(jnp.abs(k_g))
    return (bound <= ATTN_SCORE_BOUND).astype(jnp.int32).reshape(1)


def _hgrn_prepare(qt, f_logit, lb, reverse):
    rows = qt.shape[0]
    c = HGRN_CHUNK
    n_chunks = rows // c
    f = lb + (1.0 - lb) * jax.nn.sigmoid(f_logit)
    kk = 1.0 - f
    log_f = jnp.log(f)
    r_i = lax.broadcasted_iota(jnp.int32, (rows, rows), 0)
    c_i = lax.broadcasted_iota(jnp.int32, (rows, rows), 1)
    same_chunk = (r_i // c) == (c_i // c)
    causal = same_chunk & ((c_i >= r_i) if reverse else (c_i <= r_i))
    tri = jnp.where(causal, 1.0, 0.0).astype(BF16)
    hi, mid, lo = _split3(log_f)
    bcum = _dot(tri, hi) + _dot(tri, mid) + _dot(tri, lo)
    b3 = bcum.reshape(n_chunks, c, bcum.shape[-1])
    mid_row = c // 2 if reverse else c // 2 - 1
    last_row = 0 if reverse else c - 1
    b_mid = b3[:, mid_row:mid_row + 1, :]
    b_last = b3[:, last_row:last_row + 1, :]
    q3 = qt.reshape(b3.shape)
    k3 = kk.reshape(b3.shape)
    d_mid = b3 - b_mid
    q_mid = q3 * jnp.exp(d_mid)
    k_mid = k3 * jnp.exp(-d_mid)
    k_end = k_mid * jnp.exp(b_last - b_mid)
    q_in = q_mid * jnp.exp(b_mid)
    flat = lambda a: a.reshape(bcum.shape).astype(BF16)
    sub = HGRN_INTRA_ROWS
    causal_sub = jnp.concatenate([causal[:sub, :sub]] * 2, axis=1)
    return dict(causal_sub=causal_sub, q_mid=flat(q_mid), k_mid=flat(k_mid), k_end=flat(k_end), q_in=flat(q_in),
                decay=jnp.exp(b_last))


def _pair_slice(p):
    return slice(p * HGRN_PAIR, (p + 1) * HGRN_PAIR)


def _hgrn_intra(ops, v, p):
    rows = v.shape[0]
    sub = HGRN_INTRA_ROWS
    lower_head = lax.broadcasted_iota(jnp.int32, (sub, HGRN_PAIR), 1) < HGRN_DIM
    zero_bf = jnp.zeros((), BF16)
    stack = lambda x: jnp.concatenate([jnp.where(lower_head, x, zero_bf), jnp.where(lower_head, zero_bf, x)], axis=0)
    causal = ops["causal_sub"]
    sl = _pair_slice(p)
    outs = []
    for h in range(rows // sub):
        rs = slice(h * sub, (h + 1) * sub)
        scores = _dot_nt(ops["q_mid"][rs, sl], stack(ops["k_mid"][rs, sl]))
        scores = jnp.where(causal, scores, 0.0).astype(BF16)
        outs.append(_dot(scores, stack(v[rs, sl])))
    return jnp.concatenate(outs, axis=0)


def _hgrn_contribs(ops, v):
    c = HGRN_CHUNK
    n_chunks = v.shape[0] // c
    return [[_dot_tn(v[n * c:(n + 1) * c, _pair_slice(p)], ops["k_end"][n * c:(n + 1) * c, _pair_slice(p)])
             for n in range(n_chunks)] for p in range(HGRN_PAIRS)]


def _hgrn_scan(ops, contribs, state_ref, reverse):
    n_chunks = len(contribs[0])
    sr = lax.broadcasted_iota(jnp.int32, (HGRN_PAIR, HGRN_PAIR), 0)
    sc = lax.broadcasted_iota(jnp.int32, (HGRN_PAIR, HGRN_PAIR), 1)
    same_head = (sr // HGRN_DIM) == (sc // HGRN_DIM)
    order = range(n_chunks - 1, -1, -1) if reverse else range(n_chunks)
    states = [[None] * n_chunks for _ in range(HGRN_PAIRS)]
    for p in range(HGRN_PAIRS):
        st = state_ref[p]
        for n in order:
            states[p][n] = st.astype(BF16)
            st = ops["decay"][n, :, _pair_slice(p)] * st + jnp.where(same_head, contribs[p][n], 0.0)
        state_ref[p] = st
    return states


def _hgrn_inter(ops, states, p):
    c = HGRN_CHUNK
    parts = [_dot_nt(ops["q_in"][n * c:(n + 1) * c, _pair_slice(p)], st) for n, st in enumerate(states[p])]
    return jnp.concatenate(parts, axis=0)


def _hgrn_kernel(lbl_ref, qf_ref, ff_ref, vf_ref, qb_ref, fb_ref, vb_ref, of_ref, ob_ref,
                 st_f_ref, st_b_ref):
    @pl.when(pl.program_id(1) == 0)
    def _():
        st_f_ref[...] = jnp.zeros_like(st_f_ref)
        st_b_ref[...] = jnp.zeros_like(st_b_ref)

    logits = lbl_ref[...]
    e = jnp.exp(logits - jnp.max(logits, axis=1, keepdims=True))
    lb = e[:, 0, :] / jnp.sum(e, axis=1)
    v_f, v_b = vf_ref[0], vb_ref[0]
    ops_f = _hgrn_prepare(qf_ref[0], ff_ref[0], lb[0:1], False)
    ops_b = _hgrn_prepare(qb_ref[0], fb_ref[0], lb[1:2], True)
    con_f = _hgrn_contribs(ops_f, v_f)
    con_b = _hgrn_contribs(ops_b, v_b)
    states_f = _hgrn_scan(ops_f, con_f, st_f_ref, False)
    states_b = _hgrn_scan(ops_b, con_b, st_b_ref, True)
    pairs = range(HGRN_PAIRS)
    inter_f = [_hgrn_inter(ops_f, states_f, p) for p in pairs]
    inter_b = [_hgrn_inter(ops_b, states_b, p) for p in pairs]
    intra_f = [_hgrn_intra(ops_f, v_f, p) for p in pairs]
    intra_b = [_hgrn_intra(ops_b, v_b, p) for p in pairs]
    of_ref[0] = jnp.concatenate([a + b for a, b in zip(intra_f, inter_f)], axis=1)
    ob_ref[0] = jnp.concatenate([a + b for a, b in zip(intra_b, inter_b)], axis=1)


def _hgrn(lb_logits, hq, hff, hfb, hi):
    b, s, w = hq.shape
    r = HGRN_ROWS
    assert s % r == 0 and r % HGRN_INTRA_ROWS == 0 and HGRN_INTRA_ROWS % HGRN_CHUNK == 0
    nb = s // r
    fwd = pl.BlockSpec((1, r, w), lambda bi, j: (bi, j, 0))
    bwd = pl.BlockSpec((1, r, w), lambda bi, j: (bi, nb - 1 - j, 0))
    state = pltpu.VMEM((HGRN_PAIRS, HGRN_PAIR, HGRN_PAIR), F32)
    return pl.pallas_call(
        _hgrn_kernel,
        grid=(b, nb),
        in_specs=[_resident(lb_logits.shape), fwd, fwd, fwd, bwd, bwd, bwd],
        out_specs=(fwd, bwd),
        out_shape=(jax.ShapeDtypeStruct((b, s, w), F32), jax.ShapeDtypeStruct((b, s, w), F32)),
        scratch_shapes=[state, state],
        compiler_params=_compiler_params(("parallel", "arbitrary")),
        name="hgrn",
    )(lb_logits, hq, hff, hi, hq, hfb, hi)


def _merge_kernel(x_ref, ya_ref, of_ref, ob_ref, hg_ref, mg_ref, win_hbm, ng_ref, gm_ref,
                  wa_hbm, wb_hbm, wo_hbm, o_ref, wg_ref, wa_ref, wb_ref, wo_ref, stage, sem):
    @pl.when(pl.program_id(0) == 0)
    def _():
        n_gate = wg_ref.shape[1]
        n_mix = win_hbm.shape[2] - n_gate
        _load_as_bf16(lambda r, n: win_hbm.at[0, pl.ds(r, n), pl.ds(n_mix, n_gate)], wg_ref.shape[0], wg_ref, stage, sem)
        for hbm, ref in ((wa_hbm, wa_ref), (wb_hbm, wb_ref), (wo_hbm, wo_ref)):
            _load_as_bf16(lambda r, n, hbm=hbm: hbm.at[0, pl.ds(r, n), :], ref.shape[0], ref, stage, sem)

    x = x_ref[...]
    d = x.shape[-1]
    h = _rmsnorm(x, mg_ref[...]).astype(BF16)
    gate_a = jax.nn.sigmoid(_dot(h, wg_ref[:, :d]))
    gate_b = jax.nn.sigmoid(_dot(h, wg_ref[:, d:]))
    o = of_ref[...] + ob_ref[...]
    on = o * lax.rsqrt(_group_mean(o * o, gm_ref[...]) + NORM_EPS) * ng_ref[...]
    yb = (on * hg_ref[...]).astype(BF16)
    merged = gate_a * _dot(ya_ref[...], wa_ref[...]) + gate_b * _dot(yb, wb_ref[...])
    o_ref[...] = x + _dot(merged.astype(BF16), wo_ref[...])


def _merge(x1, y_attn, o_f, o_b, hg, mix_g, w_in, norm_g, w_a, w_b, w_out):
    t, d = x1.shape
    tm = MERGE_ROWS
    assert t % tm == 0
    row = lambda width: pl.BlockSpec((tm, width), lambda i: (i, 0))
    hbm = pl.BlockSpec(memory_space=pl.ANY)
    return pl.pallas_call(
        _merge_kernel,
        grid=(t // tm,),
        in_specs=[row(d), row(ATTN_WIDTH), row(HGRN_WIDTH), row(HGRN_WIDTH), row(HGRN_WIDTH),
                  _resident((1, d)), hbm,
                  _resident((1, HGRN_WIDTH)), _resident((V7X_MXU_DIM, V7X_MXU_DIM)), hbm, hbm, hbm],
        out_specs=row(d),
        out_shape=jax.ShapeDtypeStruct((t, d), F32),
        scratch_shapes=[pltpu.VMEM((d, 2 * d), BF16), pltpu.VMEM((ATTN_WIDTH, d), BF16),
                        pltpu.VMEM((HGRN_WIDTH, d), BF16), pltpu.VMEM((d, d), BF16),
                        pltpu.VMEM((WEIGHT_STAGE_SLOTS, WEIGHT_STAGE_ROWS, 2 * d), F32),
                        pltpu.SemaphoreType.DMA((WEIGHT_STAGE_SLOTS,))],
        compiler_params=_compiler_params(("arbitrary",)),
        name="merge",
    )(x1, y_attn, o_f, o_b, hg, mix_g, w_in, norm_g, _group_matrix(V7X_MXU_DIM, HGRN_DIM), w_a, w_b, w_out)


def kernel(x, ffn1_norm_g, ffn1_w_gate, ffn1_w_up, ffn1_w_down, mix_norm_g, w_in, q_norm_g, k_norm_g,
           hgrn_lb_logits, hgrn_out_norm_g, w_branch_attn, w_branch_hgrn, w_out, ffn2_norm_g,
           ffn2_w_gate, ffn2_w_up, ffn2_w_down, final_norm_g):
    b, s, d = x.shape
    t = b * s
    depth = w_in.shape[0]
    assert depth == 1 and hgrn_lb_logits.shape[1] == depth + 1
    final_g = final_norm_g.reshape(1, d)
    x1 = _ffn(x.reshape(t, d), ffn1_norm_g, ffn1_w_gate[0], ffn1_w_up[0], ffn1_w_down[0], final_g, False)
    q, k, v, hq, hff, hfb, hi, hg = _in_proj(x1.reshape(b, s, d), mix_norm_g, w_in, q_norm_g, k_norm_g)
    y_attn = _attention(_scores_bounded(q_norm_g, k_norm_g), q, k, v)
    o_f, o_b = _hgrn(hgrn_lb_logits, hq, hff, hfb, hi)
    flat = lambda a: a.reshape(t, a.shape[-1])
    x2 = _merge(x1, flat(y_attn), flat(o_f), flat(o_b), flat(hg), mix_norm_g, w_in, hgrn_out_norm_g,
                w_branch_attn, w_branch_hgrn, w_out)
    out = _ffn(x2, ffn2_norm_g, ffn2_w_gate[0], ffn2_w_up[0], ffn2_w_down[0], final_g, True)
    return out.reshape(b, s, d)
```

```python
import functools

import jax
import jax.numpy as jnp
import numpy as np
from jax import lax
from jax.experimental import pallas as pl
from jax.experimental.pallas import tpu as pltpu

F32 = jnp.float32
BF16 = jnp.bfloat16

NORM_EPS = 1e-6
ROPE_THETA = 10000.0
GRID_W = 64
HEAD_DIM = 64
ATTN_HEADS = 8
ATTN_KV_HEADS = 2
ATTN_GROUP = ATTN_HEADS // ATTN_KV_HEADS
HGRN_HEADS = 8
HGRN_DIM = 64
HGRN_CHUNK = 32
ATTN_WIDTH = ATTN_HEADS * HEAD_DIM
KV_WIDTH = ATTN_KV_HEADS * HEAD_DIM
HGRN_WIDTH = HGRN_HEADS * HGRN_DIM
HGRN_PAIR = 2 * HGRN_DIM
HGRN_PAIRS = HGRN_HEADS // 2

V7X_LANES = 128
V7X_MXU_DIM = 256
V7X_VMEM_LIMIT_BYTES = 56 * 1024 * 1024

FFN_ROWS = 1024
FFN_SUB_ROWS = 512
FFN_ANCHOR_CHUNK = 6
FFN_COLS = V7X_MXU_DIM
PROJ_ROWS = 1024
ATTN_Q_ROWS = 512
ATTN_KEY_SPLIT = 4
ATTN_SCORE_BOUND = 60.0
HGRN_ROWS = 256
HGRN_INTRA_ROWS = V7X_MXU_DIM // 2
MERGE_ROWS = 1024
WEIGHT_STAGE_ROWS = 128
WEIGHT_STAGE_SLOTS = 4


def _compiler_params(semantics):
    return pltpu.CompilerParams(dimension_semantics=semantics, vmem_limit_bytes=V7X_VMEM_LIMIT_BYTES)


def _resident(shape):
    zeros = (0,) * len(shape)
    return pl.BlockSpec(shape, lambda *_: zeros, pipeline_mode=pl.Buffered(1))


def _rmsnorm(x, g):
    return x * lax.rsqrt(jnp.mean(x * x, axis=-1, keepdims=True) + NORM_EPS) * g


def _dot(a, b):
    return jnp.dot(a, b, preferred_element_type=F32)


def _dot_nt(a, b):
    return lax.dot_general(a, b, (((1,), (1,)), ((), ())), preferred_element_type=F32)


def _dot_tn(a, b):
    return lax.dot_general(a, b, (((0,), (0,)), ((), ())), preferred_element_type=F32)


def _split2(x):
    hi = x.astype(BF16)
    lo = (x - hi.astype(F32)).astype(BF16)
    return hi, lo


def _split3(x):
    hi = x.astype(BF16)
    r = x - hi.astype(F32)
    mid = r.astype(BF16)
    lo = (r - mid.astype(F32)).astype(BF16)
    return hi, mid, lo


def _group_mean(x, gmat):
    w = gmat.shape[0]
    hi, lo = _split2(x)
    tiles = [_dot(hi[:, i:i + w], gmat) + _dot(lo[:, i:i + w], gmat) for i in range(0, x.shape[1], w)]
    return tiles[0] if len(tiles) == 1 else jnp.concatenate(tiles, axis=1)


def _load_as_bf16(src_rows, n_rows, dst, stage, sem):
    depth, chunk, _ = stage.shape
    cols = dst.shape[1]
    n = n_rows // chunk
    assert n_rows % chunk == 0 and cols <= stage.shape[2]

    def copy(k):
        slot = k % depth
        return pltpu.make_async_copy(src_rows(k * chunk, chunk), stage.at[slot, :, pl.ds(0, cols)], sem.at[slot])

    for k in range(min(depth - 1, n)):
        copy(k).start()
    for k in range(n):
        if k + depth - 1 < n:
            copy(k + depth - 1).start()
        copy(k).wait()
        dst[pl.ds(k * chunk, chunk), :] = stage[k % depth, :, :cols].astype(BF16)


def _ffn_kernel(x_ref, g_ref, wg_hbm, wu_hbm, wd_hbm, fg_ref, o_ref,
                wg_ref, wu_ref, wd_ref, stage_g, stage_u, stage_d, sem, *, final_norm):
    d_ff = wg_ref.shape[1]
    n_chunks = d_ff // FFN_COLS

    def chunk_copies(c):
        cols = pl.ds(c * FFN_COLS, FFN_COLS)
        slot = c % 2
        return (pltpu.make_async_copy(wg_hbm.at[:, cols], stage_g.at[slot], sem.at[0, slot]),
                pltpu.make_async_copy(wu_hbm.at[:, cols], stage_u.at[slot], sem.at[1, slot]),
                pltpu.make_async_copy(wd_hbm.at[cols, :], stage_d.at[slot], sem.at[2, slot]))

    def tile(stream_weights):
        if stream_weights:
            for cp in chunk_copies(0):
                cp.start()
        anchor = None
        for half in range(FFN_ROWS // FFN_SUB_ROWS):
            rows = slice(half * FFN_SUB_ROWS, (half + 1) * FFN_SUB_ROWS)
            x = x_ref[rows, :]
            xn = _rmsnorm(x, g_ref[...]).astype(BF16)
            acc = None
            for c in range(n_chunks):
                sl = slice(c * FFN_COLS, (c + 1) * FFN_COLS)
                if stream_weights and half == 0:
                    if c + 1 < n_chunks:
                        for cp in chunk_copies(c + 1):
                            cp.start()
                    for cp in chunk_copies(c):
                        cp.wait()
                    wg_ref[:, sl] = stage_g[c % 2].astype(BF16)
                    wu_ref[:, sl] = stage_u[c % 2].astype(BF16)
                    wd_ref[sl, :] = stage_d[c % 2].astype(BF16)
                if anchor is not None and c == FFN_ANCHOR_CHUNK:
                    xn = xn + anchor
                gate = _dot(xn, wg_ref[:, sl])
                up = _dot(xn, wu_ref[:, sl])
                act = (jax.nn.silu(gate) * up).astype(BF16)
                part = _dot(act, wd_ref[sl, :])
                acc = part if acc is None else acc + part
            y = x + 0.5 * acc
            if final_norm:
                y = _rmsnorm(y, fg_ref[...])
            o_ref[rows, :] = y
            anchor = (y[0:1, :] * 0.0).astype(BF16)

    pl.when(pl.program_id(0) == 0)(functools.partial(tile, True))
    pl.when(pl.program_id(0) != 0)(functools.partial(tile, False))


def _ffn(x, norm_g, w_gate, w_up, w_down, final_g, final_norm):
    t, d = x.shape
    d_ff = w_gate.shape[1]
    assert t % FFN_ROWS == 0 and d_ff % FFN_COLS == 0
    row = pl.BlockSpec((FFN_ROWS, d), lambda i: (i, 0))
    hbm = pl.BlockSpec(memory_space=pl.ANY)
    return pl.pallas_call(
        functools.partial(_ffn_kernel, final_norm=final_norm),
        grid=(t // FFN_ROWS,),
        in_specs=[row, _resident((1, d)), hbm, hbm, hbm, _resident((1, d))],
        out_specs=row,
        out_shape=jax.ShapeDtypeStruct((t, d), F32),
        scratch_shapes=[pltpu.VMEM((d, d_ff), BF16), pltpu.VMEM((d, d_ff), BF16), pltpu.VMEM((d_ff, d), BF16),
                        pltpu.VMEM((2, d, FFN_COLS), F32), pltpu.VMEM((2, d, FFN_COLS), F32),
                        pltpu.VMEM((2, FFN_COLS, d), F32), pltpu.SemaphoreType.DMA((3, 2))],
        compiler_params=_compiler_params(("arbitrary",)),
        name="ffn_final" if final_norm else "ffn",
    )(x, norm_g, w_gate, w_up, w_down, final_g)


def _proj_kernel(x_ref, g_ref, w_hbm, qgt_ref, kgt_ref, cost_ref, sint_ref,
                 q_ref, k_ref, v_ref, hq_ref, ff_ref, fb_ref, hi_ref, hg_ref, w_ref, stage, sem, wqvt_ref):
    @pl.when((pl.program_id(0) == 0) & (pl.program_id(1) == 0))
    def _():
        n_in = w_ref.shape[1]
        _load_as_bf16(lambda r, n: w_hbm.at[0, pl.ds(r, n), pl.ds(0, n_in)], w_ref.shape[0], w_ref, stage, sem)
        v0 = ATTN_WIDTH + KV_WIDTH
        wqvt_ref[:v0, :] = w_ref[:, :v0].astype(F32).T.astype(BF16)
        wqvt_ref[v0:, :] = w_ref[:, v0:v0 + KV_WIDTH].astype(F32).T.astype(BF16)

    h = _rmsnorm(x_ref[0], g_ref[...]).astype(BF16)
    rows = h.shape[0]

    def cols(start, width):
        return _dot(h, w_ref[:, start:start + width])

    qv_t = _dot_nt(wqvt_ref[...], h)
    q3 = qv_t[:ATTN_WIDTH].reshape(ATTN_HEADS, HEAD_DIM, rows)
    qn = q3 * lax.rsqrt(jnp.mean(q3 * q3, axis=1, keepdims=True) + NORM_EPS) * qgt_ref[...]
    half = HEAD_DIM // 2
    cos_t, sin_t = cost_ref[...], sint_ref[...]

    def rope_t(x):
        a, b = x[:, :half], x[:, half:]
        return jnp.concatenate([a * cos_t - b * sin_t, b * cos_t + a * sin_t], axis=1)

    q_ref[0] = (rope_t(qn) * (HEAD_DIM ** -0.5)).astype(BF16)
    k3 = qv_t[ATTN_WIDTH:ATTN_WIDTH + KV_WIDTH].reshape(ATTN_KV_HEADS, HEAD_DIM, rows)
    kn = k3 * lax.rsqrt(jnp.mean(k3 * k3, axis=1, keepdims=True) + NORM_EPS) * kgt_ref[...]
    kr = rope_t(kn).reshape(KV_WIDTH, rows).T
    off = ATTN_WIDTH + 2 * KV_WIDTH
    ones = jnp.ones((HEAD_DIM, rows), F32)
    for hd in range(ATTN_KV_HEADS):
        k_ref[0, hd] = kr[:, hd * HEAD_DIM:(hd + 1) * HEAD_DIM].astype(BF16)
        v_hd = qv_t[ATTN_WIDTH + KV_WIDTH + hd * HEAD_DIM:ATTN_WIDTH + KV_WIDTH + (hd + 1) * HEAD_DIM]
        v_ref[0, hd] = jnp.concatenate([v_hd, ones], axis=0).astype(BF16)
    hq_ref[0] = jax.nn.silu(cols(off, HGRN_WIDTH))
    off += HGRN_WIDTH
    ff_ref[0] = cols(off, HGRN_WIDTH)
    off += HGRN_WIDTH
    fb_ref[0] = cols(off, HGRN_WIDTH)
    off += HGRN_WIDTH
    hi_ref[0] = cols(off, HGRN_WIDTH).astype(BF16)
    off += HGRN_WIDTH
    hg_ref[0] = jax.nn.silu(cols(off, HGRN_WIDTH))


def _group_matrix(width, group):
    idx = np.arange(width) // group
    return jnp.asarray((idx[:, None] == idx[None, :]).astype(np.float32) / group, dtype=BF16)


def _rope_tables(seq_len):
    f32 = np.float32
    rows = seq_len // GRID_W
    row = np.repeat(np.arange(rows, dtype=f32), GRID_W)
    col = np.tile(np.arange(GRID_W, dtype=f32), rows)
    n_freq = HEAD_DIM // 4
    inv = (f32(ROPE_THETA) ** (-np.arange(n_freq, dtype=f32) / f32(n_freq))).astype(f32)
    ang = np.concatenate([row[:, None] * inv, col[:, None] * inv], axis=-1)
    cos, sin = np.cos(ang).astype(f32), np.sin(ang).astype(f32)
    return jnp.asarray(np.ascontiguousarray(cos.T)), jnp.asarray(np.ascontiguousarray(sin.T))


def _in_proj(x1, norm_g, w_in, q_g, k_g):
    b, s, d = x1.shape
    n_in = ATTN_WIDTH + 2 * KV_WIDTH + 5 * HGRN_WIDTH
    assert s % PROJ_ROWS == 0 and w_in.shape[2] == n_in + 2 * d
    cos_t, sin_t = _rope_tables(s)
    tm = PROJ_ROWS
    tok = lambda width: pl.BlockSpec((1, tm, width), lambda bi, i: (bi, i, 0))
    heads = lambda n, width=HEAD_DIM: pl.BlockSpec((1, n, tm, width), lambda bi, i: (bi, 0, i, 0))
    pos_t = pl.BlockSpec((HEAD_DIM // 2, tm), lambda bi, i: (0, i))
    f32_tok = lambda width: jax.ShapeDtypeStruct((b, s, width), F32)
    out_shape = (
        jax.ShapeDtypeStruct((b, ATTN_HEADS, HEAD_DIM, s), BF16),
        jax.ShapeDtypeStruct((b, ATTN_KV_HEADS, s, HEAD_DIM), BF16),
        jax.ShapeDtypeStruct((b, ATTN_KV_HEADS, V7X_LANES, s), BF16),
        f32_tok(HGRN_WIDTH), f32_tok(HGRN_WIDTH), f32_tok(HGRN_WIDTH),
        jax.ShapeDtypeStruct((b, s, HGRN_WIDTH), BF16),
        f32_tok(HGRN_WIDTH),
    )
    vt_spec = pl.BlockSpec((1, ATTN_KV_HEADS, V7X_LANES, tm), lambda bi, i: (bi, 0, 0, i))
    qt_spec = pl.BlockSpec((1, ATTN_HEADS, HEAD_DIM, tm), lambda bi, i: (bi, 0, 0, i))
    out_specs = (qt_spec, heads(ATTN_KV_HEADS), vt_spec,
                 tok(HGRN_WIDTH), tok(HGRN_WIDTH), tok(HGRN_WIDTH), tok(HGRN_WIDTH), tok(HGRN_WIDTH))
    return pl.pallas_call(
        _proj_kernel,
        grid=(b, s // tm),
        in_specs=[tok(d), _resident((1, d)), pl.BlockSpec(memory_space=pl.ANY),
                  _resident((HEAD_DIM, tm)), _resident((HEAD_DIM, tm)), pos_t, pos_t],
        out_specs=out_specs,
        out_shape=out_shape,
        scratch_shapes=[pltpu.VMEM((d, n_in), BF16), pltpu.VMEM((WEIGHT_STAGE_SLOTS, WEIGHT_STAGE_ROWS, n_in), F32),
                        pltpu.SemaphoreType.DMA((WEIGHT_STAGE_SLOTS,)),
                        pltpu.VMEM((ATTN_WIDTH + 2 * KV_WIDTH, d), BF16)],
        compiler_params=_compiler_params(("arbitrary", "arbitrary")),
        name="in_proj",
    )(x1, norm_g, w_in, jnp.broadcast_to(q_g.reshape(HEAD_DIM, 1), (HEAD_DIM, tm)),
      jnp.broadcast_to(k_g.reshape(HEAD_DIM, 1), (HEAD_DIM, tm)), cos_t, sin_t)


def _attn_kernel(bounded_ref, q_ref, k_ref, vt_ref, o_ref):
    tq = q_ref.shape[3]
    s_len = k_ref.shape[2]
    q_t = jnp.concatenate([q_ref[0, r] for r in range(ATTN_GROUP)], axis=1)

    def finish(accs):
        norm = [a[:HEAD_DIM] * (1.0 / a[HEAD_DIM:HEAD_DIM + 1]) for a in accs]
        pairs = [jnp.concatenate(norm[r:r + 2], axis=0).T for r in range(0, ATTN_GROUP, 2)]
        o_ref[0] = jnp.concatenate(pairs, axis=1).astype(BF16)

    def attend(shifted):
        kc = s_len // ATTN_KEY_SPLIT
        chunks = [slice(c * kc, (c + 1) * kc) for c in range(ATTN_KEY_SPLIT)]
        if shifted:
            s_t = _dot(k_ref[0, 0], q_t)
            p_full = jnp.exp(s_t - jnp.max(s_t, axis=0, keepdims=True)).astype(BF16)
            p_chunks = [p_full[ks] for ks in chunks]
        else:
            p_chunks = [jnp.exp(_dot(k_ref[0, 0, ks, :], q_t)).astype(BF16) for ks in chunks]
        accs = [None] * ATTN_GROUP
        for ks, p_t in zip(chunks, p_chunks):
            vt_c = vt_ref[0, 0, :, ks]
            for r in range(ATTN_GROUP):
                part = _dot(vt_c, p_t[:, r * tq:(r + 1) * tq])
                accs[r] = part if accs[r] is None else accs[r] + part
        finish(accs)

    pl.when(bounded_ref[0] == 1)(functools.partial(attend, False))
    pl.when(bounded_ref[0] != 1)(functools.partial(attend, True))


def _attention(bounded, q, k, v):
    b, _, _, s = q.shape
    tq = ATTN_Q_ROWS
    assert s % tq == 0 and s % (ATTN_KEY_SPLIT * V7X_LANES) == 0
    k_spec = pl.BlockSpec((1, 1, s, HEAD_DIM), lambda bi, g, i, *_: (bi, g, 0, 0))
    vt_spec = pl.BlockSpec((1, 1, V7X_LANES, s), lambda bi, g, i, *_: (bi, g, 0, 0))
    grid_spec = pltpu.PrefetchScalarGridSpec(
        num_scalar_prefetch=1,
        grid=(b, ATTN_KV_HEADS, s // tq),
        in_specs=[pl.BlockSpec((1, ATTN_GROUP, HEAD_DIM, tq), lambda bi, g, i, *_: (bi, g, 0, i)),
                  k_spec, vt_spec],
        out_specs=pl.BlockSpec((1, tq, ATTN_GROUP * HEAD_DIM), lambda bi, g, i, *_: (bi, i, g)),
    )
    return pl.pallas_call(
        _attn_kernel,
        grid_spec=grid_spec,
        out_shape=jax.ShapeDtypeStruct((b, s, ATTN_WIDTH), BF16),
        compiler_params=_compiler_params(("parallel", "parallel", "parallel")),
        name="attention",
    )(bounded, q, k, v)


def _scores_bounded(q_g, k_g):
    bound = (HEAD_DIM ** 0.5) * jnp.max(jnp.abs(q_g)) * jnp.max(jnp.abs(k_g))
    return (bound <= ATTN_SCORE_BOUND).astype(jnp.int32).reshape(1)


def _hgrn_prepare(qt, f_logit, lb, reverse):
    rows = qt.shape[0]
    c = HGRN_CHUNK
    n_chunks = rows // c
    f = lb + (1.0 - lb) * jax.nn.sigmoid(f_logit)
    kk = 1.0 - f
    log_f = jnp.log(f)
    r_i = lax.broadcasted_iota(jnp.int32, (rows, rows), 0)
    c_i = lax.broadcasted_iota(jnp.int32, (rows, rows), 1)
    same_chunk = (r_i // c) == (c_i // c)
    causal = same_chunk & ((c_i >= r_i) if reverse else (c_i <= r_i))
    tri = jnp.where(causal, 1.0, 0.0).astype(BF16)
    hi, mid, lo = _split3(log_f)
    bcum = _dot(tri, hi) + _dot(tri, mid) + _dot(tri, lo)
    b3 = bcum.reshape(n_chunks, c, bcum.shape[-1])
    mid_row = c // 2 if reverse else c // 2 - 1
    last_row = 0 if reverse else c - 1
    b_mid = b3[:, mid_row:mid_row + 1, :]
    b_last = b3[:, last_row:last_row + 1, :]
    q3 = qt.reshape(b3.shape)
    k3 = kk.reshape(b3.shape)
    d_mid = b3 - b_mid
    q_mid = q3 * jnp.exp(d_mid)
    k_mid = k3 * jnp.exp(-d_mid)
    k_end = k_mid * jnp.exp(b_last - b_mid)
    q_in = q_mid * jnp.exp(b_mid)
    flat = lambda a: a.reshape(bcum.shape).astype(BF16)
    sub = HGRN_INTRA_ROWS
    causal_sub = jnp.concatenate([causal[:sub, :sub]] * 2, axis=1)
    return dict(causal_sub=causal_sub, q_mid=flat(q_mid), k_mid=flat(k_mid), k_end=flat(k_end), q_in=flat(q_in),
                decay=jnp.exp(b_last))


def _pair_slice(p):
    return slice(p * HGRN_PAIR, (p + 1) * HGRN_PAIR)


def _hgrn_intra(ops, v, p):
    rows = v.shape[0]
    sub = HGRN_INTRA_ROWS
    lower_head = lax.broadcasted_iota(jnp.int32, (sub, HGRN_PAIR), 1) < HGRN_DIM
    zero_bf = jnp.zeros((), BF16)
    stack = lambda x: jnp.concatenate([jnp.where(lower_head, x, zero_bf), jnp.where(lower_head, zero_bf, x)], axis=0)
    causal = ops["causal_sub"]
    sl = _pair_slice(p)
    outs = []
    for h in range(rows // sub):
        rs = slice(h * sub, (h + 1) * sub)
        scores = _dot_nt(ops["q_mid"][rs, sl], stack(ops["k_mid"][rs, sl]))
        scores = jnp.where(causal, scores, 0.0).astype(BF16)
        outs.append(_dot(scores, stack(v[rs, sl])))
    return jnp.concatenate(outs, axis=0)


def _hgrn_contribs(ops, v):
    c = HGRN_CHUNK
    n_chunks = v.shape[0] // c
    return [[_dot_tn(v[n * c:(n + 1) * c, _pair_slice(p)], ops["k_end"][n * c:(n + 1) * c, _pair_slice(p)])
             for n in range(n_chunks)] for p in range(HGRN_PAIRS)]


def _hgrn_scan(ops, contribs, state_ref, reverse):
    n_chunks = len(contribs[0])
    sr = lax.broadcasted_iota(jnp.int32, (HGRN_PAIR, HGRN_PAIR), 0)
    sc = lax.broadcasted_iota(jnp.int32, (HGRN_PAIR, HGRN_PAIR), 1)
    same_head = (sr // HGRN_DIM) == (sc // HGRN_DIM)
    order = range(n_chunks - 1, -1, -1) if reverse else range(n_chunks)
    states = [[None] * n_chunks for _ in range(HGRN_PAIRS)]
    for p in range(HGRN_PAIRS):
        st = state_ref[p]
        for n in order:
            states[p][n] = st.astype(BF16)
            st = ops["decay"][n, :, _pair_slice(p)] * st + jnp.where(same_head, contribs[p][n], 0.0)
        state_ref[p] = st
    return states


def _hgrn_inter(ops, states, p):
    c = HGRN_CHUNK
    parts = [_dot_nt(ops["q_in"][n * c:(n + 1) * c, _pair_slice(p)], st) for n, st in enumerate(states[p])]
    return jnp.concatenate(parts, axis=0)


def _hgrn_kernel(lbl_ref, qf_ref, ff_ref, vf_ref, qb_ref, fb_ref, vb_ref, of_ref, ob_ref,
                 st_f_ref, st_b_ref):
    @pl.when(pl.program_id(1) == 0)
    def _():
        st_f_ref[...] = jnp.zeros_like(st_f_ref)
        st_b_ref[...] = jnp.zeros_like(st_b_ref)

    logits = lbl_ref[...]
    e = jnp.exp(logits - jnp.max(logits, axis=1, keepdims=True))
    lb = e[:, 0, :] / jnp.sum(e, axis=1)
    v_f, v_b = vf_ref[0], vb_ref[0]
    ops_f = _hgrn_prepare(qf_ref[0], ff_ref[0], lb[0:1], False)
    ops_b = _hgrn_prepare(qb_ref[0], fb_ref[0], lb[1:2], True)
    con_f = _hgrn_contribs(ops_f, v_f)
    con_b = _hgrn_contribs(ops_b, v_b)
    states_f = _hgrn_scan(ops_f, con_f, st_f_ref, False)
    states_b = _hgrn_scan(ops_b, con_b, st_b_ref, True)
    pairs = range(HGRN_PAIRS)
    inter_f = [_hgrn_inter(ops_f, states_f, p) for p in pairs]
    inter_b = [_hgrn_inter(ops_b, states_b, p) for p in pairs]
    intra_f = [_hgrn_intra(ops_f, v_f, p) for p in pairs]
    intra_b = [_hgrn_intra(ops_b, v_b, p) for p in pairs]
    of_ref[0] = jnp.concatenate([a + b for a, b in zip(intra_f, inter_f)], axis=1)
    ob_ref[0] = jnp.concatenate([a + b for a, b in zip(intra_b, inter_b)], axis=1)


def _hgrn(lb_logits, hq, hff, hfb, hi):
    b, s, w = hq.shape
    r = HGRN_ROWS
    assert s % r == 0 and r % HGRN_INTRA_ROWS == 0 and HGRN_INTRA_ROWS % HGRN_CHUNK == 0
    nb = s // r
    fwd = pl.BlockSpec((1, r, w), lambda bi, j: (bi, j, 0))
    bwd = pl.BlockSpec((1, r, w), lambda bi, j: (bi, nb - 1 - j, 0))
    state = pltpu.VMEM((HGRN_PAIRS, HGRN_PAIR, HGRN_PAIR), F32)
    return pl.pallas_call(
        _hgrn_kernel,
        grid=(b, nb),
        in_specs=[_resident(lb_logits.shape), fwd, fwd, fwd, bwd, bwd, bwd],
        out_specs=(fwd, bwd),
        out_shape=(jax.ShapeDtypeStruct((b, s, w), F32), jax.ShapeDtypeStruct((b, s, w), F32)),
        scratch_shapes=[state, state],
        compiler_params=_compiler_params(("parallel", "arbitrary")),
        name="hgrn",
    )(lb_logits, hq, hff, hi, hq, hfb, hi)


def _merge_kernel(x_ref, ya_ref, of_ref, ob_ref, hg_ref, mg_ref, win_hbm, ng_ref, gm_ref,
                  wa_hbm, wb_hbm, wo_hbm, o_ref, wg_ref, wa_ref, wb_ref, wo_ref, stage, sem):
    @pl.when(pl.program_id(0) == 0)
    def _():
        n_gate = wg_ref.shape[1]
        n_mix = win_hbm.shape[2] - n_gate
        _load_as_bf16(lambda r, n: win_hbm.at[0, pl.ds(r, n), pl.ds(n_mix, n_gate)], wg_ref.shape[0], wg_ref, stage, sem)
        for hbm, ref in ((wa_hbm, wa_ref), (wb_hbm, wb_ref), (wo_hbm, wo_ref)):
            _load_as_bf16(lambda r, n, hbm=hbm: hbm.at[0, pl.ds(r, n), :], ref.shape[0], ref, stage, sem)

    x = x_ref[...]
    d = x.shape[-1]
    h = _rmsnorm(x, mg_ref[...]).astype(BF16)
    gate_a = jax.nn.sigmoid(_dot(h, wg_ref[:, :d]))
    gate_b = jax.nn.sigmoid(_dot(h, wg_ref[:, d:]))
    o = of_ref[...] + ob_ref[...]
    on = o * lax.rsqrt(_group_mean(o * o, gm_ref[...]) + NORM_EPS) * ng_ref[...]
    yb = (on * hg_ref[...]).astype(BF16)
    merged = gate_a * _dot(ya_ref[...], wa_ref[...]) + gate_b * _dot(yb, wb_ref[...])
    o_ref[...] = x + _dot(merged.astype(BF16), wo_ref[...])


def _merge(x1, y_attn, o_f, o_b, hg, mix_g, w_in, norm_g, w_a, w_b, w_out):
    t, d = x1.shape
    tm = MERGE_ROWS
    assert t % tm == 0
    row = lambda width: pl.BlockSpec((tm, width), lambda i: (i, 0))
    hbm = pl.BlockSpec(memory_space=pl.ANY)
    return pl.pallas_call(
        _merge_kernel,
        grid=(t // tm,),
        in_specs=[row(d), row(ATTN_WIDTH), row(HGRN_WIDTH), row(HGRN_WIDTH), row(HGRN_WIDTH),
                  _resident((1, d)), hbm,
                  _resident((1, HGRN_WIDTH)), _resident((V7X_MXU_DIM, V7X_MXU_DIM)), hbm, hbm, hbm],
        out_specs=row(d),
        out_shape=jax.ShapeDtypeStruct((t, d), F32),
        scratch_shapes=[pltpu.VMEM((d, 2 * d), BF16), pltpu.VMEM((ATTN_WIDTH, d), BF16),
                        pltpu.VMEM((HGRN_WIDTH, d), BF16), pltpu.VMEM((d, d), BF16),
                        pltpu.VMEM((WEIGHT_STAGE_SLOTS, WEIGHT_STAGE_ROWS, 2 * d), F32),
                        pltpu.SemaphoreType.DMA((WEIGHT_STAGE_SLOTS,))],
        compiler_params=_compiler_params(("arbitrary",)),
        name="merge",
    )(x1, y_attn, o_f, o_b, hg, mix_g, w_in, norm_g, _group_matrix(V7X_MXU_DIM, HGRN_DIM), w_a, w_b, w_out)


def kernel(x, ffn1_norm_g, ffn1_w_gate, ffn1_w_up, ffn1_w_down, mix_norm_g, w_in, q_norm_g, k_norm_g,
           hgrn_lb_logits, hgrn_out_norm_g, w_branch_attn, w_branch_hgrn, w_out, ffn2_norm_g,
           ffn2_w_gate, ffn2_w_up, ffn2_w_down, final_norm_g):
    b, s, d = x.shape
    t = b * s
    depth = w_in.shape[0]
    assert depth == 1 and hgrn_lb_logits.shape[1] == depth + 1
    final_g = final_norm_g.reshape(1, d)
    x1 = _ffn(x.reshape(t, d), ffn1_norm_g, ffn1_w_gate[0], ffn1_w_up[0], ffn1_w_down[0], final_g, False)
    q, k, v, hq, hff, hfb, hi, hg = _in_proj(x1.reshape(b, s, d), mix_norm_g, w_in, q_norm_g, k_norm_g)
    y_attn = _attention(_scores_bounded(q_norm_g, k_norm_g), q, k, v)
    o_f, o_b = _hgrn(hgrn_lb_logits, hq, hff, hfb, hi)
    flat = lambda a: a.reshape(t, a.shape[-1])
    x2 = _merge(x1, flat(y_attn), flat(o_f), flat(o_b), flat(hg), mix_norm_g, w_in, hgrn_out_norm_g,
                w_branch_attn, w_branch_hgrn, w_out)
    out = _ffn(x2, ffn2_norm_g, ffn2_w_gate[0], ffn2_w_up[0], ffn2_w_down[0], final_g, True)
    return out.reshape(b, s, d)
```
